```python
import math
import jax, jax.numpy as jnp
from jax import lax
import numpy as np

D_MODEL = 1024
BATCH = 8
SEQ = 2048
DEPTH = 2

GRID_W = 64
CTX_LEN = 256
EPS = 1e-6
DA_HEADS = 4
DA_QK = 64
DA_V = 2 * DA_QK
DA_WIDTH = DA_HEADS * DA_V
ROPE_BASE = 10000.0
Q_BLOCK = 128
ML_HEADS = 4
ML_DH = 128
ML_WIDTH = ML_HEADS * ML_DH
ML_CHUNK = 128
SHORT_CONV = 3
CV_WIDTH = 512
N_BRANCH = 3
D_IN = 3 * DA_WIDTH + 4 * ML_WIDTH + 4 * ML_HEADS + 3 * CV_WIDTH + N_BRANCH * D_MODEL
PEER_HEADS = 8
PEER_NKEYS = 128
PEER_N = PEER_NKEYS * PEER_NKEYS
PEER_DKEY = 256
PEER_TOPK = 16
PEER_BLOCK = 128

kernel_name = "hybrid_diffattn_mlstm_conv_peer_dit"


def rmsnorm(x, g):
    x32 = x.astype(jnp.float32)
    y = x32 * lax.rsqrt(jnp.mean(x32 * x32, axis=-1, keepdims=True) + EPS)
    return (y * g.astype(jnp.float32)).astype(x.dtype)


def modulate(h, shift, scale):
    return h * (1 + scale) + shift


def split_cols(z):
    sizes = [DA_WIDTH, DA_WIDTH, DA_WIDTH, 2 * ML_WIDTH, ML_WIDTH, ML_WIDTH, 4 * ML_HEADS,
             CV_WIDTH, CV_WIDTH, CV_WIDTH, N_BRANCH * D_MODEL]
    offs = np.cumsum(sizes)[:-1].tolist()
    return jnp.split(z, offs, axis=-1)


def conv3(x, w):
    ch = x.shape[-1]
    return lax.conv_general_dilated(x, w[:, None, :].astype(x.dtype), window_strides=(1,),
                                    padding=((SHORT_CONV // 2, SHORT_CONV // 2),),
                                    dimension_numbers=('NWC', 'WIO', 'NWC'),
                                    feature_group_count=ch)


def axial_rope(x, row, col):
    quarter = DA_QK // 4
    inv = ROPE_BASE ** (-jnp.arange(quarter, dtype=jnp.float32) / quarter)

    def rot(xp, pos):
        ang = pos.astype(jnp.float32)[:, None] * inv[None, :]
        cos = jnp.cos(ang)[None, :, None, None, :].astype(x.dtype)
        sin = jnp.sin(ang)[None, :, None, None, :].astype(x.dtype)
        x1, x2 = jnp.split(xp, 2, axis=-1)
        return jnp.concatenate([x1 * cos - x2 * sin, x1 * sin + x2 * cos], axis=-1)

    xr, xc = jnp.split(x, 2, axis=-1)
    return jnp.concatenate([rot(xr, row), rot(xc, col)], axis=-1)


def diff_attention(qc, kc, vc, ql, kl, vl, lam, lam_init, subln_g, need_ctx):
    B, S, _ = ql.shape
    rows = S // GRID_W
    row = jnp.repeat(jnp.arange(rows), GRID_W)
    col = jnp.tile(jnp.arange(GRID_W), rows)

    def heads_qk(a):
        return a.reshape(a.shape[0], a.shape[1], DA_HEADS, 2, DA_QK)

    def heads_v(a):
        return a.reshape(a.shape[0], a.shape[1], DA_HEADS, DA_V)

    ql = axial_rope(heads_qk(ql), row, col)
    kl = axial_rope(heads_qk(kl), row, col)
    qc, kc, vc = heads_qk(qc), heads_qk(kc), heads_v(vc)
    k_all = jnp.concatenate([kc, kl], axis=1)
    v_all = jnp.concatenate([vc, heads_v(vl)], axis=1)
    scale = DA_QK ** -0.5

    def attend(q, k, v):
        s = jnp.einsum('bqhmd,bkhmd->bhmqk', q, k).astype(jnp.float32) * scale
        a = jax.nn.softmax(s, axis=-1)
        w = (a[:, :, 0] - lam * a[:, :, 1]).astype(v.dtype)
        return jnp.einsum('bhqk,bkhe->bqhe', w, v)

    def finish(o):
        T = o.shape[1]
        o = rmsnorm(o, subln_g) * (1.0 - lam_init)
        return o.reshape(o.shape[0], T, DA_WIDTH)

    nb = S // Q_BLOCK
    qb = jnp.moveaxis(ql.reshape(B, nb, Q_BLOCK, DA_HEADS, 2, DA_QK), 1, 0)
    ol = lax.map(lambda q: attend(q, k_all, v_all), qb)
    ol = finish(jnp.moveaxis(ol, 0, 1).reshape(B, S, DA_HEADS, DA_V))
    oc = finish(attend(qc, kc, vc)) if need_ctx else None
    return oc, ol


def mlstm_chunk(carry, inp):
    C, n, m = carry
    q, k, v, ig, lf = inp
    L = q.shape[2]
    b = jnp.cumsum(lf, axis=-1)
    seen = jnp.tril(jnp.ones((L, L), dtype=bool))
    log_d = jnp.where(seen, b[..., :, None] - b[..., None, :] + ig[..., None, :], -jnp.inf)
    m_path = b + m[..., None]
    m_t = jnp.maximum(m_path, jnp.max(log_d, axis=-1))
    d = jnp.exp(log_d - m_t[..., None])
    carry_w = jnp.exp(m_path - m_t)
    w = jnp.einsum('bhtd,bhsd->bhts', q, k) * d
    num = jnp.einsum('bhts,bhse->bhte', w, v) + carry_w[..., None] * jnp.einsum('bhtd,bhde->bhte', q, C)
    den = jnp.sum(w, axis=-1) + carry_w * jnp.einsum('bhtd,bhd->bht', q, n)
    h = num / jnp.maximum(jnp.abs(den), jnp.exp(-m_t))[..., None]
    b_end = b[..., -1]
    log_w = b_end[..., None] - b + ig
    m_new = jnp.maximum(b_end + m, jnp.max(log_w, axis=-1))
    wk = jnp.exp(log_w - m_new[..., None])
    decay = jnp.exp(b_end + m - m_new)
    C_new = decay[..., None, None] * C + jnp.einsum('bhs,bhsd,bhse->bhde', wk, k, v)
    n_new = decay[..., None] * n + jnp.einsum('bhs,bhsd->bhd', wk, k)
    return (C_new, n_new, m_new), h


def mlstm_scan(q, k, v, ig, lf, state):
    B, H, T, d = q.shape
    nc = T // ML_CHUNK

    def to_chunks(a):
        return jnp.moveaxis(a.reshape(a.shape[:2] + (nc, ML_CHUNK) + a.shape[3:]), 2, 0)

    state, h = lax.scan(mlstm_chunk, state, (to_chunks(q), to_chunks(k), to_chunks(v), to_chunks(ig), to_chunks(lf)))
    h = jnp.moveaxis(h, 0, 2).reshape(B, H, T, d)
    return h, state


def mlstm_mixer(qk_c, v_c, o_c, g_c, qk_l, v_l, o_l, g_l, conv_w, gate_b, norm_g, need_ctx):
    def prep(qk, v, gates):
        qk = jax.nn.silu(conv3(qk, conv_w))
        q, k = jnp.split(qk, 2, axis=-1)
        B, T, _ = q.shape

        def hd(a):
            return a.reshape(B, T, ML_HEADS, ML_DH).transpose(0, 2, 1, 3).astype(jnp.float32)

        g = (gates.reshape(B, T, 4, ML_HEADS) + gate_b).astype(jnp.float32).transpose(2, 0, 3, 1)
        fwd = (g[0], jax.nn.log_sigmoid(g[1]))
        bwd = (g[2], jax.nn.log_sigmoid(g[3]))
        return hd(q), hd(k) * (ML_DH ** -0.5), hd(v), fwd, bwd

    qc, kc, vc, gfc, gbc = prep(qk_c, v_c, g_c)
    ql, kl, vl, gfl, gbl = prep(qk_l, v_l, g_l)
    B = ql.shape[0]
    zero = (jnp.zeros((B, ML_HEADS, ML_DH, ML_DH), jnp.float32),
            jnp.zeros((B, ML_HEADS, ML_DH), jnp.float32),
            jnp.zeros((B, ML_HEADS), jnp.float32))

    def flip(a):
        return jnp.flip(a, axis=2)

    hc_f, st_f = mlstm_scan(qc, kc, vc, gfc[0], gfc[1], zero)
    hl_f, _ = mlstm_scan(ql, kl, vl, gfl[0], gfl[1], st_f)
    hc_b, st_b = mlstm_scan(flip(qc), flip(kc), flip(vc), flip(gbc[0]), flip(gbc[1]), zero)
    hl_b, _ = mlstm_scan(flip(ql), flip(kl), flip(vl), flip(gbl[0]), flip(gbl[1]), st_b)

    def out(h, o):
        Bh, H, T, d = h.shape
        h = rmsnorm(h.transpose(0, 2, 1, 3), norm_g.reshape(H, d)).reshape(Bh, T, H * d)
        return (h * jax.nn.sigmoid(o.astype(jnp.float32))).astype(o.dtype)

    yl = out(hl_f + flip(hl_b), o_l)
    yc = out(hc_f + flip(hc_b), o_c) if need_ctx else None
    return yc, yl


def short_conv(bg, cg, xv, w):
    return bg * conv3(cg * xv, w)


def merge(y_da, y_ml, y_cv, gate, w_da, w_ml, w_cv, w_o):
    g = jax.nn.sigmoid(gate.astype(jnp.float32)).astype(y_da.dtype)
    g_da, g_ml, g_cv = jnp.split(g, N_BRANCH, axis=-1)
    m = g_da * (y_da @ w_da) + g_ml * (y_ml @ w_ml) + g_cv * (y_cv @ w_cv)
    return m @ w_o


def token_mixers(hc, hl, w_in, ml_gate_b, ml_conv_w, ml_norm_g, cv_conv_w, da_lam, da_subln_g,
                 w_da, w_ml, w_cv, w_o, lam_init, need_ctx):
    zc = split_cols(hc @ w_in)
    zl = split_cols(hl @ w_in)
    lv = da_lam.astype(jnp.float32)
    lam = jnp.exp(jnp.sum(lv[0] * lv[1])) - jnp.exp(jnp.sum(lv[2] * lv[3])) + lam_init
    da_c, da_l = diff_attention(zc[0], zc[1], zc[2], zl[0], zl[1], zl[2], lam, lam_init, da_subln_g, need_ctx)
    ml_c, ml_l = mlstm_mixer(zc[3], zc[4], zc[5], zc[6], zl[3], zl[4], zl[5], zl[6],
                             ml_conv_w, ml_gate_b, ml_norm_g, need_ctx)
    cv_l = short_conv(zl[7], zl[8], zl[9], cv_conv_w)
    out_l = merge(da_l, ml_l, cv_l, zl[10], w_da, w_ml, w_cv, w_o)
    out_c = None
    if need_ctx:
        cv_c = short_conv(zc[7], zc[8], zc[9], cv_conv_w)
        out_c = merge(da_c, ml_c, cv_c, zc[10], w_da, w_ml, w_cv, w_o)
    return out_c, out_l


def peer(h, wq, keys, u_tab, v_tab):
    B, T, D = h.shape
    N = B * T
    K = PEER_TOPK
    x = h.reshape(N, D)
    q = (x @ wq).reshape(N, PEER_HEADS, 2, PEER_DKEY // 2)
    s = jnp.einsum('nhpd,hpkd->nhpk', q, keys).astype(jnp.float32)
    top_s, top_i = lax.top_k(s, K)
    cand = top_s[:, :, 0, :, None] + top_s[:, :, 1, None, :]
    best_s, best = lax.top_k(cand.reshape(N, PEER_HEADS, K * K), K)
    i1 = jnp.take_along_axis(top_i[:, :, 0], best // K, axis=-1)
    i2 = jnp.take_along_axis(top_i[:, :, 1], best % K, axis=-1)
    expert = (i1 * PEER_NKEYS + i2).reshape(N, PEER_HEADS * K)
    gate = jax.nn.softmax(best_s, axis=-1).reshape(N, PEER_HEADS * K).astype(h.dtype)
    nb = N // PEER_BLOCK

    def block(args):
        xb, eb, gb = args
        a = jax.nn.gelu(jnp.einsum('tkd,td->tk', u_tab[eb], xb), approximate=False)
        return jnp.einsum('tk,tkd->td', gb * a, v_tab[eb])

    y = lax.map(block, (x.reshape(nb, PEER_BLOCK, D),
                        expert.reshape(nb, PEER_BLOCK, PEER_HEADS * K),
                        gate.reshape(nb, PEER_BLOCK, PEER_HEADS * K)))
    return y.reshape(B, T, D)


def setup_inputs(seed: int = 0) -> dict:
    key = jax.random.key(seed)
    ks = jax.random.split(key, 26)

    def nrm(k, shape, s):
        return jax.random.normal(k, shape, jnp.float32) * s

    f_bias = jnp.linspace(3.0, 6.0, ML_HEADS, dtype=jnp.float32)
    zb = jnp.zeros((ML_HEADS,), jnp.float32)
    gate_base = jnp.stack([zb, f_bias, zb, f_bias])
    return {
        "x": nrm(ks[0], (BATCH, SEQ, D_MODEL), 1.0),
        "c": nrm(ks[1], (BATCH, D_MODEL), 1.0),
        "ctx": nrm(ks[2], (BATCH, CTX_LEN, D_MODEL), 1.0),
        "c_ctx": nrm(ks[3], (D_MODEL,), 1.0),
        "w_ada": nrm(ks[4], (DEPTH, D_MODEL, 6 * D_MODEL), D_MODEL ** -0.5),
        "b_ada": nrm(ks[5], (DEPTH, 6 * D_MODEL), 0.02),
        "norm1_g": 1.0 + nrm(ks[6], (DEPTH, D_MODEL), 0.02),
        "norm2_g": 1.0 + nrm(ks[7], (DEPTH, D_MODEL), 0.02),
        "w_in": nrm(ks[8], (DEPTH, D_MODEL, D_IN), D_MODEL ** -0.5),
        "ml_gate_b": gate_base[None] + nrm(ks[9], (DEPTH, 4, ML_HEADS), 0.1),
        "ml_conv_w": nrm(ks[10], (DEPTH, SHORT_CONV, 2 * ML_WIDTH), SHORT_CONV ** -0.5),
        "ml_norm_g": 1.0 + nrm(ks[11], (DEPTH, ML_WIDTH), 0.02),
        "cv_conv_w": nrm(ks[12], (DEPTH, SHORT_CONV, CV_WIDTH), SHORT_CONV ** -0.5),
        "da_lam": nrm(ks[13], (DEPTH, 4, DA_QK), 0.1),
        "da_subln_g": 1.0 + nrm(ks[14], (DEPTH, DA_V), 0.02),
        "w_da": nrm(ks[15], (DEPTH, DA_WIDTH, D_MODEL), DA_WIDTH ** -0.5),
        "w_ml": nrm(ks[16], (DEPTH, ML_WIDTH, D_MODEL), ML_WIDTH ** -0.5),
        "w_cv": nrm(ks[17], (DEPTH, CV_WIDTH, D_MODEL), CV_WIDTH ** -0.5),
        "w_o": nrm(ks[18], (DEPTH, D_MODEL, D_MODEL), D_MODEL ** -0.5),
        "peer_wq": nrm(ks[19], (DEPTH, D_MODEL, PEER_HEADS * PEER_DKEY), D_MODEL ** -0.5),
        "peer_keys": nrm(ks[20], (DEPTH, PEER_HEADS, 2, PEER_NKEYS, PEER_DKEY // 2), (PEER_DKEY // 2) ** -0.5),
        "peer_u": nrm(ks[21], (DEPTH, PEER_N, D_MODEL), D_MODEL ** -0.5),
        "peer_v": nrm(ks[22], (DEPTH, PEER_N, D_MODEL), 0.5),
        "final_g": 1.0 + nrm(ks[23], (D_MODEL,), 0.02),
    }


def reference(x, c, ctx, c_ctx, w_ada, b_ada, norm1_g, norm2_g, w_in, ml_gate_b, ml_conv_w, ml_norm_g,
              cv_conv_w, da_lam, da_subln_g, w_da, w_ml, w_cv, w_o, peer_wq, peer_keys, peer_u, peer_v,
              final_g):
    xl, xc = x, ctx
    for l in range(DEPTH):
        need_ctx = l < DEPTH - 1
        lam_init = 0.8 - 0.6 * math.exp(-0.3 * l)
        sh1, sc1, g1, sh2, sc2, g2 = jnp.split(jax.nn.silu(c) @ w_ada[l] + b_ada[l], 6, axis=-1)
        csh1, csc1, cg1, csh2, csc2, cg2 = jnp.split(jax.nn.silu(c_ctx) @ w_ada[l] + b_ada[l], 6, axis=-1)
        hl = modulate(rmsnorm(xl, norm1_g[l]), sh1[:, None], sc1[:, None])
        hc = modulate(rmsnorm(xc, norm1_g[l]), csh1, csc1)
        out_c, out_l = token_mixers(hc, hl, w_in[l], ml_gate_b[l], ml_conv_w[l], ml_norm_g[l], cv_conv_w[l],
                                    da_lam[l], da_subln_g[l], w_da[l], w_ml[l], w_cv[l], w_o[l], lam_init, need_ctx)
        xl = xl + g1[:, None] * out_l
        hl = modulate(rmsnorm(xl, norm2_g[l]), sh2[:, None], sc2[:, None])
        xl = xl + g2[:, None] * peer(hl, peer_wq[l], peer_keys[l], peer_u[l], peer_v[l])
        if need_ctx:
            xc = xc + cg1 * out_c
            hc = modulate(rmsnorm(xc, norm2_g[l]), csh2, csc2)
            xc = xc + cg2 * peer(hc, peer_wq[l], peer_keys[l], peer_u[l], peer_v[l])
    return rmsnorm(xl, final_g)
```

```python
import functools
import math

import numpy as np
import jax
import jax.numpy as jnp
from jax import lax
from jax.experimental import pallas as pl
from jax.experimental.pallas import tpu as pltpu

EPS = 1e-6
GRID_W = 64
DA_HEADS = 4
DA_QK = 64
DA_V = 2 * DA_QK
DA_WIDTH = DA_HEADS * DA_V
ROPE_BASE = 10000.0
ML_HEADS = 4
ML_DH = 128
ML_WIDTH = ML_HEADS * ML_DH
ML_CHUNK = 128
CV_WIDTH = 512
PEER_HEADS = 8
PEER_TOPK = 16

LANES = 128
SUBLANES = 8
TOK_TILE = 256
HALO = 16
VMEM_LIMIT = 56 * 1024 * 1024

NEG_BIG = -1e30
BF16 = jnp.bfloat16
F32 = jnp.float32
HIGHEST = lax.Precision.HIGHEST


def _cparams(sem):
    return pltpu.CompilerParams(dimension_semantics=sem, vmem_limit_bytes=VMEM_LIMIT)


def _dot(a, b):
    return jnp.dot(a, b, preferred_element_type=F32)


def _dot_nt(a, b):
    return lax.dot_general(a, b, (((1,), (1,)), ((), ())), preferred_element_type=F32)


def _dot_tn(a, b):
    return lax.dot_general(a, b, (((0,), (0,)), ((), ())), preferred_element_type=F32)


def _rms(x, g):
    return x * lax.rsqrt(jnp.mean(x * x, axis=-1, keepdims=True) + EPS) * g


def _sigmoid(x):
    return 1.0 / (1.0 + jnp.exp(-x))


def _log_sigmoid(x):
    return jnp.minimum(x, 0.0) - jnp.log1p(jnp.exp(-jnp.abs(x)))


def _ada_kernel(c_ref, w_ref, b_ref, o_ref):
    c = c_ref[...]
    s = c * _sigmoid(c)
    o_ref[0] = jnp.dot(s, w_ref[0], preferred_element_type=F32, precision=HIGHEST) + b_ref[0]


def _ada(cc, w_ada, b_ada):
    depth, d, d6 = w_ada.shape
    rows = cc.shape[0]
    nb = d6 // d
    return pl.pallas_call(
        _ada_kernel,
        grid=(depth, nb),
        in_specs=[pl.BlockSpec((rows, d), lambda l, j: (0, 0)),
                  pl.BlockSpec((1, d, d), lambda l, j: (l, 0, j)),
                  pl.BlockSpec((1, 1, d), lambda l, j: (l, 0, j))],
        out_specs=pl.BlockSpec((1, rows, d), lambda l, j: (l, 0, j)),
        out_shape=jax.ShapeDtypeStruct((depth, rows, d6), F32),
        compiler_params=_cparams(("parallel", "parallel")),
        name="ada",
    )(cc, w_ada, b_ada.reshape(depth, 1, d6))


def _inproj_kernel(n_lat_tiles, n_batch, x_ref, ada_ref, g_ref, cos_ref, sin_ref,
                   wda_ref, wml_ref, wmg_ref, wcv_ref, wgt_ref,
                   da_ref, ml_ref, mg_ref, cv_ref, gt_ref):
    b = pl.program_id(0)
    i = pl.program_id(1)
    d = x_ref.shape[-1]
    row = jnp.where(i < n_lat_tiles, b, n_batch)
    shift = ada_ref[pl.ds(row, 1), 0:d]
    scale = ada_ref[pl.ds(row, 1), d:2 * d]
    h = _rms(x_ref[0], g_ref[...]) * (1.0 + scale) + shift
    hb = h.astype(BF16)

    zda = _dot(hb, wda_ref[...])
    cos = cos_ref[...]
    sin = sin_ref[...]
    for j in range(2 * DA_HEADS):
        blk = zda[:, j * LANES:(j + 1) * LANES]
        rot = blk * cos + pltpu.roll(blk, LANES // 2, 1) * sin
        da_ref[0, :, j * LANES:(j + 1) * LANES] = rot.astype(BF16)
    da_ref[0, :, 2 * DA_WIDTH:] = zda[:, 2 * DA_WIDTH:].astype(BF16)

    ml_ref[0] = _dot(hb, wml_ref[...]).astype(BF16)
    mg_ref[0] = _dot(hb, wmg_ref[...])

    zcv = _dot(hb, wcv_ref[...])
    cv_ref[0, :, 0:CV_WIDTH] = zcv[:, 0:CV_WIDTH].astype(BF16)
    cv_ref[0, :, CV_WIDTH:] = (zcv[:, CV_WIDTH:2 * CV_WIDTH] * zcv[:, 2 * CV_WIDTH:]).astype(BF16)

    for j in range(3):
        zg = _dot(hb, wgt_ref[:, j * d:(j + 1) * d])
        gt_ref[0, :, j * d:(j + 1) * d] = _sigmoid(zg).astype(BF16)


def _inproj(x, ada, g, cos, sin, wda, wml, wmg, wcv, wgt, n_lat_tiles):
    bsz, t, d = x.shape
    tm = TOK_TILE
    full = lambda arr: pl.BlockSpec(arr.shape, lambda b, i: (0,) * arr.ndim, pipeline_mode=pl.Buffered(1))
    tok = lambda w: pl.BlockSpec((1, tm, w), lambda b, i: (b, i, 0))
    widths =(wda.shape[1], wml.shape[1], wmg.shape[1], 2 * CV_WIDTH, wgt.shape[1])
    dts = (BF16, BF16, F32, BF16, BF16)
    return pl.pallas_call(
        functools.partial(_inproj_kernel, n_lat_tiles, bsz),
        grid=(bsz, t // tm),
        in_specs=[tok(d), full(ada), full(g),
                  pl.BlockSpec((tm, LANES), lambda b, i: (i, 0)),
                  pl.BlockSpec((tm, LANES), lambda b, i: (i, 0)),
                  full(wda), full(wml), full(wmg), full(wcv), full(wgt)],
        out_specs=[tok(w) for w in widths],
        out_shape=[jax.ShapeDtypeStruct((bsz, t, w), dt) for w, dt in zip(widths, dts)],
        compiler_params=_cparams(("parallel", "parallel")),
        name="inproj",
    )(x, ada, g, cos, sin, wda, wml, wmg, wcv, wgt)


def _attn_kernel(n_lat_tiles, seq, lam_init, q_ref, k_ref, v_ref, lam_ref, g_ref, o_ref):
    i = pl.program_id(2)
    lv = lam_ref[...]
    lam = (jnp.exp(jnp.sum(lv[0:1] * lv[1:2], axis=-1, keepdims=True))
           - jnp.exp(jnp.sum(lv[2:3] * lv[3:4], axis=-1, keepdims=True)) + lam_init)
    lane = lax.broadcasted_iota(jnp.int32, (1, LANES), 1)
    map0 = (lane % (LANES // 2)) < (DA_QK // 2)

    def attend(k, v):
        q = q_ref[0].astype(F32) * (DA_QK ** -0.5)
        q0 = jnp.where(map0, q, 0.0).astype(BF16)
        q1 = jnp.where(map0, 0.0, q).astype(BF16)
        s0 = _dot_nt(q0, k)
        s1 = _dot_nt(q1, k)
        e0 = jnp.exp(s0 - jnp.max(s0, axis=-1, keepdims=True))
        e1 = jnp.exp(s1 - jnp.max(s1, axis=-1, keepdims=True))
        r0 = 1.0 / jnp.sum(e0, axis=-1, keepdims=True)
        r1 = lam / jnp.sum(e1, axis=-1, keepdims=True)
        p = (e0 * r0 - e1 * r1).astype(BF16)
        o = _dot(p, v)
        o_ref[0] = (_rms(o, g_ref[...]) * (1.0 - lam_init)).astype(o_ref.dtype)

    @pl.when(i < n_lat_tiles)
    def _():
        attend(k_ref[0], v_ref[0])

    @pl.when(i >= n_lat_tiles)
    def _():
        attend(k_ref[0, seq:, :], v_ref[0, seq:, :])


def _attention(da, lam_p, subln_g, n_lat_tiles, seq, lam_init):
    bsz, t, _ = da.shape
    tq = TOK_TILE
    return pl.pallas_call(
        functools.partial(_attn_kernel, n_lat_tiles, seq, lam_init),
        grid=(bsz, DA_HEADS, t // tq),
        in_specs=[pl.BlockSpec((1, tq, LANES), lambda b, h, i: (b, i, h)),
                  pl.BlockSpec((1, t, LANES), lambda b, h, i: (b, 0, DA_HEADS + h)),
                  pl.BlockSpec((1, t, LANES), lambda b, h, i: (b, 0, 2 * DA_HEADS + h)),
                  pl.BlockSpec(lam_p.shape, lambda b, h, i: (0, 0)),
                  pl.BlockSpec((1, LANES), lambda b, h, i: (0, 0))],
        out_specs=pl.BlockSpec((1, tq, LANES), lambda b, h, i: (b, i, h)),
        out_shape=jax.ShapeDtypeStruct((bsz, t, DA_WIDTH), BF16),
        compiler_params=_cparams(("parallel", "parallel", "parallel")),
        name="diffattn",
    )(da, da, da, lam_p, subln_g.reshape(1, LANES))


def _mlstm_kernel(ncl, ncc, ml_ref, mg_ref, cw_ref, gb_ref, ng_ref, o_ref,
                  qs_ref, ks_ref, hf_ref, hb_ref, c_ref, n_ref, m_ref):
    L = ML_CHUNK
    nc = ncl + ncc
    t_all = nc * L
    seq = ncl * L
    w = ML_WIDTH

    trow = lax.broadcasted_iota(jnp.int32, (t_all, 1), 0)
    first = (trow == 0) | (trow == seq)
    last = (trow == seq - 1) | (trow == t_all - 1)
    for j in range(2 * ML_HEADS):
        x = ml_ref[0, :, j * LANES:(j + 1) * LANES].astype(F32)
        xm = jnp.where(first, 0.0, pltpu.roll(x, 1, 0))
        xp = jnp.where(last, 0.0, pltpu.roll(x, t_all - 1, 0))
        cw = cw_ref[:, j * LANES:(j + 1) * LANES]
        y = xm * cw[0:1] + x * cw[1:2] + xp * cw[2:3]
        y = y * _sigmoid(y)
        if j < ML_HEADS:
            qs_ref[:, j * LANES:(j + 1) * LANES] = y.astype(BF16)
        else:
            jj = j - ML_HEADS
            ks_ref[:, jj * LANES:(jj + 1) * LANES] = (y * (ML_DH ** -0.5)).astype(BF16)

    c_ref[...] = jnp.zeros_like(c_ref)
    n_ref[...] = jnp.zeros_like(n_ref)
    m_ref[...] = jnp.zeros_like(m_ref)

    r_i = lax.broadcasted_iota(jnp.int32, (L, L), 0)
    c_i = lax.broadcasted_iota(jnp.int32, (L, L), 1)
    lower = r_i >= c_i
    upper = r_i <= c_i
    lower_f = lower.astype(F32)
    upper_f = upper.astype(F32)

    def step(it, carry):
        for dr in range(2):
            if dr == 0:
                ch = (it + ncl) % nc
                seen, tri_col, tri_row, end = lower, lower_f, upper_f, L - 1
                h_ref = hf_ref
            else:
                ch = nc - 1 - it
                seen, tri_col, tri_row, end = upper, upper_f, lower_f, 0
                h_ref = hb_ref
            r0 = pl.multiple_of(ch * L, L)
            g = mg_ref[0, pl.ds(r0, L), :] + gb_ref[...]
            gt = g.T
            bcol_all = jnp.dot(tri_col, _log_sigmoid(g), preferred_element_type=F32, precision=HIGHEST)
            brow_all = jnp.dot(_log_sigmoid(gt), tri_row, preferred_element_type=F32, precision=HIGHEST)
            for h in range(ML_HEADS):
                idx = dr * ML_HEADS + h
                icol = (2 * dr) * ML_HEADS + h
                fcol = (2 * dr + 1) * ML_HEADS + h
                q = qs_ref[pl.ds(r0, L), h * LANES:(h + 1) * LANES]
                k = ks_ref[pl.ds(r0, L), h * LANES:(h + 1) * LANES]
                v = ml_ref[0, pl.ds(r0, L), 2 * w + h * LANES:2 * w + (h + 1) * LANES]
                b_col = bcol_all[:, fcol:fcol + 1]
                ig_col = g[:, icol:icol + 1]
                b_row = brow_all[fcol:fcol + 1, :]
                ig_row = gt[icol:icol + 1, :]
                m_prev = m_ref[idx][:, 0:1]
                cmat = c_ref[idx]
                nrow = n_ref[idx]

                log_d = jnp.where(seen, b_col - b_row + ig_row, NEG_BIG)
                m_path = b_col + m_prev
                m_t = jnp.maximum(m_path, jnp.max(log_d, axis=-1, keepdims=True))
                dmat = jnp.exp(log_d - m_t)
                carry_w = jnp.exp(m_path - m_t)
                wmat = _dot_nt(q, k) * dmat
                num = _dot(wmat.astype(BF16), v) + carry_w * _dot(q, cmat.astype(BF16))
                den = (jnp.sum(wmat, axis=-1, keepdims=True)
                       + carry_w * jnp.sum(q.astype(F32) * nrow, axis=-1, keepdims=True))
                hval = num / jnp.maximum(jnp.abs(den), jnp.exp(-m_t))
                h_ref[pl.ds(r0, L), h * LANES:(h + 1) * LANES] = hval

                b_end = b_col[end:end + 1, :]
                log_w = b_end - b_col + ig_col
                m_new = jnp.maximum(b_end + m_prev, jnp.max(log_w, axis=0, keepdims=True))
                wk = jnp.exp(log_w - m_new)
                decay = jnp.exp(b_end + m_prev - m_new)
                kw = k.astype(F32) * wk
                c_ref[idx] = decay * cmat + _dot_tn(kw.astype(BF16), v)
                n_ref[idx] = decay * nrow + jnp.sum(kw, axis=0, keepdims=True)
                m_ref[idx] = jnp.broadcast_to(m_new, (1, LANES))
        return carry

    lax.fori_loop(0, nc, step, 0)

    for h in range(ML_HEADS):
        sl = slice(h * LANES, (h + 1) * LANES)
        hs = hf_ref[:, sl] + hb_ref[:, sl]
        y = _rms(hs, ng_ref[:, sl])
        og = ml_ref[0, :, 3 * w + h * LANES:3 * w + (h + 1) * LANES].astype(F32)
        o_ref[0, :, sl] = (y * _sigmoid(og)).astype(o_ref.dtype)


def _mlstm(ml, mg, conv_w, gate_b_row, norm_g, ncl, ncc):
    bsz, t, wtot = ml.shape
    nstate = 2 * ML_HEADS
    return pl.pallas_call(
        functools.partial(_mlstm_kernel, ncl, ncc),
        grid=(bsz,),
        in_specs=[pl.BlockSpec((1, t, wtot), lambda b: (b, 0, 0)),
                  pl.BlockSpec((1, t, LANES), lambda b: (b, 0, 0)),
                  pl.BlockSpec(conv_w.shape, lambda b: (0, 0)),
                  pl.BlockSpec((1, LANES), lambda b: (0, 0)),
                  pl.BlockSpec((1, ML_WIDTH), lambda b: (0, 0))],
        out_specs=pl.BlockSpec((1, t, ML_WIDTH), lambda b: (b, 0, 0)),
        out_shape=jax.ShapeDtypeStruct((bsz, t, ML_WIDTH), BF16),
        scratch_shapes=[pltpu.VMEM((t, ML_WIDTH), BF16), pltpu.VMEM((t, ML_WIDTH), BF16),
                        pltpu.VMEM((t, ML_WIDTH), F32), pltpu.VMEM((t, ML_WIDTH), F32),
                        pltpu.VMEM((nstate, ML_DH, ML_DH), F32),
                        pltpu.VMEM((nstate, 1, ML_DH), F32),
                        pltpu.VMEM((nstate, 1, LANES), F32)],
        compiler_params=_cparams(("parallel",)),
        name="mlstm",
    )(ml, mg, conv_w, gate_b_row, norm_g.reshape(1, ML_WIDTH))


def _merge_kernel(n_lat_tiles, n_batch, x_ref, ada_ref, g2_ref, yda_ref, yml_ref, cv_ref, pprev_ref, pnext_ref,
                  gt_ref, cw_ref, wda_ref, wml_ref, wcv_ref, wo_ref, x1_ref, h2_ref):
    b = pl.program_id(0)
    i = pl.program_id(1)
    n_tiles = pl.num_programs(1)
    d = x_ref.shape[-1]
    tm = x_ref.shape[1]
    row = jnp.where(i < n_lat_tiles, b, n_batch)

    bg = cv_ref[0, :, 0:CV_WIDTH].astype(F32)
    p = cv_ref[0, :, CV_WIDTH:].astype(F32)
    seq_start = (i == 0) | (i == n_lat_tiles)
    seq_end = (i == n_lat_tiles - 1) | (i == n_tiles - 1)
    prev_row = jnp.where(seq_start, 0.0, pprev_ref[0, HALO - 1:HALO, :].astype(F32))
    next_row = jnp.where(seq_end, 0.0, pnext_ref[0, 0:1, :].astype(F32))
    trow = lax.broadcasted_iota(jnp.int32, (tm, 1), 0)
    pm = jnp.where(trow == 0, prev_row, pltpu.roll(p, 1, 0))
    pp = jnp.where(trow == tm - 1, next_row, pltpu.roll(p, tm - 1, 0))
    cw = cw_ref[...]
    ycv = bg * (pm * cw[0:1] + p * cw[1:2] + pp * cw[2:3])

    m = (gt_ref[0, :, 0:d].astype(F32) * _dot(yda_ref[0], wda_ref[...])
         + gt_ref[0, :, d:2 * d].astype(F32) * _dot(yml_ref[0], wml_ref[...])
         + gt_ref[0, :, 2 * d:].astype(F32) * _dot(ycv.astype(BF16), wcv_ref[...]))
    out = _dot(m.astype(BF16), wo_ref[...])
    g1 = ada_ref[pl.ds(row, 1), 2 * d:3 * d]
    x1 = x_ref[0] + g1 * out
    x1_ref[0] = x1
    sh2 = ada_ref[pl.ds(row, 1), 3 * d:4 * d]
    sc2 = ada_ref[pl.ds(row, 1), 4 * d:5 * d]
    h2_ref[0] = (_rms(x1, g2_ref[...]) * (1.0 + sc2) + sh2).astype(BF16)


def _merge(x, ada, g2, yda, yml, cv, gt, cv_w, wda, wml, wcv, wo, n_lat_tiles, n_tiles):
    bsz, t, d = x.shape
    tm = TOK_TILE
    r = tm // HALO
    nrb = t // HALO
    full = lambda arr: pl.BlockSpec(arr.shape, lambda b, i: (0,) * arr.ndim, pipeline_mode=pl.Buffered(1))
    tok = lambda w: pl.BlockSpec((1, tm, w), lambda b, i: (b, i, 0))
    return pl.pallas_call(
        functools.partial(_merge_kernel, n_lat_tiles, bsz),
        grid=(bsz, n_tiles),
        in_specs=[tok(d), full(ada), full(g2), tok(DA_WIDTH), tok(ML_WIDTH), tok(2 * CV_WIDTH),
                  pl.BlockSpec((1, HALO, CV_WIDTH), lambda b, i: (b, jnp.maximum(i * r - 1, 0), 1)),
                  pl.BlockSpec((1, HALO, CV_WIDTH), lambda b, i: (b, jnp.minimum((i + 1) * r, nrb - 1), 1)),
                  tok(3 * d), full(cv_w), full(wda), full(wml), full(wcv), full(wo)],
        out_specs=[tok(d), tok(d)],
        out_shape=[jax.ShapeDtypeStruct((bsz, t, d), F32), jax.ShapeDtypeStruct((bsz, t, d), BF16)],
        compiler_params=_cparams(("parallel", "parallel")),
        name="merge",
    )(x, ada, g2, yda, yml, cv, cv, cv, gt, cv_w, wda, wml, wcv, wo)


def _cmpx(v, i, j):
    a, b = v[i], v[j]
    v[i] = jnp.maximum(a, b)
    v[j] = jnp.minimum(a, b)


def _bitonic_merge_desc(v):
    n = len(v)
    j = n // 2
    while j >= 1:
        for i in range(n):
            l = i ^ j
            if l > i:
                _cmpx(v, i, l)
        j //= 2


def _bitonic_sort_desc(v):
    n = len(v)
    k = 2
    while k <= n:
        j = k // 2
        while j >= 1:
            for i in range(n):
                l = i ^ j
                if l > i:
                    if (i & k) == 0:
                        _cmpx(v, i, l)
                    else:
                        _cmpx(v, l, i)
            j //= 2
        k *= 2


def _merge_top(a, b):
    n = len(a)
    c = [jnp.maximum(a[r], b[n - 1 - r]) for r in range(n)]
    _bitonic_merge_desc(c)
    return c


def _top16_sorted(s):
    nk, tm = s.shape
    k = PEER_TOPK
    groups = nk // k
    assert groups == SUBLANES
    a = s.reshape(k, SUBLANES, tm)
    v = [a[r] for r in range(k)]
    _bitonic_sort_desc(v)
    sh = SUBLANES // 2
    while sh >= 1:
        v = _merge_top(v, [pltpu.roll(x, sh, 0) for x in v])
        sh //= 2
    return v


def _erf(x):
    return lax.erf(x)


def _peer_kernel(n_lat_tiles, n_batch, n_i1, final, x_ref, h_ref, ada_ref, fg_ref, wq_ref, keys_ref, u_ref, vt_ref,
                 o_ref, s1_ref, s2_ref, f1_ref, e2_ref, tau_ref, w_ref, acc_ref):
    b = pl.program_id(0)
    i = pl.program_id(1)
    c = pl.program_id(2)
    n_chunks = pl.num_programs(2)
    d = x_ref.shape[-1]
    tm = x_ref.shape[1]
    nk = keys_ref.shape[2]
    dsub = keys_ref.shape[3]
    k = PEER_TOPK

    @pl.when(c == 0)
    def _():
        hb = h_ref[0]
        q = _dot(hb, wq_ref[...]).astype(BF16)
        tops = [[], []]
        for h in range(PEER_HEADS):
            for p in range(2):
                col = (h * 2 + p) * dsub
                s = _dot_nt(keys_ref[h, p], q[:, col:col + dsub])
                (s1_ref if p == 0 else s2_ref)[h] = s
                tops[p].append(_top16_sorted(s))
        sub = lax.broadcasted_iota(jnp.int32, (SUBLANES, tm), 0)

        def pack(lst, r):
            out = lst[0][r]
            for h in range(1, PEER_HEADS):
                out = jnp.where(sub == h, lst[h][r], out)
            return out

        t1 = [pack(tops[0], r) for r in range(k)]
        t2 = [pack(tops[1], r) for r in range(k)]
        rows = [[t1[r1] + t2[r2] for r2 in range(k // (r1 + 1))] for r1 in range(k)]
        neg = jnp.full((SUBLANES, tm), -jnp.inf, F32)
        best = rows[0]
        rest = [x for rw in rows[1:] for x in rw]
        while rest:
            grp, rest = rest[:k], rest[k:]
            grp = grp + [neg] * (k - len(grp))
            _bitonic_sort_desc(grp)
            best = _merge_top(best, grp)
        tau = best[k - 1]
        mx = best[0]
        z = jnp.exp(best[0] - mx)
        for r in range(1, k):
            z = z + jnp.exp(best[r] - mx)
        rz = 1.0 / z
        tau_ref[...] = tau
        for h in range(PEER_HEADS):
            f1_ref[h] = jnp.exp(s1_ref[h] - t1[0][h:h + 1]) * rz[h:h + 1]
            e2_ref[h] = jnp.exp(s2_ref[h] - t2[0][h:h + 1])
        acc_ref[...] = jnp.zeros_like(acc_ref)

    at = _dot_nt(u_ref[...], h_ref[0])
    for j in range(n_i1):
        i1 = c * n_i1 + j
        g = jnp.zeros((nk, tm), F32)
        for h in range(PEER_HEADS):
            s1r = s1_ref[h, pl.ds(i1, 1), :]
            f1r = f1_ref[h, pl.ds(i1, 1), :]
            sel = (s2_ref[h] + s1r) >= tau_ref[h:h + 1, :]
            g = g + jnp.where(sel, e2_ref[h] * f1r, 0.0)
        a = at[j * nk:(j + 1) * nk, :]
        act = 0.5 * a * (1.0 + _erf(a * (2.0 ** -0.5)))
        w_ref[j * nk:(j + 1) * nk, :] = (act * g).astype(BF16)
    acc_ref[...] += _dot(vt_ref[...], w_ref[...])

    @pl.when(c == n_chunks - 1)
    def _():
        row = jnp.where(i < n_lat_tiles, b, n_batch)
        g2 = ada_ref[pl.ds(row, 1), 5 * d:6 * d]
        x2 = x_ref[0] + g2 * acc_ref[...].T
        if final:
            x2 = _rms(x2, fg_ref[...])
        o_ref[0] = x2


def _peer(x1, h2, ada, final_g, wq, keys, u, vt, n_lat_tiles, n_tiles, final):
    bsz, t, d = x1.shape
    tm = TOK_TILE
    nk = keys.shape[2]
    n_i1 = 8
    ec = n_i1 * nk
    n_chunks = u.shape[0] // ec
    full = lambda arr: pl.BlockSpec(arr.shape, lambda b, i, c: (0,) * arr.ndim, pipeline_mode=pl.Buffered(1))
    tok = lambda w: pl.BlockSpec((1, tm, w), lambda b, i, c: (b, i, 0))
    t_out = n_tiles * tm
    return pl.pallas_call(
        functools.partial(_peer_kernel, n_lat_tiles, bsz, n_i1, final),
        grid=(bsz, n_tiles, n_chunks),
        in_specs=[tok(d), tok(d), full(ada), full(final_g), full(wq), full(keys),
                  pl.BlockSpec((ec, d), lambda b, i, c: (c, 0)),
                  pl.BlockSpec((d, ec), lambda b, i, c: (0, c))],
        out_specs=tok(d),
        out_shape=jax.ShapeDtypeStruct((bsz, t_out, d), F32),
        scratch_shapes=[pltpu.VMEM((PEER_HEADS, nk, tm), F32), pltpu.VMEM((PEER_HEADS, nk, tm), F32),
                        pltpu.VMEM((PEER_HEADS, nk, tm), F32), pltpu.VMEM((PEER_HEADS, nk, tm), F32),
                        pltpu.VMEM((SUBLANES, tm), F32),
                        pltpu.VMEM((ec, tm), BF16), pltpu.VMEM((d, tm), F32)],
        compiler_params=_cparams(("parallel", "parallel", "arbitrary")),
        name="peer",
    )(x1, h2, ada, final_g, wq, keys, u, vt)


def _rope_tables(seq, t):
    quarter = DA_QK // 4
    inv = ROPE_BASE ** (-np.arange(quarter, dtype=np.float32) / quarter)
    pos = np.arange(seq)
    ang_r = (pos // GRID_W).astype(np.float32)[:, None] * inv[None, :]
    ang_c = (pos % GRID_W).astype(np.float32)[:, None] * inv[None, :]
    half = np.concatenate([ang_r, ang_c, ang_r, ang_c], axis=1)
    cos = np.ones((t, LANES), np.float32)
    sin = np.zeros((t, LANES), np.float32)
    cos[:seq] = np.concatenate([np.cos(half), np.cos(half)], axis=1)
    sin[:seq] = np.concatenate([-np.sin(half), np.sin(half)], axis=1)
    return jnp.asarray(cos), jnp.asarray(sin)


def _qk_perm():
    quarter = DA_QK // 4
    first, second = [], []
    for m in range(2):
        for part in range(2):
            base = m * DA_QK + part * (DA_QK // 2)
            first += [base + j for j in range(quarter)]
            second += [base + quarter + j for j in range(quarter)]
    return np.array(first + second)


def kernel(x, c, ctx, c_ctx, w_ada, b_ada, norm1_g, norm2_g, w_in, ml_gate_b, ml_conv_w, ml_norm_g, cv_conv_w,
           da_lam, da_subln_g, w_da, w_ml, w_cv, w_o, peer_wq, peer_keys, peer_u, peer_v, final_g):
    bsz, seq, d = x.shape
    n_ctx = ctx.shape[1]
    t = seq + n_ctx
    depth = w_ada.shape[0]
    tm = TOK_TILE
    assert seq % tm == 0 and n_ctx % tm == 0 and seq % ML_CHUNK == 0 and n_ctx % ML_CHUNK == 0
    n_lat_tiles = seq // tm
    n_tiles = t // tm

    rows = -(-(bsz + 1) // SUBLANES) * SUBLANES
    cc = jnp.zeros((rows, d), F32).at[:bsz].set(c).at[bsz].set(c_ctx)
    ada = _ada(cc, w_ada, b_ada)

    o = np.cumsum([0, DA_WIDTH, DA_WIDTH, DA_WIDTH, 2 * ML_WIDTH, ML_WIDTH, ML_WIDTH, 4 * ML_HEADS,
                   CV_WIDTH, CV_WIDTH, CV_WIDTH, 3 * d])
    perm = _qk_perm()
    qk_cols = np.concatenate([o[g] + h * DA_V + perm for g in range(2) for h in range(DA_HEADS)])
    da_cols = np.concatenate([qk_cols, np.arange(o[2], o[3])])
    wda_in = w_in[:, :, da_cols].astype(BF16)
    wml_in = w_in[:, :, o[3]:o[6]].astype(BF16)
    wmg_in = jnp.pad(w_in[:, :, o[6]:o[7]], ((0, 0), (0, 0), (0, LANES - 4 * ML_HEADS))).astype(BF16)
    wcv_in = w_in[:, :, o[7]:o[10]].astype(BF16)
    wgt_in = w_in[:, :, o[10]:o[11]].astype(BF16)
    gate_b = jnp.pad(ml_gate_b.reshape(depth, 1, 4 * ML_HEADS), ((0, 0), (0, 0), (0, LANES - 4 * ML_HEADS)))
    lam_p = da_lam
    w_da_b, w_ml_b, w_cv_b, w_o_b = (a.astype(BF16) for a in (w_da, w_ml, w_cv, w_o))
    wq_b = peer_wq.astype(BF16)
    keys_b = peer_keys.astype(BF16)
    u_b = peer_u.astype(BF16)
    vt_b = jnp.swapaxes(peer_v, 1, 2).astype(BF16)
    cos, sin = _rope_tables(seq, t)

    xs = jnp.concatenate([x, ctx], axis=1)
    for l in range(depth):
        need_ctx = l < depth - 1
        lam_init = 0.8 - 0.6 * math.exp(-0.3 * l)
        da, ml, mg, cv, gt = _inproj(xs, ada[l], norm1_g[l].reshape(1, d), cos, sin,
                                     wda_in[l], wml_in[l], wmg_in[l], wcv_in[l], wgt_in[l], n_lat_tiles)
        yda = _attention(da, lam_p[l], da_subln_g[l], n_lat_tiles, seq, lam_init)
        yml = _mlstm(ml, mg, ml_conv_w[l], gate_b[l], ml_norm_g[l], seq // ML_CHUNK, n_ctx // ML_CHUNK)
        n_out_tiles = n_tiles if need_ctx else n_lat_tiles
        x1, h2 = _merge(xs, ada[l], norm2_g[l].reshape(1, d), yda, yml, cv, gt, cv_conv_w[l],
                        w_da_b[l], w_ml_b[l], w_cv_b[l], w_o_b[l], n_lat_tiles, n_out_tiles)
        xs = _peer(x1, h2, ada[l], final_g.reshape(1, d), wq_b[l], keys_b[l], u_b[l], vt_b[l],
                   n_lat_tiles, n_out_tiles, final=not need_ctx)
    return xs
```

```python
import functools
import math

import numpy as np
import jax
import jax.numpy as jnp
from jax import lax
from jax.experimental import pallas as pl
from jax.experimental.pallas import tpu as pltpu

EPS = 1e-6
GRID_W = 64
DA_HEADS = 4
DA_QK = 64
DA_V = 2 * DA_QK
DA_WIDTH = DA_HEADS * DA_V
ROPE_BASE = 10000.0
ML_HEADS = 4
ML_DH = 128
ML_WIDTH = ML_HEADS * ML_DH
ML_CHUNK = 128
CV_WIDTH = 512
PEER_HEADS = 8
PEER_TOPK = 16
PEER_GROUP = 2

LANES = 128
SUBLANES = 8
TOK_TILE = 256
HALO = 16
PEER_TILE = 512
PEER_CHUNK_I1 = 8
PEER_SUBSTEPS = 2
PEER_SORT_W = 2 * LANES
VMEM_LIMIT = 56 * 1024 * 1024

NEG_BIG = -1e30
BF16 = jnp.bfloat16
F32 = jnp.float32
HIGHEST = lax.Precision.HIGHEST


def _cparams(sem):
    return pltpu.CompilerParams(dimension_semantics=sem, vmem_limit_bytes=VMEM_LIMIT)


def _dot(a, b):
    return jnp.dot(a, b, preferred_element_type=F32)


def _dot_nt(a, b):
    return lax.dot_general(a, b, (((1,), (1,)), ((), ())), preferred_element_type=F32)


def _dot_tn(a, b):
    return lax.dot_general(a, b, (((0,), (0,)), ((), ())), preferred_element_type=F32)


def _rms(x, g):
    return x * lax.rsqrt(jnp.mean(x * x, axis=-1, keepdims=True) + EPS) * g


def _sigmoid(x):
    return 1.0 / (1.0 + jnp.exp(-x))


def _log_sigmoid(x):
    return jnp.minimum(x, 0.0) - jnp.log1p(jnp.exp(-jnp.abs(x)))


def _ada_kernel(c_ref, w_ref, b_ref, o_ref):
    c = c_ref[...]
    s = c * _sigmoid(c)
    o_ref[0] = jnp.dot(s, w_ref[0], preferred_element_type=F32, precision=HIGHEST) + b_ref[0]


def _ada(cc, w_ada, b_ada):
    depth, d, d6 = w_ada.shape
    rows = cc.shape[0]
    nb = d6 // d
    return pl.pallas_call(
        _ada_kernel,
        grid=(depth, nb),
        in_specs=[pl.BlockSpec((rows, d), lambda l, j: (0, 0)),
                  pl.BlockSpec((1, d, d), lambda l, j: (l, 0, j)),
                  pl.BlockSpec((1, 1, d), lambda l, j: (l, 0, j))],
        out_specs=pl.BlockSpec((1, rows, d), lambda l, j: (l, 0, j)),
        out_shape=jax.ShapeDtypeStruct((depth, rows, d6), F32),
        compiler_params=_cparams(("parallel", "parallel")),
        name="ada",
    )(cc, w_ada, b_ada.reshape(depth, 1, d6))


def _inproj_kernel(n_lat_tiles, n_batch, x_ref, ada_ref, g_ref, cos_ref, sin_ref,
                   wda_ref, wml_ref, wmg_ref, wcv_ref, wgt_ref,
                   da_ref, ml_ref, mg_ref, cv_ref, gt_ref):
    b = pl.program_id(0)
    i = pl.program_id(1)
    d = x_ref.shape[-1]
    row = jnp.where(i < n_lat_tiles, b, n_batch)
    shift = ada_ref[pl.ds(row, 1), 0:d]
    scale = ada_ref[pl.ds(row, 1), d:2 * d]
    h = _rms(x_ref[0], g_ref[...]) * (1.0 + scale) + shift
    hb = h.astype(BF16)

    zda = _dot(hb, wda_ref[...])
    cos = cos_ref[...]
    sin = sin_ref[...]
    for j in range(2 * DA_HEADS):
        blk = zda[:, j * LANES:(j + 1) * LANES]
        rot = blk * cos + pltpu.roll(blk, LANES // 2, 1) * sin
        da_ref[0, :, j * LANES:(j + 1) * LANES] = rot.astype(BF16)
    da_ref[0, :, 2 * DA_WIDTH:] = zda[:, 2 * DA_WIDTH:].astype(BF16)

    ml_ref[0] = _dot(hb, wml_ref[...]).astype(BF16)
    mg_ref[0] = _dot(hb, wmg_ref[...])

    zcv = _dot(hb, wcv_ref[...])
    cv_ref[0, :, 0:CV_WIDTH] = zcv[:, 0:CV_WIDTH].astype(BF16)
    cv_ref[0, :, CV_WIDTH:] = (zcv[:, CV_WIDTH:2 * CV_WIDTH] * zcv[:, 2 * CV_WIDTH:]).astype(BF16)

    for j in range(3):
        zg = _dot(hb, wgt_ref[:, j * d:(j + 1) * d])
        gt_ref[0, :, j * d:(j + 1) * d] = _sigmoid(zg).astype(BF16)


def _inproj(x, ada, g, cos, sin, wda, wml, wmg, wcv, wgt, n_lat_tiles):
    bsz, t, d = x.shape
    tm = TOK_TILE
    full = lambda arr: pl.BlockSpec(arr.shape, lambda b, i: (0,) * arr.ndim, pipeline_mode=pl.Buffered(1))
    tok = lambda w: pl.BlockSpec((1, tm, w), lambda b, i: (b, i, 0))
    widths =(wda.shape[1], wml.shape[1], wmg.shape[1], 2 * CV_WIDTH, wgt.shape[1])
    dts = (BF16, BF16, F32, BF16, BF16)
    return pl.pallas_call(
        functools.partial(_inproj_kernel, n_lat_tiles, bsz),
        grid=(bsz, t // tm),
        in_specs=[tok(d), full(ada), full(g),
                  pl.BlockSpec((tm, LANES), lambda b, i: (i, 0)),
                  pl.BlockSpec((tm, LANES), lambda b, i: (i, 0)),
                  full(wda), full(wml), full(wmg), full(wcv), full(wgt)],
        out_specs=[tok(w) for w in widths],
        out_shape=[jax.ShapeDtypeStruct((bsz, t, w), dt) for w, dt in zip(widths, dts)],
        compiler_params=_cparams(("parallel", "parallel")),
        name="inproj",
    )(x, ada, g, cos, sin, wda, wml, wmg, wcv, wgt)


def _attn_kernel(n_lat_tiles, seq, lam_init, q_ref, k_ref, v_ref, lam_ref, g_ref, o_ref):
    i = pl.program_id(2)
    lv = lam_ref[...]
    lam = (jnp.exp(jnp.sum(lv[0:1] * lv[1:2], axis=-1, keepdims=True))
           - jnp.exp(jnp.sum(lv[2:3] * lv[3:4], axis=-1, keepdims=True)) + lam_init)
    lane = lax.broadcasted_iota(jnp.int32, (1, LANES), 1)
    map0 = (lane % (LANES // 2)) < (DA_QK // 2)

    def attend(k, v):
        q = q_ref[0].astype(F32) * (DA_QK ** -0.5)
        q0 = jnp.where(map0, q, 0.0).astype(BF16)
        q1 = jnp.where(map0, 0.0, q).astype(BF16)
        s0 = _dot_nt(q0, k)
        s1 = _dot_nt(q1, k)
        e0 = jnp.exp(s0 - jnp.max(s0, axis=-1, keepdims=True))
        e1 = jnp.exp(s1 - jnp.max(s1, axis=-1, keepdims=True))
        r0 = 1.0 / jnp.sum(e0, axis=-1, keepdims=True)
        r1 = lam / jnp.sum(e1, axis=-1, keepdims=True)
        p = (e0 * r0 - e1 * r1).astype(BF16)
        o = _dot(p, v)
        o_ref[0] = (_rms(o, g_ref[...]) * (1.0 - lam_init)).astype(o_ref.dtype)

    @pl.when(i < n_lat_tiles)
    def _():
        attend(k_ref[0], v_ref[0])

    @pl.when(i >= n_lat_tiles)
    def _():
        attend(k_ref[0, seq:, :], v_ref[0, seq:, :])


def _attention(da, lam_p, subln_g, n_lat_tiles, seq, lam_init):
    bsz, t, _ = da.shape
    tq = TOK_TILE
    return pl.pallas_call(
        functools.partial(_attn_kernel, n_lat_tiles, seq, lam_init),
        grid=(bsz, DA_HEADS, t // tq),
        in_specs=[pl.BlockSpec((1, tq, LANES), lambda b, h, i: (b, i, h)),
                  pl.BlockSpec((1, t, LANES), lambda b, h, i: (b, 0, DA_HEADS + h)),
                  pl.BlockSpec((1, t, LANES), lambda b, h, i: (b, 0, 2 * DA_HEADS + h)),
                  pl.BlockSpec(lam_p.shape, lambda b, h, i: (0, 0)),
                  pl.BlockSpec((1, LANES), lambda b, h, i: (0, 0))],
        out_specs=pl.BlockSpec((1, tq, LANES), lambda b, h, i: (b, i, h)),
        out_shape=jax.ShapeDtypeStruct((bsz, t, DA_WIDTH), BF16),
        compiler_params=_cparams(("parallel", "parallel", "parallel")),
        name="diffattn",
    )(da, da, da, lam_p, subln_g.reshape(1, LANES))


def _mlstm_kernel(ncl, ncc, ml_ref, mg_ref, cw_ref, gb_ref, ng_ref, o_ref,
                  qs_ref, ks_ref, hf_ref, hb_ref, c_ref, n_ref, m_ref):
    L = ML_CHUNK
    nc = ncl + ncc
    t_all = nc * L
    seq = ncl * L
    w = ML_WIDTH

    trow = lax.broadcasted_iota(jnp.int32, (t_all, 1), 0)
    first = (trow == 0) | (trow == seq)
    last = (trow == seq - 1) | (trow == t_all - 1)
    for j in range(2 * ML_HEADS):
        x = ml_ref[0, :, j * LANES:(j + 1) * LANES].astype(F32)
        xm = jnp.where(first, 0.0, pltpu.roll(x, 1, 0))
        xp = jnp.where(last, 0.0, pltpu.roll(x, t_all - 1, 0))
        cw = cw_ref[:, j * LANES:(j + 1) * LANES]
        y = xm * cw[0:1] + x * cw[1:2] + xp * cw[2:3]
        y = y * _sigmoid(y)
        if j < ML_HEADS:
            qs_ref[:, j * LANES:(j + 1) * LANES] = y.astype(BF16)
        else:
            jj = j - ML_HEADS
            ks_ref[:, jj * LANES:(jj + 1) * LANES] = (y * (ML_DH ** -0.5)).astype(BF16)

    c_ref[...] = jnp.zeros_like(c_ref)
    n_ref[...] = jnp.zeros_like(n_ref)
    m_ref[...] = jnp.zeros_like(m_ref)

    r_i = lax.broadcasted_iota(jnp.int32, (L, L), 0)
    c_i = lax.broadcasted_iota(jnp.int32, (L, L), 1)
    lower = r_i >= c_i
    upper = r_i <= c_i
    lower_f = lower.astype(F32)
    upper_f = upper.astype(F32)

    def step(it, carry):
        for dr in range(2):
            if dr == 0:
                ch = (it + ncl) % nc
                seen, tri_col, tri_row, end = lower, lower_f, upper_f, L - 1
                h_ref = hf_ref
            else:
                ch = nc - 1 - it
                seen, tri_col, tri_row, end = upper, upper_f, lower_f, 0
                h_ref = hb_ref
            r0 = pl.multiple_of(ch * L, L)
            g = mg_ref[0, pl.ds(r0, L), :] + gb_ref[...]
            gt = g.T
            bcol_all = jnp.dot(tri_col, _log_sigmoid(g), preferred_element_type=F32, precision=HIGHEST)
            brow_all = jnp.dot(_log_sigmoid(gt), tri_row, preferred_element_type=F32, precision=HIGHEST)
            for h in range(ML_HEADS):
                idx = dr * ML_HEADS + h
                icol = (2 * dr) * ML_HEADS + h
                fcol = (2 * dr + 1) * ML_HEADS + h
                q = qs_ref[pl.ds(r0, L), h * LANES:(h + 1) * LANES]
                k = ks_ref[pl.ds(r0, L), h * LANES:(h + 1) * LANES]
                v = ml_ref[0, pl.ds(r0, L), 2 * w + h * LANES:2 * w + (h + 1) * LANES]
                b_col = bcol_all[:, fcol:fcol + 1]
                ig_col = g[:, icol:icol + 1]
                b_row = brow_all[fcol:fcol + 1, :]
                ig_row = gt[icol:icol + 1, :]
                m_prev = m_ref[idx][:, 0:1]
                cmat = c_ref[idx]
                nrow = n_ref[idx]

                log_d = jnp.where(seen, b_col - b_row + ig_row, NEG_BIG)
                m_path = b_col + m_prev
                m_t = jnp.maximum(m_path, jnp.max(log_d, axis=-1, keepdims=True))
                dmat = jnp.exp(log_d - m_t)
                carry_w = jnp.exp(m_path - m_t)
                wmat = _dot_nt(q, k) * dmat
                num = _dot(wmat.astype(BF16), v) + carry_w * _dot(q, cmat.astype(BF16))
                den = (jnp.sum(wmat, axis=-1, keepdims=True)
                       + carry_w * jnp.sum(q.astype(F32) * nrow, axis=-1, keepdims=True))
                hval = num / jnp.maximum(jnp.abs(den), jnp.exp(-m_t))
                h_ref[pl.ds(r0, L), h * LANES:(h + 1) * LANES] = hval

                b_end = b_col[end:end + 1, :]
                log_w = b_end - b_col + ig_col
                m_new = jnp.maximum(b_end + m_prev, jnp.max(log_w, axis=0, keepdims=True))
                wk = jnp.exp(log_w - m_new)
                decay = jnp.exp(b_end + m_prev - m_new)
                kw = k.astype(F32) * wk
                c_ref[idx] = decay * cmat + _dot_tn(kw.astype(BF16), v)
                n_ref[idx] = decay * nrow + jnp.sum(kw, axis=0, keepdims=True)
                m_ref[idx] = jnp.broadcast_to(m_new, (1, LANES))
        return carry

    lax.fori_loop(0, nc, step, 0)

    for h in range(ML_HEADS):
        sl = slice(h * LANES, (h + 1) * LANES)
        hs = hf_ref[:, sl] + hb_ref[:, sl]
        y = _rms(hs, ng_ref[:, sl])
        og = ml_ref[0, :, 3 * w + h * LANES:3 * w + (h + 1) * LANES].astype(F32)
        o_ref[0, :, sl] = (y * _sigmoid(og)).astype(o_ref.dtype)


def _mlstm(ml, mg, conv_w, gate_b_row, norm_g, ncl, ncc):
    bsz, t, wtot = ml.shape
    nstate = 2 * ML_HEADS
    return pl.pallas_call(
        functools.partial(_mlstm_kernel, ncl, ncc),
        grid=(bsz,),
        in_specs=[pl.BlockSpec((1, t, wtot), lambda b: (b, 0, 0)),
                  pl.BlockSpec((1, t, LANES), lambda b: (b, 0, 0)),
                  pl.BlockSpec(conv_w.shape, lambda b: (0, 0)),
                  pl.BlockSpec((1, LANES), lambda b: (0, 0)),
                  pl.BlockSpec((1, ML_WIDTH), lambda b: (0, 0))],
        out_specs=pl.BlockSpec((1, t, ML_WIDTH), lambda b: (b, 0, 0)),
        out_shape=jax.ShapeDtypeStruct((bsz, t, ML_WIDTH), BF16),
        scratch_shapes=[pltpu.VMEM((t, ML_WIDTH), BF16), pltpu.VMEM((t, ML_WIDTH), BF16),
                        pltpu.VMEM((t, ML_WIDTH), F32), pltpu.VMEM((t, ML_WIDTH), F32),
                        pltpu.VMEM((nstate, ML_DH, ML_DH), F32),
                        pltpu.VMEM((nstate, 1, ML_DH), F32),
                        pltpu.VMEM((nstate, 1, LANES), F32)],
        compiler_params=_cparams(("parallel",)),
        name="mlstm",
    )(ml, mg, conv_w, gate_b_row, norm_g.reshape(1, ML_WIDTH))


def _merge_kernel(n_lat_tiles, n_batch, x_ref, ada_ref, g2_ref, yda_ref, yml_ref, cv_ref, pprev_ref, pnext_ref,
                  gt_ref, cw_ref, wda_ref, wml_ref, wcv_ref, wo_ref, x1_ref, h2_ref):
    b = pl.program_id(0)
    i = pl.program_id(1)
    n_tiles = pl.num_programs(1)
    d = x_ref.shape[-1]
    tm = x_ref.shape[1]
    row = jnp.where(i < n_lat_tiles, b, n_batch)

    bg = cv_ref[0, :, 0:CV_WIDTH].astype(F32)
    p = cv_ref[0, :, CV_WIDTH:].astype(F32)
    seq_start = (i == 0) | (i == n_lat_tiles)
    seq_end = (i == n_lat_tiles - 1) | (i == n_tiles - 1)
    prev_row = jnp.where(seq_start, 0.0, pprev_ref[0, HALO - 1:HALO, :].astype(F32))
    next_row = jnp.where(seq_end, 0.0, pnext_ref[0, 0:1, :].astype(F32))
    trow = lax.broadcasted_iota(jnp.int32, (tm, 1), 0)
    pm = jnp.where(trow == 0, prev_row, pltpu.roll(p, 1, 0))
    pp = jnp.where(trow == tm - 1, next_row, pltpu.roll(p, tm - 1, 0))
    cw = cw_ref[...]
    ycv = bg * (pm * cw[0:1] + p * cw[1:2] + pp * cw[2:3])

    m = (gt_ref[0, :, 0:d].astype(F32) * _dot(yda_ref[0], wda_ref[...])
         + gt_ref[0, :, d:2 * d].astype(F32) * _dot(yml_ref[0], wml_ref[...])
         + gt_ref[0, :, 2 * d:].astype(F32) * _dot(ycv.astype(BF16), wcv_ref[...]))
    out = _dot(m.astype(BF16), wo_ref[...])
    g1 = ada_ref[pl.ds(row, 1), 2 * d:3 * d]
    x1 = x_ref[0] + g1 * out
    x1_ref[0] = x1
    sh2 = ada_ref[pl.ds(row, 1), 3 * d:4 * d]
    sc2 = ada_ref[pl.ds(row, 1), 4 * d:5 * d]
    h2_ref[0] = (_rms(x1, g2_ref[...]) * (1.0 + sc2) + sh2).astype(BF16)


def _merge(x, ada, g2, yda, yml, cv, gt, cv_w, wda, wml, wcv, wo, n_lat_tiles, n_tiles):
    bsz, t, d = x.shape
    tm = TOK_TILE
    r = tm // HALO
    nrb = t // HALO
    full = lambda arr: pl.BlockSpec(arr.shape, lambda b, i: (0,) * arr.ndim, pipeline_mode=pl.Buffered(1))
    tok = lambda w: pl.BlockSpec((1, tm, w), lambda b, i: (b, i, 0))
    return pl.pallas_call(
        functools.partial(_merge_kernel, n_lat_tiles, bsz),
        grid=(bsz, n_tiles),
        in_specs=[tok(d), full(ada), full(g2), tok(DA_WIDTH), tok(ML_WIDTH), tok(2 * CV_WIDTH),
                  pl.BlockSpec((1, HALO, CV_WIDTH), lambda b, i: (b, jnp.maximum(i * r - 1, 0), 1)),
                  pl.BlockSpec((1, HALO, CV_WIDTH), lambda b, i: (b, jnp.minimum((i + 1) * r, nrb - 1), 1)),
                  tok(3 * d), full(cv_w), full(wda), full(wml), full(wcv), full(wo)],
        out_specs=[tok(d), tok(d)],
        out_shape=[jax.ShapeDtypeStruct((bsz, n_tiles * tm, d), F32),
                   jax.ShapeDtypeStruct((bsz, n_tiles * tm, d), BF16)],
        compiler_params=_cparams(("parallel", "parallel")),
        name="merge",
    )(x, ada, g2, yda, yml, cv, cv, cv, gt, cv_w, wda, wml, wcv, wo)


def _cmpx(v, i, j):
    a, b = v[i], v[j]
    v[i] = jnp.maximum(a, b)
    v[j] = jnp.minimum(a, b)


def _bitonic_merge_desc(v):
    n = len(v)
    j = n // 2
    while j >= 1:
        for i in range(n):
            l = i ^ j
            if l > i:
                _cmpx(v, i, l)
        j //= 2


def _bitonic_sort_desc(v):
    n = len(v)
    k = 2
    while k <= n:
        j = k // 2
        while j >= 1:
            for i in range(n):
                l = i ^ j
                if l > i:
                    if (i & k) == 0:
                        _cmpx(v, i, l)
                    else:
                        _cmpx(v, l, i)
            j //= 2
        k *= 2


def _merge_top(a, b):
    n = len(a)
    c = [jnp.maximum(a[r], b[n - 1 - r]) for r in range(n)]
    _bitonic_merge_desc(c)
    return c


def _top16_sorted(s):
    nk, tm = s.shape
    k = PEER_TOPK
    groups = nk // k
    assert groups == SUBLANES
    a = s.reshape(k, SUBLANES, tm)
    v = [a[r] for r in range(k)]
    _bitonic_sort_desc(v)
    sh = SUBLANES // 2
    while sh >= 1:
        v = _merge_top(v, [pltpu.roll(x, sh, 0) for x in v])
        sh //= 2
    return v


def _erf(x):
    return lax.erf(x)


def _peer_kernel(tiles_per_batch, n_lat_tiles, n_batch, n_i1, final, x_ref, h_ref, ada_ref, fg_ref, wq_ref, keys_ref,
                 u_ref, vt_ref, o_ref, s1_ref, s2_ref, f1_ref, cnt_ref, e2_ref, rk2_ref, t1_ref, t2_ref, cr_ref, rz_ref,
                 at0_ref, at1_ref, w0_ref, w1_ref, acc_ref, ht_ref):
    t = pl.program_id(0)
    s = pl.program_id(1)
    n_steps = pl.num_programs(1)
    n_chunks = (n_steps - 1) * PEER_SUBSTEPS
    d = x_ref.shape[-1]
    tm = x_ref.shape[0]
    nk = keys_ref.shape[2]
    dsub = keys_ref.shape[3]
    k = PEER_TOPK
    sort_w = PEER_SORT_W
    at_refs = (at0_ref, at1_ref)
    w_refs = (w0_ref, w1_ref)
    zero_b = jnp.zeros((), BF16)

    @pl.when(s == 0)
    def _():
        hb = h_ref[...]
        for half in range(tm // TOK_TILE):
            cs = slice(half * TOK_TILE, (half + 1) * TOK_TILE)
            ht_ref[:, cs] = hb[cs, :].astype(F32).T.astype(BF16)
        q = _dot(hb, wq_ref[...]).astype(BF16)
        for h in range(PEER_HEADS):
            for p in range(2):
                col = (h * 2 + p) * dsub
                (s1_ref if p == 0 else s2_ref)[h] = _dot_nt(keys_ref[h, p], q[:, col:col + dsub])
        neg = jnp.full((SUBLANES, sort_w), -jnp.inf, F32)
        for blk in range(tm // sort_w):
            ls = slice(blk * sort_w, (blk + 1) * sort_w)

            def sort_head(h, carry):
                for sref, tref in ((s1_ref, t1_ref), (s2_ref, t2_ref)):
                    v = _top16_sorted(sref[h, :, ls])
                    for r in range(k):
                        tref[blk, r, pl.ds(h, 1), :] = v[r][0:1, :]
                return carry

            lax.fori_loop(0, PEER_HEADS, sort_head, 0)
            t1 = [t1_ref[blk, r] for r in range(k)]
            t2 = [t2_ref[blk, r] for r in range(k)]
            rows = [[t1[r1] + t2[r2] for r2 in range(k // (r1 + 1))] for r1 in range(k)]
            best = rows[0]
            rest = [x for rw in rows[1:] for x in rw]
            while rest:
                grp, rest = rest[:k], rest[k:]
                grp = grp + [neg] * (k - len(grp))
                _bitonic_sort_desc(grp)
                best = _merge_top(best, grp)
            mx = best[0]
            z = jnp.exp(best[0] - mx)
            for r in range(1, k):
                z = z + jnp.exp(best[r] - mx)
            rz = 1.0 / z
            tau = best[k - 1]
            for r1 in range(k):
                cnt = jnp.where(rows[r1][0] >= tau, 1.0, 0.0)
                for r2 in range(1, k // (r1 + 1)):
                    cnt = cnt + jnp.where(rows[r1][r2] >= tau, 1.0, 0.0)
                cr_ref[blk, r1] = cnt
            rz_ref[blk] = rz

            def finish_head(h, carry):
                hs = pl.ds(h, 1)
                s1h = s1_ref[h, :, ls]
                s2h = s2_ref[h, :, ls]
                cn = jnp.zeros_like(s1h)
                rk = jnp.full_like(s2h, float(k))
                for r in reversed(range(k)):
                    cn = jnp.where(s1h == t1_ref[blk, r, hs, :], cr_ref[blk, r, hs, :], cn)
                    rk = jnp.where(s2h >= t2_ref[blk, r, hs, :], float(r), rk)
                cnt_ref[h, :, ls] = cn
                rk2_ref[h, :, ls] = rk.astype(BF16)
                f1_ref[h, :, ls] = jnp.exp(s1h - t1_ref[blk, 0, hs, :]) * rz_ref[blk, hs, :]
                e2_ref[h, :, ls] = jnp.exp(s2h - t2_ref[blk, 0, hs, :]).astype(BF16)
                return carry

            lax.fori_loop(0, PEER_HEADS, finish_head, 0)
        acc_ref[...] = jnp.zeros_like(acc_ref)
        at_refs[1][...] = jnp.zeros_like(at_refs[1])
        w_refs[0][...] = jnp.zeros_like(w_refs[0])

    ec = n_i1 * nk
    rb_rows = nk // 2
    n_grp = n_i1 // PEER_GROUP
    d_rows = d // n_grp
    e_rows = ec // n_grp
    for sub in range(PEER_SUBSTEPS):
        at_cur, at_oth = at_refs[sub], at_refs[1 - sub]
        w_cur, w_oth = w_refs[sub], w_refs[1 - sub]
        cb = jnp.clip(s * PEER_SUBSTEPS + sub - 1, 0, n_chunks - 1)

        def group(grp, carry, sub=sub, at_cur=at_cur, at_oth=at_oth, w_cur=w_cur, w_oth=w_oth, cb=cb):
            e_off = pl.multiple_of(grp * e_rows, e_rows)
            u_off = pl.multiple_of((sub * ec + e_off) // 2, e_rows // 2)
            u_rows = pltpu.bitcast(u_ref[pl.ds(u_off, e_rows // 2), :], BF16)
            at_cur[pl.ds(e_off, e_rows), :] = _dot(u_rows, ht_ref[...])
            d_off = pl.multiple_of(grp * d_rows, d_rows)
            v_off = pl.multiple_of(d_off // 2, d_rows // 2)
            v_rows = pltpu.bitcast(vt_ref[pl.ds(v_off, d_rows // 2), sub * ec:(sub + 1) * ec], BF16)
            acc_ref[pl.ds(d_off, d_rows), :] += _dot(v_rows, w_cur[...])

            i1s = [cb * n_i1 + grp * PEER_GROUP + jj for jj in range(PEER_GROUP)]
            cnrows = [[cnt_ref[h, pl.ds(i1, 1), :].astype(BF16) for i1 in i1s] for h in range(PEER_HEADS)]
            f1rows = [[f1_ref[h, pl.ds(i1, 1), :].astype(BF16) for i1 in i1s] for h in range(PEER_HEADS)]
            for tb in range(tm // LANES):
                ts = slice(tb * LANES, (tb + 1) * LANES)
                for rb in range(nk // rb_rows):
                    rs = slice(rb * rb_rows, (rb + 1) * rb_rows)
                    gs = [jnp.zeros((rb_rows, LANES), BF16) for _ in range(PEER_GROUP)]
                    for h in range(PEER_HEADS):
                        rk2 = rk2_ref[h, rs, ts]
                        e2b = e2_ref[h, rs, ts]
                        for jj in range(PEER_GROUP):
                            cnr = cnrows[h][jj][:, ts]
                            f1r = f1rows[h][jj][:, ts]
                            gs[jj] = gs[jj] + jnp.where(rk2 < cnr, e2b * f1r, zero_b)
                    for jj in range(PEER_GROUP):
                        r0 = pl.multiple_of(e_off + jj * nk + rb * rb_rows, rb_rows)
                        a = at_oth[pl.ds(r0, rb_rows), ts]
                        act = 0.5 * a * (1.0 + _erf(a * (2.0 ** -0.5)))
                        w_oth[pl.ds(r0, rb_rows), ts] = act.astype(BF16) * gs[jj]
            return carry

        lax.fori_loop(0, n_grp, group, 0)

    @pl.when(s == n_steps - 1)
    def _():
        for half in range(tm // TOK_TILE):
            g256 = t * (tm // TOK_TILE) + half
            row = jnp.where(g256 % tiles_per_batch < n_lat_tiles, g256 // tiles_per_batch, n_batch)
            g2 = ada_ref[pl.ds(row, 1), 5 * d:6 * d]
            rs = slice(half * TOK_TILE, (half + 1) * TOK_TILE)
            x2 = x_ref[rs, :] + g2 * acc_ref[:, rs].T
            if final:
                x2 = _rms(x2, fg_ref[...])
            o_ref[rs, :] = x2


def _peer(x1, h2, ada, final_g, wq, keys, u, vt, tiles_per_batch, n_lat_tiles, n_batch, final):
    rows, d = x1.shape
    tm = PEER_TILE
    nk = keys.shape[2]
    n_i1 = PEER_CHUNK_I1
    ec = n_i1 * nk
    n_blocks = 2 * u.shape[0] // (PEER_SUBSTEPS * ec)
    n_sort = tm // PEER_SORT_W
    assert PEER_HEADS == SUBLANES
    full = lambda arr: pl.BlockSpec(arr.shape, lambda t, s: (0,) * arr.ndim, pipeline_mode=pl.Buffered(1))
    tok = pl.BlockSpec((tm, d), lambda t, s: (t, 0))
    return pl.pallas_call(
        functools.partial(_peer_kernel, tiles_per_batch, n_lat_tiles, n_batch, n_i1, final),
        grid=(rows // tm, n_blocks + 1),
        in_specs=[tok, tok, full(ada), full(final_g), full(wq), full(keys),
                  pl.BlockSpec((PEER_SUBSTEPS * ec // 2, d), lambda t, s: (jnp.minimum(s, n_blocks - 1), 0)),
                  pl.BlockSpec((d // 2, PEER_SUBSTEPS * ec), lambda t, s: (0, jnp.maximum(s - 1, 0)))],
        out_specs=tok,
        out_shape=jax.ShapeDtypeStruct((rows, d), F32),
        scratch_shapes=[pltpu.VMEM((PEER_HEADS, nk, tm), F32), pltpu.VMEM((PEER_HEADS, nk, tm), F32),
                        pltpu.VMEM((PEER_HEADS, nk, tm), F32), pltpu.VMEM((PEER_HEADS, nk, tm), F32),
                        pltpu.VMEM((PEER_HEADS, nk, tm), BF16), pltpu.VMEM((PEER_HEADS, nk, tm), BF16),
                        pltpu.VMEM((n_sort, PEER_TOPK, PEER_HEADS, PEER_SORT_W), F32),
                        pltpu.VMEM((n_sort, PEER_TOPK, PEER_HEADS, PEER_SORT_W), F32),
                        pltpu.VMEM((n_sort, PEER_TOPK, PEER_HEADS, PEER_SORT_W), F32),
                        pltpu.VMEM((n_sort, PEER_HEADS, PEER_SORT_W), F32),
                        pltpu.VMEM((ec, tm), F32), pltpu.VMEM((ec, tm), F32),
                        pltpu.VMEM((ec, tm), BF16), pltpu.VMEM((ec, tm), BF16),
                        pltpu.VMEM((d, tm), F32), pltpu.VMEM((d, tm), BF16)],
        compiler_params=_cparams(("parallel", "arbitrary")),
        name="peer",
    )(x1, h2, ada, final_g, wq, keys, u, vt)


def _rope_tables(seq, t):
    quarter = DA_QK // 4
    inv = ROPE_BASE ** (-np.arange(quarter, dtype=np.float32) / quarter)
    pos = np.arange(seq)
    ang_r = (pos // GRID_W).astype(np.float32)[:, None] * inv[None, :]
    ang_c = (pos % GRID_W).astype(np.float32)[:, None] * inv[None, :]
    half = np.concatenate([ang_r, ang_c, ang_r, ang_c], axis=1)
    cos = np.ones((t, LANES), np.float32)
    sin = np.zeros((t, LANES), np.float32)
    cos[:seq] = np.concatenate([np.cos(half), np.cos(half)], axis=1)
    sin[:seq] = np.concatenate([-np.sin(half), np.sin(half)], axis=1)
    return jnp.asarray(cos), jnp.asarray(sin)


def _qk_perm():
    quarter = DA_QK // 4
    first, second = [], []
    for m in range(2):
        for part in range(2):
            base = m * DA_QK + part * (DA_QK // 2)
            first += [base + j for j in range(quarter)]
            second += [base + quarter + j for j in range(quarter)]
    return np.array(first + second)


def _pack_row_pairs(a):
    *lead, r, c = a.shape
    pairs = jnp.swapaxes(a.reshape(*lead, r // 2, 2, c), -1, -2)
    return lax.bitcast_convert_type(pairs, jnp.uint32)


def kernel(x, c, ctx, c_ctx, w_ada, b_ada, norm1_g, norm2_g, w_in, ml_gate_b, ml_conv_w, ml_norm_g, cv_conv_w,
           da_lam, da_subln_g, w_da, w_ml, w_cv, w_o, peer_wq, peer_keys, peer_u, peer_v, final_g):
    bsz, seq, d = x.shape
    n_ctx = ctx.shape[1]
    t = seq + n_ctx
    depth = w_ada.shape[0]
    tm = TOK_TILE
    assert seq % tm == 0 and n_ctx % tm == 0 and seq % ML_CHUNK == 0 and n_ctx % ML_CHUNK == 0
    assert (bsz * t) % PEER_TILE == 0 and (bsz * seq) % PEER_TILE == 0
    n_lat_tiles = seq // tm
    n_tiles = t // tm

    rows = -(-(bsz + 1) // SUBLANES) * SUBLANES
    cc = jnp.zeros((rows, d), F32).at[:bsz].set(c).at[bsz].set(c_ctx)
    ada = _ada(cc, w_ada, b_ada)

    o = np.cumsum([0, DA_WIDTH, DA_WIDTH, DA_WIDTH, 2 * ML_WIDTH, ML_WIDTH, ML_WIDTH, 4 * ML_HEADS,
                   CV_WIDTH, CV_WIDTH, CV_WIDTH, 3 * d])
    perm = _qk_perm()
    qk_cols = np.concatenate([o[g] + h * DA_V + perm for g in range(2) for h in range(DA_HEADS)])
    da_cols = np.concatenate([qk_cols, np.arange(o[2], o[3])])
    wda_in = w_in[:, :, da_cols].astype(BF16)
    wml_in = w_in[:, :, o[3]:o[6]].astype(BF16)
    wmg_in = jnp.pad(w_in[:, :, o[6]:o[7]], ((0, 0), (0, 0), (0, LANES - 4 * ML_HEADS))).astype(BF16)
    wcv_in = w_in[:, :, o[7]:o[10]].astype(BF16)
    wgt_in = w_in[:, :, o[10]:o[11]].astype(BF16)
    gate_b = jnp.pad(ml_gate_b.reshape(depth, 1, 4 * ML_HEADS), ((0, 0), (0, 0), (0, LANES - 4 * ML_HEADS)))
    lam_p = da_lam
    w_da_b, w_ml_b, w_cv_b, w_o_b = (a.astype(BF16) for a in (w_da, w_ml, w_cv, w_o))
    wq_b = peer_wq.astype(BF16)
    keys_b = peer_keys.astype(BF16)
    u_b = _pack_row_pairs(peer_u.astype(BF16))
    vt_b = _pack_row_pairs(jnp.swapaxes(peer_v, 1, 2).astype(BF16))
    cos, sin = _rope_tables(seq, t)

    xs = jnp.concatenate([x, ctx], axis=1)
    for l in range(depth):
        need_ctx = l < depth - 1
        lam_init = 0.8 - 0.6 * math.exp(-0.3 * l)
        da, ml, mg, cv, gt = _inproj(xs, ada[l], norm1_g[l].reshape(1, d), cos, sin,
                                     wda_in[l], wml_in[l], wmg_in[l], wcv_in[l], wgt_in[l], n_lat_tiles)
        yda = _attention(da, lam_p[l], da_subln_g[l], n_lat_tiles, seq, lam_init)
        yml = _mlstm(ml, mg, ml_conv_w[l], gate_b[l], ml_norm_g[l], seq // ML_CHUNK, n_ctx // ML_CHUNK)
        n_out_tiles = n_tiles if need_ctx else n_lat_tiles
        x1, h2 = _merge(xs, ada[l], norm2_g[l].reshape(1, d), yda, yml, cv, gt, cv_conv_w[l],
                        w_da_b[l], w_ml_b[l], w_cv_b[l], w_o_b[l], n_lat_tiles, n_out_tiles)
        t_out = n_out_tiles * tm
        xs = _peer(x1.reshape(bsz * t_out, d), h2.reshape(bsz * t_out, d), ada[l], final_g.reshape(1, d),
                   wq_b[l], keys_b[l], u_b[l], vt_b[l], n_out_tiles, n_lat_tiles, bsz, final=not need_ctx)
        xs = xs.reshape(bsz, t_out, d)
    return xs
```

```python
import functools
import math

import numpy as np
import jax
import jax.numpy as jnp
from jax import lax
from jax.experimental import pallas as pl
from jax.experimental.pallas import tpu as pltpu

EPS = 1e-6
GRID_W = 64
DA_HEADS = 4
DA_QK = 64
DA_V = 2 * DA_QK
DA_WIDTH = DA_HEADS * DA_V
ROPE_BASE = 10000.0
ML_HEADS = 4
ML_DH = 128
ML_WIDTH = ML_HEADS * ML_DH
ML_CHUNK = 128
CV_WIDTH = 512
PEER_HEADS = 8
PEER_TOPK = 16
PEER_GROUP = 2

LANES = 128
SUBLANES = 8
TOK_TILE = 256
HALO = 16
PEER_TILE = 512
PEER_CHUNK_I1 = 8
PEER_SUBSTEPS = 2
PACK_ROWS = 512
PEER_MM_SLICES = 2
PEER_SORT_W = 2 * LANES
VMEM_LIMIT = 56 * 1024 * 1024

NEG_BIG = -1e30
BF16 = jnp.bfloat16
F32 = jnp.float32
HIGHEST = lax.Precision.HIGHEST


def _cparams(sem):
    return pltpu.CompilerParams(dimension_semantics=sem, vmem_limit_bytes=VMEM_LIMIT)


def _dot(a, b):
    return jnp.dot(a, b, preferred_element_type=F32)


def _dot_nt(a, b):
    return lax.dot_general(a, b, (((1,), (1,)), ((), ())), preferred_element_type=F32)


def _dot_tn(a, b):
    return lax.dot_general(a, b, (((0,), (0,)), ((), ())), preferred_element_type=F32)


def _rms(x, g):
    return x * lax.rsqrt(jnp.mean(x * x, axis=-1, keepdims=True) + EPS) * g


def _sigmoid(x):
    return 1.0 / (1.0 + jnp.exp(-x))


def _log_sigmoid(x):
    return jnp.minimum(x, 0.0) - jnp.log1p(jnp.exp(-jnp.abs(x)))


def _ada_kernel(c_ref, w_ref, b_ref, o_ref):
    c = c_ref[...]
    s = c * _sigmoid(c)
    o_ref[0] = jnp.dot(s, w_ref[0], preferred_element_type=F32, precision=HIGHEST) + b_ref[0]


def _ada(cc, w_ada, b_ada):
    depth, d, d6 = w_ada.shape
    rows = cc.shape[0]
    nb = d6 // d
    return pl.pallas_call(
        _ada_kernel,
        grid=(depth, nb),
        in_specs=[pl.BlockSpec((rows, d), lambda l, j: (0, 0)),
                  pl.BlockSpec((1, d, d), lambda l, j: (l, 0, j)),
                  pl.BlockSpec((1, 1, d), lambda l, j: (l, 0, j))],
        out_specs=pl.BlockSpec((1, rows, d), lambda l, j: (l, 0, j)),
        out_shape=jax.ShapeDtypeStruct((depth, rows, d6), F32),
        compiler_params=_cparams(("parallel", "parallel")),
        name="ada",
    )(cc, w_ada, b_ada.reshape(depth, 1, d6))


def _inproj_kernel(n_lat_tiles, n_batch, x_ref, ada_ref, g_ref, cos_ref, sin_ref,
                   wda_ref, wml_ref, wmg_ref, wcv_ref, wgt_ref,
                   da_ref, ml_ref, mg_ref, cv_ref, gt_ref):
    b = pl.program_id(0)
    i = pl.program_id(1)
    d = x_ref.shape[-1]
    row = jnp.where(i < n_lat_tiles, b, n_batch)
    shift = ada_ref[pl.ds(row, 1), 0:d]
    scale = ada_ref[pl.ds(row, 1), d:2 * d]
    h = _rms(x_ref[0], g_ref[...]) * (1.0 + scale) + shift
    hb = h.astype(BF16)

    zda = _dot(hb, wda_ref[...])
    cos = cos_ref[...]
    sin = sin_ref[...]
    for j in range(2 * DA_HEADS):
        blk = zda[:, j * LANES:(j + 1) * LANES]
        rot = blk * cos + pltpu.roll(blk, LANES // 2, 1) * sin
        da_ref[0, :, j * LANES:(j + 1) * LANES] = rot.astype(BF16)
    da_ref[0, :, 2 * DA_WIDTH:] = zda[:, 2 * DA_WIDTH:].astype(BF16)

    ml_ref[0] = _dot(hb, wml_ref[...]).astype(BF16)
    mg_ref[0] = _dot(hb, wmg_ref[...])

    zcv = _dot(hb, wcv_ref[...])
    cv_ref[0, :, 0:CV_WIDTH] = zcv[:, 0:CV_WIDTH].astype(BF16)
    cv_ref[0, :, CV_WIDTH:] = (zcv[:, CV_WIDTH:2 * CV_WIDTH] * zcv[:, 2 * CV_WIDTH:]).astype(BF16)

    for j in range(3):
        zg = _dot(hb, wgt_ref[:, j * d:(j + 1) * d])
        gt_ref[0, :, j * d:(j + 1) * d] = _sigmoid(zg).astype(BF16)


def _inproj(x, ada, g, cos, sin, wda, wml, wmg, wcv, wgt, n_lat_tiles):
    bsz, t, d = x.shape
    tm = TOK_TILE
    full = lambda arr: pl.BlockSpec(arr.shape, lambda b, i: (0,) * arr.ndim, pipeline_mode=pl.Buffered(1))
    tok = lambda w: pl.BlockSpec((1, tm, w), lambda b, i: (b, i, 0))
    widths =(wda.shape[1], wml.shape[1], wmg.shape[1], 2 * CV_WIDTH, wgt.shape[1])
    dts = (BF16, BF16, F32, BF16, BF16)
    return pl.pallas_call(
        functools.partial(_inproj_kernel, n_lat_tiles, bsz),
        grid=(bsz, t // tm),
        in_specs=[tok(d), full(ada), full(g),
                  pl.BlockSpec((tm, LANES), lambda b, i: (i, 0)),
                  pl.BlockSpec((tm, LANES), lambda b, i: (i, 0)),
                  full(wda), full(wml), full(wmg), full(wcv), full(wgt)],
        out_specs=[tok(w) for w in widths],
        out_shape=[jax.ShapeDtypeStruct((bsz, t, w), dt) for w, dt in zip(widths, dts)],
        compiler_params=_cparams(("parallel", "parallel")),
        name="inproj",
    )(x, ada, g, cos, sin, wda, wml, wmg, wcv, wgt)


def _attn_kernel(n_lat_tiles, seq, lam_init, q_ref, k_ref, v_ref, lam_ref, g_ref, o_ref, va_ref):
    i = pl.program_id(2)
    lv = lam_ref[...]
    lam = (jnp.exp(jnp.sum(lv[0:1] * lv[1:2], axis=-1, keepdims=True))
           - jnp.exp(jnp.sum(lv[2:3] * lv[3:4], axis=-1, keepdims=True)) + lam_init)
    lane = lax.broadcasted_iota(jnp.int32, (1, LANES), 1)
    map0 = (lane % (LANES // 2)) < (DA_QK // 2)

    @pl.when(i == 0)
    def _():
        va_ref[:, 0:LANES] = v_ref[0]
        va_ref[:, LANES:] = jnp.ones((va_ref.shape[0], LANES), BF16)

    def attend(k, va):
        q = q_ref[0].astype(F32) * (DA_QK ** -0.5)
        q0 = jnp.where(map0, q, 0.0).astype(BF16)
        q1 = jnp.where(map0, 0.0, q).astype(BF16)
        s0 = _dot_nt(q0, k)
        s1 = _dot_nt(q1, k)
        e0 = jnp.exp((s0 - jnp.max(s0, axis=-1, keepdims=True)).astype(BF16))
        e1 = jnp.exp((s1 - jnp.max(s1, axis=-1, keepdims=True)).astype(BF16))
        o0 = _dot(e0, va)
        o1 = _dot(e1, va)
        o = o0[:, 0:LANES] / o0[:, LANES:] - lam * (o1[:, 0:LANES] / o1[:, LANES:])
        o_ref[0] = (_rms(o, g_ref[...]) * (1.0 - lam_init)).astype(o_ref.dtype)

    @pl.when(i < n_lat_tiles)
    def _():
        attend(k_ref[0], va_ref[...])

    @pl.when(i >= n_lat_tiles)
    def _():
        attend(k_ref[0, seq:, :], va_ref[seq:, :])


def _attention(da, lam_p, subln_g, n_lat_tiles, seq, lam_init):
    bsz, t, _ = da.shape
    tq = TOK_TILE
    return pl.pallas_call(
        functools.partial(_attn_kernel, n_lat_tiles, seq, lam_init),
        grid=(bsz, DA_HEADS, t // tq),
        in_specs=[pl.BlockSpec((1, tq, LANES), lambda b, h, i: (b, i, h)),
                  pl.BlockSpec((1, t, LANES), lambda b, h, i: (b, 0, DA_HEADS + h)),
                  pl.BlockSpec((1, t, LANES), lambda b, h, i: (b, 0, 2 * DA_HEADS + h)),
                  pl.BlockSpec(lam_p.shape, lambda b, h, i: (0, 0)),
                  pl.BlockSpec((1, LANES), lambda b, h, i: (0, 0))],
        out_specs=pl.BlockSpec((1, tq, LANES), lambda b, h, i: (b, i, h)),
        out_shape=jax.ShapeDtypeStruct((bsz, t, DA_WIDTH), BF16),
        scratch_shapes=[pltpu.VMEM((t, 2 * LANES), BF16)],
        compiler_params=_cparams(("parallel", "parallel", "arbitrary")),
        name="diffattn",
    )(da, da, da, lam_p, subln_g.reshape(1, LANES))


def _mlstm_kernel(ncl, ncc, ml_ref, mg_ref, cw_ref, gb_ref, ng_ref, o_ref,
                  qs_ref, ks_ref, hf_ref, hb_ref, c_ref, n_ref, m_ref):
    L = ML_CHUNK
    nc = ncl + ncc
    t_all = nc * L
    seq = ncl * L
    w = ML_WIDTH

    trow = lax.broadcasted_iota(jnp.int32, (t_all, 1), 0)
    first = (trow == 0) | (trow == seq)
    last = (trow == seq - 1) | (trow == t_all - 1)
    for j in range(2 * ML_HEADS):
        x = ml_ref[0, :, j * LANES:(j + 1) * LANES].astype(F32)
        xm = jnp.where(first, 0.0, pltpu.roll(x, 1, 0))
        xp = jnp.where(last, 0.0, pltpu.roll(x, t_all - 1, 0))
        cw = cw_ref[:, j * LANES:(j + 1) * LANES]
        y = xm * cw[0:1] + x * cw[1:2] + xp * cw[2:3]
        y = y * _sigmoid(y)
        if j < ML_HEADS:
            qs_ref[:, j * LANES:(j + 1) * LANES] = y.astype(BF16)
        else:
            jj = j - ML_HEADS
            ks_ref[:, jj * LANES:(jj + 1) * LANES] = (y * (ML_DH ** -0.5)).astype(BF16)

    c_ref[...] = jnp.zeros_like(c_ref)
    n_ref[...] = jnp.zeros_like(n_ref)
    m_ref[...] = jnp.zeros_like(m_ref)

    r_i = lax.broadcasted_iota(jnp.int32, (L, L), 0)
    c_i = lax.broadcasted_iota(jnp.int32, (L, L), 1)
    lower = r_i >= c_i
    upper = r_i <= c_i
    lower_f = lower.astype(F32)
    upper_f = upper.astype(F32)

    def step(it, carry):
        for dr in range(2):
            if dr == 0:
                ch = (it + ncl) % nc
                seen, tri_col, tri_row, end = lower, lower_f, upper_f, L - 1
                h_ref = hf_ref
            else:
                ch = nc - 1 - it
                seen, tri_col, tri_row, end = upper, upper_f, lower_f, 0
                h_ref = hb_ref
            r0 = pl.multiple_of(ch * L, L)
            g = mg_ref[0, pl.ds(r0, L), :] + gb_ref[...]
            gt = g.T
            bcol_all = jnp.dot(tri_col, _log_sigmoid(g), preferred_element_type=F32, precision=HIGHEST)
            brow_all = jnp.dot(_log_sigmoid(gt), tri_row, preferred_element_type=F32, precision=HIGHEST)
            for h in range(ML_HEADS):
                idx = dr * ML_HEADS + h
                icol = (2 * dr) * ML_HEADS + h
                fcol = (2 * dr + 1) * ML_HEADS + h
                q = qs_ref[pl.ds(r0, L), h * LANES:(h + 1) * LANES]
                k = ks_ref[pl.ds(r0, L), h * LANES:(h + 1) * LANES]
                v = ml_ref[0, pl.ds(r0, L), 2 * w + h * LANES:2 * w + (h + 1) * LANES]
                b_col = bcol_all[:, fcol:fcol + 1]
                ig_col = g[:, icol:icol + 1]
                b_row = brow_all[fcol:fcol + 1, :]
                ig_row = gt[icol:icol + 1, :]
                m_prev = m_ref[idx][:, 0:1]
                cmat = c_ref[idx]
                nrow = n_ref[idx]

                log_d = jnp.where(seen, b_col - b_row + ig_row, NEG_BIG)
                m_path = b_col + m_prev
                m_t = jnp.maximum(m_path, jnp.max(log_d, axis=-1, keepdims=True))
                dmat = jnp.exp(log_d - m_t)
                carry_w = jnp.exp(m_path - m_t)
                wmat = _dot_nt(q, k) * dmat
                num = _dot(wmat.astype(BF16), v) + carry_w * _dot(q, cmat.astype(BF16))
                den = (jnp.sum(wmat, axis=-1, keepdims=True)
                       + carry_w * jnp.sum(q.astype(F32) * nrow, axis=-1, keepdims=True))
                hval = num / jnp.maximum(jnp.abs(den), jnp.exp(-m_t))
                h_ref[pl.ds(r0, L), h * LANES:(h + 1) * LANES] = hval

                b_end = b_col[end:end + 1, :]
                log_w = b_end - b_col + ig_col
                m_new = jnp.maximum(b_end + m_prev, jnp.max(log_w, axis=0, keepdims=True))
                wk = jnp.exp(log_w - m_new)
                decay = jnp.exp(b_end + m_prev - m_new)
                kw = k.astype(F32) * wk
                c_ref[idx] = decay * cmat + _dot_tn(kw.astype(BF16), v)
                n_ref[idx] = decay * nrow + jnp.sum(kw, axis=0, keepdims=True)
                m_ref[idx] = jnp.broadcast_to(m_new, (1, LANES))
        return carry

    lax.fori_loop(0, nc, step, 0)

    for h in range(ML_HEADS):
        sl = slice(h * LANES, (h + 1) * LANES)
        hs = hf_ref[:, sl] + hb_ref[:, sl]
        y = _rms(hs, ng_ref[:, sl])
        og = ml_ref[0, :, 3 * w + h * LANES:3 * w + (h + 1) * LANES].astype(F32)
        o_ref[0, :, sl] = (y * _sigmoid(og)).astype(o_ref.dtype)


def _mlstm(ml, mg, conv_w, gate_b_row, norm_g, ncl, ncc):
    bsz, t, wtot = ml.shape
    nstate = 2 * ML_HEADS
    return pl.pallas_call(
        functools.partial(_mlstm_kernel, ncl, ncc),
        grid=(bsz,),
        in_specs=[pl.BlockSpec((1, t, wtot), lambda b: (b, 0, 0)),
                  pl.BlockSpec((1, t, LANES), lambda b: (b, 0, 0)),
                  pl.BlockSpec(conv_w.shape, lambda b: (0, 0)),
                  pl.BlockSpec((1, LANES), lambda b: (0, 0)),
                  pl.BlockSpec((1, ML_WIDTH), lambda b: (0, 0))],
        out_specs=pl.BlockSpec((1, t, ML_WIDTH), lambda b: (b, 0, 0)),
        out_shape=jax.ShapeDtypeStruct((bsz, t, ML_WIDTH), BF16),
        scratch_shapes=[pltpu.VMEM((t, ML_WIDTH), BF16), pltpu.VMEM((t, ML_WIDTH), BF16),
                        pltpu.VMEM((t, ML_WIDTH), F32), pltpu.VMEM((t, ML_WIDTH), F32),
                        pltpu.VMEM((nstate, ML_DH, ML_DH), F32),
                        pltpu.VMEM((nstate, 1, ML_DH), F32),
                        pltpu.VMEM((nstate, 1, LANES), F32)],
        compiler_params=_cparams(("parallel",)),
        name="mlstm",
    )(ml, mg, conv_w, gate_b_row, norm_g.reshape(1, ML_WIDTH))


def _merge_kernel(n_lat_tiles, n_batch, x_ref, ada_ref, g2_ref, yda_ref, yml_ref, cv_ref, pprev_ref, pnext_ref,
                  gt_ref, cw_ref, wda_ref, wml_ref, wcv_ref, wo_ref, x1_ref, h2_ref):
    b = pl.program_id(0)
    i = pl.program_id(1)
    n_tiles = pl.num_programs(1)
    d = x_ref.shape[-1]
    tm = x_ref.shape[1]
    row = jnp.where(i < n_lat_tiles, b, n_batch)

    bg = cv_ref[0, :, 0:CV_WIDTH].astype(F32)
    p = cv_ref[0, :, CV_WIDTH:].astype(F32)
    seq_start = (i == 0) | (i == n_lat_tiles)
    seq_end = (i == n_lat_tiles - 1) | (i == n_tiles - 1)
    prev_row = jnp.where(seq_start, 0.0, pprev_ref[0, HALO - 1:HALO, :].astype(F32))
    next_row = jnp.where(seq_end, 0.0, pnext_ref[0, 0:1, :].astype(F32))
    trow = lax.broadcasted_iota(jnp.int32, (tm, 1), 0)
    pm = jnp.where(trow == 0, prev_row, pltpu.roll(p, 1, 0))
    pp = jnp.where(trow == tm - 1, next_row, pltpu.roll(p, tm - 1, 0))
    cw = cw_ref[...]
    ycv = bg * (pm * cw[0:1] + p * cw[1:2] + pp * cw[2:3])

    m = (gt_ref[0, :, 0:d].astype(F32) * _dot(yda_ref[0], wda_ref[...])
         + gt_ref[0, :, d:2 * d].astype(F32) * _dot(yml_ref[0], wml_ref[...])
         + gt_ref[0, :, 2 * d:].astype(F32) * _dot(ycv.astype(BF16), wcv_ref[...]))
    out = _dot(m.astype(BF16), wo_ref[...])
    g1 = ada_ref[pl.ds(row, 1), 2 * d:3 * d]
    x1 = x_ref[0] + g1 * out
    x1_ref[0] = x1
    sh2 = ada_ref[pl.ds(row, 1), 3 * d:4 * d]
    sc2 = ada_ref[pl.ds(row, 1), 4 * d:5 * d]
    h2_ref[0] = (_rms(x1, g2_ref[...]) * (1.0 + sc2) + sh2).astype(BF16)


def _merge(x, ada, g2, yda, yml, cv, gt, cv_w, wda, wml, wcv, wo, n_lat_tiles, n_tiles):
    bsz, t, d = x.shape
    tm = TOK_TILE
    r = tm // HALO
    nrb = t // HALO
    full = lambda arr: pl.BlockSpec(arr.shape, lambda b, i: (0,) * arr.ndim, pipeline_mode=pl.Buffered(1))
    tok = lambda w: pl.BlockSpec((1, tm, w), lambda b, i: (b, i, 0))
    return pl.pallas_call(
        functools.partial(_merge_kernel, n_lat_tiles, bsz),
        grid=(bsz, n_tiles),
        in_specs=[tok(d), full(ada), full(g2), tok(DA_WIDTH), tok(ML_WIDTH), tok(2 * CV_WIDTH),
                  pl.BlockSpec((1, HALO, CV_WIDTH), lambda b, i: (b, jnp.maximum(i * r - 1, 0), 1)),
                  pl.BlockSpec((1, HALO, CV_WIDTH), lambda b, i: (b, jnp.minimum((i + 1) * r, nrb - 1), 1)),
                  tok(3 * d), full(cv_w), full(wda), full(wml), full(wcv), full(wo)],
        out_specs=[tok(d), tok(d)],
        out_shape=[jax.ShapeDtypeStruct((bsz, n_tiles * tm, d), F32),
                   jax.ShapeDtypeStruct((bsz, n_tiles * tm, d), BF16)],
        compiler_params=_cparams(("parallel", "parallel")),
        name="merge",
    )(x, ada, g2, yda, yml, cv, cv, cv, gt, cv_w, wda, wml, wcv, wo)


def _cmpx(v, i, j):
    a, b = v[i], v[j]
    v[i] = jnp.maximum(a, b)
    v[j] = jnp.minimum(a, b)


def _bitonic_merge_desc(v):
    n = len(v)
    j = n // 2
    while j >= 1:
        for i in range(n):
            l = i ^ j
            if l > i:
                _cmpx(v, i, l)
        j //= 2


def _bitonic_sort_desc(v):
    n = len(v)
    k = 2
    while k <= n:
        j = k // 2
        while j >= 1:
            for i in range(n):
                l = i ^ j
                if l > i:
                    if (i & k) == 0:
                        _cmpx(v, i, l)
                    else:
                        _cmpx(v, l, i)
            j //= 2
        k *= 2


def _merge_top(a, b):
    n = len(a)
    c = [jnp.maximum(a[r], b[n - 1 - r]) for r in range(n)]
    _bitonic_merge_desc(c)
    return c


def _top16_sorted(s):
    nk, tm = s.shape
    k = PEER_TOPK
    groups = nk // k
    assert groups == SUBLANES
    a = s.reshape(k, SUBLANES, tm)
    v = [a[r] for r in range(k)]
    _bitonic_sort_desc(v)
    sh = SUBLANES // 2
    while sh >= 1:
        v = _merge_top(v, [pltpu.roll(x, sh, 0) for x in v])
        sh //= 2
    return v


def _erf(x):
    return lax.erf(x)


def _peer_kernel(tiles_per_batch, n_lat_tiles, n_batch, n_i1, final, x_ref, h_ref, ada_ref, fg_ref, wq_ref, keys_ref,
                 u_ref, vt_ref, o_ref, s1_ref, s2_ref, f1_ref, cnt_ref, e2_ref, rk2_ref, t1_ref, t2_ref, cr_ref, rz_ref,
                 at0_ref, at1_ref, w0_ref, w1_ref, acc_ref, ht_ref):
    t = pl.program_id(0)
    s = pl.program_id(1)
    n_steps = pl.num_programs(1)
    n_chunks = (n_steps - 1) * PEER_SUBSTEPS
    d = x_ref.shape[-1]
    tm = x_ref.shape[0]
    nk = keys_ref.shape[2]
    dsub = keys_ref.shape[3]
    k = PEER_TOPK
    sort_w = PEER_SORT_W
    at_refs = (at0_ref, at1_ref)
    w_refs = (w0_ref, w1_ref)
    zero_b = jnp.zeros((), BF16)

    @pl.when(s == 0)
    def _():
        hb = h_ref[...]
        for half in range(tm // TOK_TILE):
            cs = slice(half * TOK_TILE, (half + 1) * TOK_TILE)
            ht_ref[:, cs] = hb[cs, :].astype(F32).T.astype(BF16)
        q = _dot(hb, wq_ref[...]).astype(BF16)
        for h in range(PEER_HEADS):
            for p in range(2):
                col = (h * 2 + p) * dsub
                (s1_ref if p == 0 else s2_ref)[h] = _dot_nt(keys_ref[h, p], q[:, col:col + dsub])
        neg = jnp.full((SUBLANES, sort_w), -jnp.inf, F32)
        for blk in range(tm // sort_w):
            ls = slice(blk * sort_w, (blk + 1) * sort_w)

            def sort_head(h, carry):
                for sref, tref in ((s1_ref, t1_ref), (s2_ref, t2_ref)):
                    v = _top16_sorted(sref[h, :, ls])
                    for r in range(k):
                        tref[blk, r, pl.ds(h, 1), :] = v[r][0:1, :]
                return carry

            lax.fori_loop(0, PEER_HEADS, sort_head, 0)
            t1 = [t1_ref[blk, r] for r in range(k)]
            t2 = [t2_ref[blk, r] for r in range(k)]
            rows = [[t1[r1] + t2[r2] for r2 in range(k // (r1 + 1))] for r1 in range(k)]
            best = rows[0]
            rest = [x for rw in rows[1:] for x in rw]
            while rest:
                grp, rest = rest[:k], rest[k:]
                grp = grp + [neg] * (k - len(grp))
                _bitonic_sort_desc(grp)
                best = _merge_top(best, grp)
            mx = best[0]
            z = jnp.exp(best[0] - mx)
            for r in range(1, k):
                z = z + jnp.exp(best[r] - mx)
            rz = 1.0 / z
            tau = best[k - 1]
            for r1 in range(k):
                cnt = jnp.where(rows[r1][0] >= tau, 1.0, 0.0)
                for r2 in range(1, k // (r1 + 1)):
                    cnt = cnt + jnp.where(rows[r1][r2] >= tau, 1.0, 0.0)
                cr_ref[blk, r1] = cnt
            rz_ref[blk] = rz

            def finish_head(h, carry):
                hs = pl.ds(h, 1)
                s1h = s1_ref[h, :, ls]
                s2h = s2_ref[h, :, ls]
                cn = jnp.zeros_like(s1h)
                rk = jnp.full_like(s2h, float(k))
                for r in reversed(range(k)):
                    cn = jnp.where(s1h == t1_ref[blk, r, hs, :], cr_ref[blk, r, hs, :], cn)
                    rk = jnp.where(s2h >= t2_ref[blk, r, hs, :], float(r), rk)
                cnt_ref[h, :, ls] = cn
                rk2_ref[h, :, ls] = rk.astype(BF16)
                f1_ref[h, :, ls] = jnp.exp(s1h - t1_ref[blk, 0, hs, :]) * rz_ref[blk, hs, :]
                e2_ref[h, :, ls] = jnp.exp(s2h - t2_ref[blk, 0, hs, :]).astype(BF16)
                return carry

            lax.fori_loop(0, PEER_HEADS, finish_head, 0)
        acc_ref[...] = jnp.zeros_like(acc_ref)
        at_refs[1][...] = jnp.zeros_like(at_refs[1])
        w_refs[0][...] = jnp.zeros_like(w_refs[0])

    ec = n_i1 * nk
    rb_rows = nk // 2
    n_grp = n_i1 // PEER_GROUP
    g_rows = ec // n_grp
    gpm = n_grp // PEER_MM_SLICES
    d_rows = d // PEER_MM_SLICES
    e_rows = ec // PEER_MM_SLICES

    def gate_group(grp, at_oth, w_oth, cb):
        g_off = pl.multiple_of(grp * g_rows, g_rows)
        i1s = [cb * n_i1 + grp * PEER_GROUP + jj for jj in range(PEER_GROUP)]
        cnrows = [[cnt_ref[h, pl.ds(i1, 1), :].astype(BF16) for i1 in i1s] for h in range(PEER_HEADS)]
        f1rows = [[f1_ref[h, pl.ds(i1, 1), :].astype(BF16) for i1 in i1s] for h in range(PEER_HEADS)]
        for tb in range(tm // LANES):
            ts = slice(tb * LANES, (tb + 1) * LANES)
            for rb in range(nk // rb_rows):
                rs = slice(rb * rb_rows, (rb + 1) * rb_rows)
                gs = [jnp.zeros((rb_rows, LANES), BF16) for _ in range(PEER_GROUP)]
                for h in range(PEER_HEADS):
                    rk2 = rk2_ref[h, rs, ts]
                    e2b = e2_ref[h, rs, ts]
                    for jj in range(PEER_GROUP):
                        cnr = cnrows[h][jj][:, ts]
                        f1r = f1rows[h][jj][:, ts]
                        gs[jj] = gs[jj] + jnp.where(rk2 < cnr, e2b * f1r, zero_b)
                for jj in range(PEER_GROUP):
                    r0 = pl.multiple_of(g_off + jj * nk + rb * rb_rows, rb_rows)
                    a = at_oth[pl.ds(r0, rb_rows), ts]
                    act = 0.5 * a * (1.0 + _erf(a * (2.0 ** -0.5)))
                    w_oth[pl.ds(r0, rb_rows), ts] = act.astype(BF16) * gs[jj]

    for sub in range(PEER_SUBSTEPS):
        at_cur, at_oth = at_refs[sub], at_refs[1 - sub]
        w_cur, w_oth = w_refs[sub], w_refs[1 - sub]
        cb = jnp.clip(s * PEER_SUBSTEPS + sub - 1, 0, n_chunks - 1)

        def mm_slice(ms, carry, sub=sub, at_cur=at_cur, at_oth=at_oth, w_cur=w_cur, w_oth=w_oth, cb=cb):
            e_off = pl.multiple_of(ms * e_rows, e_rows)
            u_off = pl.multiple_of((sub * ec + e_off) // 2, e_rows // 2)
            u_rows = pltpu.bitcast(u_ref[pl.ds(u_off, e_rows // 2), :], BF16)
            at_cur[pl.ds(e_off, e_rows), :] = _dot(u_rows, ht_ref[...])
            d_off = pl.multiple_of(ms * d_rows, d_rows)
            v_off = pl.multiple_of(d_off // 2, d_rows // 2)
            v_rows = pltpu.bitcast(vt_ref[pl.ds(v_off, d_rows // 2), sub * ec:(sub + 1) * ec], BF16)
            acc_ref[pl.ds(d_off, d_rows), :] += _dot(v_rows, w_cur[...])
            for gi in range(gpm):
                gate_group(ms * gpm + gi, at_oth, w_oth, cb)
            return carry

        lax.fori_loop(0, PEER_MM_SLICES, mm_slice, 0)

    @pl.when(s == n_steps - 1)
    def _():
        for half in range(tm // TOK_TILE):
            g256 = t * (tm // TOK_TILE) + half
            row = jnp.where(g256 % tiles_per_batch < n_lat_tiles, g256 // tiles_per_batch, n_batch)
            g2 = ada_ref[pl.ds(row, 1), 5 * d:6 * d]
            rs = slice(half * TOK_TILE, (half + 1) * TOK_TILE)
            x2 = x_ref[rs, :] + g2 * acc_ref[:, rs].T
            if final:
                x2 = _rms(x2, fg_ref[...])
            o_ref[rs, :] = x2


def _peer(x1, h2, ada, final_g, wq, keys, u, vt, tiles_per_batch, n_lat_tiles, n_batch, final):
    rows, d = x1.shape
    tm = PEER_TILE
    nk = keys.shape[2]
    n_i1 = PEER_CHUNK_I1
    ec = n_i1 * nk
    n_blocks = 2 * u.shape[0] // (PEER_SUBSTEPS * ec)
    n_sort = tm // PEER_SORT_W
    assert PEER_HEADS == SUBLANES
    full = lambda arr: pl.BlockSpec(arr.shape, lambda t, s: (0,) * arr.ndim, pipeline_mode=pl.Buffered(1))
    tok = pl.BlockSpec((tm, d), lambda t, s: (t, 0))
    return pl.pallas_call(
        functools.partial(_peer_kernel, tiles_per_batch, n_lat_tiles, n_batch, n_i1, final),
        grid=(rows // tm, n_blocks + 1),
        in_specs=[tok, tok, full(ada), full(final_g), full(wq), full(keys),
                  pl.BlockSpec((PEER_SUBSTEPS * ec // 2, d), lambda t, s: (jnp.minimum(s, n_blocks - 1), 0)),
                  pl.BlockSpec((d // 2, PEER_SUBSTEPS * ec), lambda t, s: (0, jnp.maximum(s - 1, 0)))],
        out_specs=tok,
        out_shape=jax.ShapeDtypeStruct((rows, d), F32),
        scratch_shapes=[pltpu.VMEM((PEER_HEADS, nk, tm), F32), pltpu.VMEM((PEER_HEADS, nk, tm), F32),
                        pltpu.VMEM((PEER_HEADS, nk, tm), F32), pltpu.VMEM((PEER_HEADS, nk, tm), F32),
                        pltpu.VMEM((PEER_HEADS, nk, tm), BF16), pltpu.VMEM((PEER_HEADS, nk, tm), BF16),
                        pltpu.VMEM((n_sort, PEER_TOPK, PEER_HEADS, PEER_SORT_W), F32),
                        pltpu.VMEM((n_sort, PEER_TOPK, PEER_HEADS, PEER_SORT_W), F32),
                        pltpu.VMEM((n_sort, PEER_TOPK, PEER_HEADS, PEER_SORT_W), F32),
                        pltpu.VMEM((n_sort, PEER_HEADS, PEER_SORT_W), F32),
                        pltpu.VMEM((ec, tm), F32), pltpu.VMEM((ec, tm), F32),
                        pltpu.VMEM((ec, tm), BF16), pltpu.VMEM((ec, tm), BF16),
                        pltpu.VMEM((d, tm), F32), pltpu.VMEM((d, tm), BF16)],
        compiler_params=_cparams(("parallel", "arbitrary")),
        name="peer",
    )(x1, h2, ada, final_g, wq, keys, u, vt)


def _rope_tables(seq, t):
    quarter = DA_QK // 4
    inv = ROPE_BASE ** (-np.arange(quarter, dtype=np.float32) / quarter)
    pos = np.arange(seq)
    ang_r = (pos // GRID_W).astype(np.float32)[:, None] * inv[None, :]
    ang_c = (pos % GRID_W).astype(np.float32)[:, None] * inv[None, :]
    half = np.concatenate([ang_r, ang_c, ang_r, ang_c], axis=1)
    cos = np.ones((t, LANES), np.float32)
    sin = np.zeros((t, LANES), np.float32)
    cos[:seq] = np.concatenate([np.cos(half), np.cos(half)], axis=1)
    sin[:seq] = np.concatenate([-np.sin(half), np.sin(half)], axis=1)
    return jnp.asarray(cos), jnp.asarray(sin)


def _qk_perm():
    quarter = DA_QK // 4
    first, second = [], []
    for m in range(2):
        for part in range(2):
            base = m * DA_QK + part * (DA_QK // 2)
            first += [base + j for j in range(quarter)]
            second += [base + quarter + j for j in range(quarter)]
    return np.array(first + second)


def _pack_kernel(transpose, x_ref, o_ref):
    x = x_ref[0]
    if transpose:
        x = x.T
    o_ref[0] = pltpu.bitcast(x.astype(BF16), jnp.uint32)


def _pack_table(tab, transpose):
    depth, r, c = tab.shape
    rb = PACK_ROWS
    if transpose:
        in_spec = pl.BlockSpec((1, rb, c), lambda l, j: (l, j, 0))
        out_spec = pl.BlockSpec((1, c // 2, rb), lambda l, j: (l, 0, j))
        out_shape = (depth, c // 2, r)
    else:
        in_spec = pl.BlockSpec((1, rb, c), lambda l, j: (l, j, 0))
        out_spec = pl.BlockSpec((1, rb // 2, c), lambda l, j: (l, j, 0))
        out_shape = (depth, r // 2, c)
    return pl.pallas_call(
        functools.partial(_pack_kernel, transpose),
        grid=(depth, r // rb),
        in_specs=[in_spec],
        out_specs=out_spec,
        out_shape=jax.ShapeDtypeStruct(out_shape, jnp.uint32),
        compiler_params=_cparams(("parallel", "parallel")),
        name="pack_t" if transpose else "pack",
    )(tab)


def kernel(x, c, ctx, c_ctx, w_ada, b_ada, norm1_g, norm2_g, w_in, ml_gate_b, ml_conv_w, ml_norm_g, cv_conv_w,
           da_lam, da_subln_g, w_da, w_ml, w_cv, w_o, peer_wq, peer_keys, peer_u, peer_v, final_g):
    bsz, seq, d = x.shape
    n_ctx = ctx.shape[1]
    t = seq + n_ctx
    depth = w_ada.shape[0]
    tm = TOK_TILE
    assert seq % tm == 0 and n_ctx % tm == 0 and seq % ML_CHUNK == 0 and n_ctx % ML_CHUNK == 0
    assert (bsz * t) % PEER_TILE == 0 and (bsz * seq) % PEER_TILE == 0
    n_lat_tiles = seq // tm
    n_tiles = t // tm

    rows = -(-(bsz + 1) // SUBLANES) * SUBLANES
    cc = jnp.zeros((rows, d), F32).at[:bsz].set(c).at[bsz].set(c_ctx)
    ada = _ada(cc, w_ada, b_ada)

    o = np.cumsum([0, DA_WIDTH, DA_WIDTH, DA_WIDTH, 2 * ML_WIDTH, ML_WIDTH, ML_WIDTH, 4 * ML_HEADS,
                   CV_WIDTH, CV_WIDTH, CV_WIDTH, 3 * d])
    perm = _qk_perm()
    qk_cols = np.concatenate([o[g] + h * DA_V + perm for g in range(2) for h in range(DA_HEADS)])
    da_cols = np.concatenate([qk_cols, np.arange(o[2], o[3])])
    wda_in = w_in[:, :, da_cols].astype(BF16)
    wml_in = w_in[:, :, o[3]:o[6]].astype(BF16)
    wmg_in = jnp.pad(w_in[:, :, o[6]:o[7]], ((0, 0), (0, 0), (0, LANES - 4 * ML_HEADS))).astype(BF16)
    wcv_in = w_in[:, :, o[7]:o[10]].astype(BF16)
    wgt_in = w_in[:, :, o[10]:o[11]].astype(BF16)
    gate_b = jnp.pad(ml_gate_b.reshape(depth, 1, 4 * ML_HEADS), ((0, 0), (0, 0), (0, LANES - 4 * ML_HEADS)))
    lam_p = da_lam
    w_da_b, w_ml_b, w_cv_b, w_o_b = (a.astype(BF16) for a in (w_da, w_ml, w_cv, w_o))
    wq_b = peer_wq.astype(BF16)
    keys_b = peer_keys.astype(BF16)
    u_b = _pack_table(peer_u, transpose=False)
    vt_b = _pack_table(peer_v, transpose=True)
    cos, sin = _rope_tables(seq, t)

    xs = jnp.concatenate([x, ctx], axis=1)
    for l in range(depth):
        need_ctx = l < depth - 1
        lam_init = 0.8 - 0.6 * math.exp(-0.3 * l)
        da, ml, mg, cv, gt = _inproj(xs, ada[l], norm1_g[l].reshape(1, d), cos, sin,
                                     wda_in[l], wml_in[l], wmg_in[l], wcv_in[l], wgt_in[l], n_lat_tiles)
        yda = _attention(da, lam_p[l], da_subln_g[l], n_lat_tiles, seq, lam_init)
        yml = _mlstm(ml, mg, ml_conv_w[l], gate_b[l], ml_norm_g[l], seq // ML_CHUNK, n_ctx // ML_CHUNK)
        n_out_tiles = n_tiles if need_ctx else n_lat_tiles
        x1, h2 = _merge(xs, ada[l], norm2_g[l].reshape(1, d), yda, yml, cv, gt, cv_conv_w[l],
                        w_da_b[l], w_ml_b[l], w_cv_b[l], w_o_b[l], n_lat_tiles, n_out_tiles)
        t_out = n_out_tiles * tm
        xs = _peer(x1.reshape(bsz * t_out, d), h2.reshape(bsz * t_out, d), ada[l], final_g.reshape(1, d),
                   wq_b[l], keys_b[l], u_b[l], vt_b[l], n_out_tiles, n_lat_tiles, bsz, final=not need_ctx)
        xs = xs.reshape(bsz, t_out, d)
    return xs
```

```python
import functools
import math

import numpy as np
import jax
import jax.numpy as jnp
from jax import lax
from jax.experimental import pallas as pl
from jax.experimental.pallas import tpu as pltpu

EPS = 1e-6
GRID_W = 64
DA_HEADS = 4
DA_QK = 64
DA_V = 2 * DA_QK
DA_WIDTH = DA_HEADS * DA_V
ROPE_BASE = 10000.0
ML_HEADS = 4
ML_DH = 128
ML_WIDTH = ML_HEADS * ML_DH
ML_CHUNK = 128
CV_WIDTH = 512
PEER_HEADS = 8
PEER_TOPK = 16
PEER_GROUP = 2

LANES = 128
SUBLANES = 8
TOK_TILE = 256
HALO = 16
PEER_TILE = 512
PEER_CHUNK_I1 = 8
PEER_SUBSTEPS = 2
PACK_ROWS = 512
PEER_MM_SLICES = 2
PEER_SORT_W = 2 * LANES
VMEM_LIMIT = 56 * 1024 * 1024

NEG_BIG = -1e30
BF16 = jnp.bfloat16
F32 = jnp.float32
HIGHEST = lax.Precision.HIGHEST


def _cparams(sem):
    return pltpu.CompilerParams(dimension_semantics=sem, vmem_limit_bytes=VMEM_LIMIT)


def _dot(a, b):
    return jnp.dot(a, b, preferred_element_type=F32)


def _dot_nt(a, b):
    return lax.dot_general(a, b, (((1,), (1,)), ((), ())), preferred_element_type=F32)


def _dot_tn(a, b):
    return lax.dot_general(a, b, (((0,), (0,)), ((), ())), preferred_element_type=F32)


def _rms(x, g):
    return x * lax.rsqrt(jnp.mean(x * x, axis=-1, keepdims=True) + EPS) * g


def _sigmoid(x):
    return 1.0 / (1.0 + jnp.exp(-x))


def _log_sigmoid(x):
    return jnp.minimum(x, 0.0) - jnp.log1p(jnp.exp(-jnp.abs(x)))


def _ada_kernel(c_ref, w_ref, b_ref, o_ref):
    c = c_ref[...]
    s = c * _sigmoid(c)
    o_ref[0] = jnp.dot(s, w_ref[0], preferred_element_type=F32, precision=HIGHEST) + b_ref[0]


def _ada(cc, w_ada, b_ada):
    depth, d, d6 = w_ada.shape
    rows = cc.shape[0]
    nb = d6 // d
    return pl.pallas_call(
        _ada_kernel,
        grid=(depth, nb),
        in_specs=[pl.BlockSpec((rows, d), lambda l, j: (0, 0)),
                  pl.BlockSpec((1, d, d), lambda l, j: (l, 0, j)),
                  pl.BlockSpec((1, 1, d), lambda l, j: (l, 0, j))],
        out_specs=pl.BlockSpec((1, rows, d), lambda l, j: (l, 0, j)),
        out_shape=jax.ShapeDtypeStruct((depth, rows, d6), F32),
        compiler_params=_cparams(("parallel", "parallel")),
        name="ada",
    )(cc, w_ada, b_ada.reshape(depth, 1, d6))


def _inproj_kernel(n_lat_tiles, n_batch, x_ref, ada_ref, g_ref, cos_ref, sin_ref,
                   wda_ref, wml_ref, wmg_ref, wcv_ref, wgt_ref,
                   da_ref, ml_ref, mg_ref, cv_ref, gt_ref):
    b = pl.program_id(0)
    i = pl.program_id(1)
    d = x_ref.shape[-1]
    row = jnp.where(i < n_lat_tiles, b, n_batch)
    shift = ada_ref[pl.ds(row, 1), 0:d]
    scale = ada_ref[pl.ds(row, 1), d:2 * d]
    h = _rms(x_ref[0], g_ref[...]) * (1.0 + scale) + shift
    hb = h.astype(BF16)

    zda = _dot(hb, wda_ref[...])
    cos = cos_ref[...]
    sin = sin_ref[...]
    for j in range(2 * DA_HEADS):
        blk = zda[:, j * LANES:(j + 1) * LANES]
        rot = blk * cos + pltpu.roll(blk, LANES // 2, 1) * sin
        da_ref[0, :, j * LANES:(j + 1) * LANES] = rot.astype(BF16)
    da_ref[0, :, 2 * DA_WIDTH:] = zda[:, 2 * DA_WIDTH:].astype(BF16)

    ml_ref[0] = _dot(hb, wml_ref[...]).astype(BF16)
    mg_ref[0] = _dot(hb, wmg_ref[...])

    zcv = _dot(hb, wcv_ref[...])
    cv_ref[0, :, 0:CV_WIDTH] = zcv[:, 0:CV_WIDTH].astype(BF16)
    cv_ref[0, :, CV_WIDTH:] = (zcv[:, CV_WIDTH:2 * CV_WIDTH] * zcv[:, 2 * CV_WIDTH:]).astype(BF16)

    for j in range(3):
        zg = _dot(hb, wgt_ref[:, j * d:(j + 1) * d])
        gt_ref[0, :, j * d:(j + 1) * d] = _sigmoid(zg).astype(BF16)


def _inproj(x, ada, g, cos, sin, wda, wml, wmg, wcv, wgt, n_lat_tiles):
    bsz, t, d = x.shape
    tm = TOK_TILE
    full = lambda arr: pl.BlockSpec(arr.shape, lambda b, i: (0,) * arr.ndim, pipeline_mode=pl.Buffered(1))
    tok = lambda w: pl.BlockSpec((1, tm, w), lambda b, i: (b, i, 0))
    widths =(wda.shape[1], wml.shape[1], wmg.shape[1], 2 * CV_WIDTH, wgt.shape[1])
    dts = (BF16, BF16, F32, BF16, BF16)
    return pl.pallas_call(
        functools.partial(_inproj_kernel, n_lat_tiles, bsz),
        grid=(bsz, t // tm),
        in_specs=[tok(d), full(ada), full(g),
                  pl.BlockSpec((tm, LANES), lambda b, i: (i, 0)),
                  pl.BlockSpec((tm, LANES), lambda b, i: (i, 0)),
                  full(wda), full(wml), full(wmg), full(wcv), full(wgt)],
        out_specs=[tok(w) for w in widths],
        out_shape=[jax.ShapeDtypeStruct((bsz, t, w), dt) for w, dt in zip(widths, dts)],
        compiler_params=_cparams(("parallel", "parallel")),
        name="inproj",
    )(x, ada, g, cos, sin, wda, wml, wmg, wcv, wgt)


def _attn_kernel(n_lat_tiles, seq, lam_init, q_ref, k_ref, v_ref, lam_ref, g_ref, o_ref, va_ref):
    i = pl.program_id(2)
    lv = lam_ref[...]
    lam = (jnp.exp(jnp.sum(lv[0:1] * lv[1:2], axis=-1, keepdims=True))
           - jnp.exp(jnp.sum(lv[2:3] * lv[3:4], axis=-1, keepdims=True)) + lam_init)
    lane = lax.broadcasted_iota(jnp.int32, (1, LANES), 1)
    map0 = (lane % (LANES // 2)) < (DA_QK // 2)

    @pl.when(i == 0)
    def _():
        va_ref[:, 0:LANES] = v_ref[0]
        va_ref[:, LANES:] = jnp.ones((va_ref.shape[0], LANES), BF16)

    def attend(k, va):
        q = q_ref[0].astype(F32) * (DA_QK ** -0.5)
        q0 = jnp.where(map0, q, 0.0).astype(BF16)
        q1 = jnp.where(map0, 0.0, q).astype(BF16)
        s0 = _dot_nt(q0, k)
        s1 = _dot_nt(q1, k)
        e0 = jnp.exp((s0 - jnp.max(s0, axis=-1, keepdims=True)).astype(BF16))
        e1 = jnp.exp((s1 - jnp.max(s1, axis=-1, keepdims=True)).astype(BF16))
        o0 = _dot(e0, va)
        o1 = _dot(e1, va)
        o = o0[:, 0:LANES] / o0[:, LANES:] - lam * (o1[:, 0:LANES] / o1[:, LANES:])
        o_ref[0] = (_rms(o, g_ref[...]) * (1.0 - lam_init)).astype(o_ref.dtype)

    @pl.when(i < n_lat_tiles)
    def _():
        attend(k_ref[0], va_ref[...])

    @pl.when(i >= n_lat_tiles)
    def _():
        attend(k_ref[0, seq:, :], va_ref[seq:, :])


def _attention(da, lam_p, subln_g, n_lat_tiles, seq, lam_init):
    bsz, t, _ = da.shape
    tq = TOK_TILE
    return pl.pallas_call(
        functools.partial(_attn_kernel, n_lat_tiles, seq, lam_init),
        grid=(bsz, DA_HEADS, t // tq),
        in_specs=[pl.BlockSpec((1, tq, LANES), lambda b, h, i: (b, i, h)),
                  pl.BlockSpec((1, t, LANES), lambda b, h, i: (b, 0, DA_HEADS + h)),
                  pl.BlockSpec((1, t, LANES), lambda b, h, i: (b, 0, 2 * DA_HEADS + h)),
                  pl.BlockSpec(lam_p.shape, lambda b, h, i: (0, 0)),
                  pl.BlockSpec((1, LANES), lambda b, h, i: (0, 0))],
        out_specs=pl.BlockSpec((1, tq, LANES), lambda b, h, i: (b, i, h)),
        out_shape=jax.ShapeDtypeStruct((bsz, t, DA_WIDTH), BF16),
        scratch_shapes=[pltpu.VMEM((t, 2 * LANES), BF16)],
        compiler_params=_cparams(("parallel", "parallel", "arbitrary")),
        name="diffattn",
    )(da, da, da, lam_p, subln_g.reshape(1, LANES))


def _mlstm_kernel(ncl, ncc, ml_ref, mg_ref, cw_ref, gb_ref, ng_ref, o_ref,
                  qs_ref, ks_ref, hf_ref, hb_ref, c_ref, n_ref, m_ref):
    L = ML_CHUNK
    nc = ncl + ncc
    t_all = nc * L
    seq = ncl * L
    w = ML_WIDTH

    trow = lax.broadcasted_iota(jnp.int32, (t_all, 1), 0)
    first = (trow == 0) | (trow == seq)
    last = (trow == seq - 1) | (trow == t_all - 1)
    for j in range(2 * ML_HEADS):
        x = ml_ref[0, :, j * LANES:(j + 1) * LANES].astype(F32)
        xm = jnp.where(first, 0.0, pltpu.roll(x, 1, 0))
        xp = jnp.where(last, 0.0, pltpu.roll(x, t_all - 1, 0))
        cw = cw_ref[:, j * LANES:(j + 1) * LANES]
        y = xm * cw[0:1] + x * cw[1:2] + xp * cw[2:3]
        y = y * _sigmoid(y)
        if j < ML_HEADS:
            qs_ref[:, j * LANES:(j + 1) * LANES] = y.astype(BF16)
        else:
            jj = j - ML_HEADS
            ks_ref[:, jj * LANES:(jj + 1) * LANES] = (y * (ML_DH ** -0.5)).astype(BF16)

    c_ref[...] = jnp.zeros_like(c_ref)
    n_ref[...] = jnp.zeros_like(n_ref)
    m_ref[...] = jnp.zeros_like(m_ref)

    r_i = lax.broadcasted_iota(jnp.int32, (L, L), 0)
    c_i = lax.broadcasted_iota(jnp.int32, (L, L), 1)
    lower = r_i >= c_i
    upper = r_i <= c_i
    lower_f = lower.astype(F32)
    upper_f = upper.astype(F32)

    def step(it, carry):
        for dr in range(2):
            if dr == 0:
                ch = (it + ncl) % nc
                seen, tri_col, tri_row, end = lower, lower_f, upper_f, L - 1
                h_ref = hf_ref
            else:
                ch = nc - 1 - it
                seen, tri_col, tri_row, end = upper, upper_f, lower_f, 0
                h_ref = hb_ref
            r0 = pl.multiple_of(ch * L, L)
            g = mg_ref[0, pl.ds(r0, L), :] + gb_ref[...]
            gt = g.T
            bcol_all = jnp.dot(tri_col, _log_sigmoid(g), preferred_element_type=F32, precision=HIGHEST)
            brow_all = jnp.dot(_log_sigmoid(gt), tri_row, preferred_element_type=F32, precision=HIGHEST)
            for h in range(ML_HEADS):
                idx = dr * ML_HEADS + h
                icol = (2 * dr) * ML_HEADS + h
                fcol = (2 * dr + 1) * ML_HEADS + h
                q = qs_ref[pl.ds(r0, L), h * LANES:(h + 1) * LANES]
                k = ks_ref[pl.ds(r0, L), h * LANES:(h + 1) * LANES]
                v = ml_ref[0, pl.ds(r0, L), 2 * w + h * LANES:2 * w + (h + 1) * LANES]
                b_col = bcol_all[:, fcol:fcol + 1]
                ig_col = g[:, icol:icol + 1]
                b_row = brow_all[fcol:fcol + 1, :]
                ig_row = gt[icol:icol + 1, :]
                m_prev = m_ref[idx][:, 0:1]
                cmat = c_ref[idx]
                nrow = n_ref[idx]

                log_d = jnp.where(seen, b_col - b_row + ig_row, NEG_BIG)
                m_path = b_col + m_prev
                m_t = jnp.maximum(m_path, jnp.max(log_d, axis=-1, keepdims=True))
                dmat = jnp.exp(log_d - m_t)
                carry_w = jnp.exp(m_path - m_t)
                wmat = _dot_nt(q, k) * dmat
                num = _dot(wmat.astype(BF16), v) + carry_w * _dot(q, cmat.astype(BF16))
                den = (jnp.sum(wmat, axis=-1, keepdims=True)
                       + carry_w * jnp.sum(q.astype(F32) * nrow, axis=-1, keepdims=True))
                hval = num / jnp.maximum(jnp.abs(den), jnp.exp(-m_t))
                h_ref[pl.ds(r0, L), h * LANES:(h + 1) * LANES] = hval

                b_end = b_col[end:end + 1, :]
                log_w = b_end - b_col + ig_col
                m_new = jnp.maximum(b_end + m_prev, jnp.max(log_w, axis=0, keepdims=True))
                wk = jnp.exp(log_w - m_new)
                decay = jnp.exp(b_end + m_prev - m_new)
                kw = k.astype(F32) * wk
                c_ref[idx] = decay * cmat + _dot_tn(kw.astype(BF16), v)
                n_ref[idx] = decay * nrow + jnp.sum(kw, axis=0, keepdims=True)
                m_ref[idx] = jnp.broadcast_to(m_new, (1, LANES))
        return carry

    lax.fori_loop(0, nc, step, 0)

    for h in range(ML_HEADS):
        sl = slice(h * LANES, (h + 1) * LANES)
        hs = hf_ref[:, sl] + hb_ref[:, sl]
        y = _rms(hs, ng_ref[:, sl])
        og = ml_ref[0, :, 3 * w + h * LANES:3 * w + (h + 1) * LANES].astype(F32)
        o_ref[0, :, sl] = (y * _sigmoid(og)).astype(o_ref.dtype)


def _mlstm(ml, mg, conv_w, gate_b_row, norm_g, ncl, ncc):
    bsz, t, wtot = ml.shape
    nstate = 2 * ML_HEADS
    return pl.pallas_call(
        functools.partial(_mlstm_kernel, ncl, ncc),
        grid=(bsz,),
        in_specs=[pl.BlockSpec((1, t, wtot), lambda b: (b, 0, 0)),
                  pl.BlockSpec((1, t, LANES), lambda b: (b, 0, 0)),
                  pl.BlockSpec(conv_w.shape, lambda b: (0, 0)),
                  pl.BlockSpec((1, LANES), lambda b: (0, 0)),
                  pl.BlockSpec((1, ML_WIDTH), lambda b: (0, 0))],
        out_specs=pl.BlockSpec((1, t, ML_WIDTH), lambda b: (b, 0, 0)),
        out_shape=jax.ShapeDtypeStruct((bsz, t, ML_WIDTH), BF16),
        scratch_shapes=[pltpu.VMEM((t, ML_WIDTH), BF16), pltpu.VMEM((t, ML_WIDTH), BF16),
                        pltpu.VMEM((t, ML_WIDTH), F32), pltpu.VMEM((t, ML_WIDTH), F32),
                        pltpu.VMEM((nstate, ML_DH, ML_DH), F32),
                        pltpu.VMEM((nstate, 1, ML_DH), F32),
                        pltpu.VMEM((nstate, 1, LANES), F32)],
        compiler_params=_cparams(("parallel",)),
        name="mlstm",
    )(ml, mg, conv_w, gate_b_row, norm_g.reshape(1, ML_WIDTH))


def _merge_kernel(n_lat_tiles, n_batch, x_ref, ada_ref, g2_ref, yda_ref, yml_ref, cv_ref, pprev_ref, pnext_ref,
                  gt_ref, cw_ref, wda_ref, wml_ref, wcv_ref, wo_ref, x1_ref, h2_ref):
    b = pl.program_id(0)
    i = pl.program_id(1)
    n_tiles = pl.num_programs(1)
    d = x_ref.shape[-1]
    tm = x_ref.shape[1]
    row = jnp.where(i < n_lat_tiles, b, n_batch)

    bg = cv_ref[0, :, 0:CV_WIDTH].astype(F32)
    p = cv_ref[0, :, CV_WIDTH:].astype(F32)
    seq_start = (i == 0) | (i == n_lat_tiles)
    seq_end = (i == n_lat_tiles - 1) | (i == n_tiles - 1)
    prev_row = jnp.where(seq_start, 0.0, pprev_ref[0, HALO - 1:HALO, :].astype(F32))
    next_row = jnp.where(seq_end, 0.0, pnext_ref[0, 0:1, :].astype(F32))
    trow = lax.broadcasted_iota(jnp.int32, (tm, 1), 0)
    pm = jnp.where(trow == 0, prev_row, pltpu.roll(p, 1, 0))
    pp = jnp.where(trow == tm - 1, next_row, pltpu.roll(p, tm - 1, 0))
    cw = cw_ref[...]
    ycv = bg * (pm * cw[0:1] + p * cw[1:2] + pp * cw[2:3])

    m = (gt_ref[0, :, 0:d].astype(F32) * _dot(yda_ref[0], wda_ref[...])
         + gt_ref[0, :, d:2 * d].astype(F32) * _dot(yml_ref[0], wml_ref[...])
         + gt_ref[0, :, 2 * d:].astype(F32) * _dot(ycv.astype(BF16), wcv_ref[...]))
    out = _dot(m.astype(BF16), wo_ref[...])
    g1 = ada_ref[pl.ds(row, 1), 2 * d:3 * d]
    x1 = x_ref[0] + g1 * out
    x1_ref[0] = x1
    sh2 = ada_ref[pl.ds(row, 1), 3 * d:4 * d]
    sc2 = ada_ref[pl.ds(row, 1), 4 * d:5 * d]
    h2_ref[0] = (_rms(x1, g2_ref[...]) * (1.0 + sc2) + sh2).astype(BF16)


def _merge(x, ada, g2, yda, yml, cv, gt, cv_w, wda, wml, wcv, wo, n_lat_tiles, n_tiles):
    bsz, t, d = x.shape
    tm = TOK_TILE
    r = tm // HALO
    nrb = t // HALO
    full = lambda arr: pl.BlockSpec(arr.shape, lambda b, i: (0,) * arr.ndim, pipeline_mode=pl.Buffered(1))
    tok = lambda w: pl.BlockSpec((1, tm, w), lambda b, i: (b, i, 0))
    return pl.pallas_call(
        functools.partial(_merge_kernel, n_lat_tiles, bsz),
        grid=(bsz, n_tiles),
        in_specs=[tok(d), full(ada), full(g2), tok(DA_WIDTH), tok(ML_WIDTH), tok(2 * CV_WIDTH),
                  pl.BlockSpec((1, HALO, CV_WIDTH), lambda b, i: (b, jnp.maximum(i * r - 1, 0), 1)),
                  pl.BlockSpec((1, HALO, CV_WIDTH), lambda b, i: (b, jnp.minimum((i + 1) * r, nrb - 1), 1)),
                  tok(3 * d), full(cv_w), full(wda), full(wml), full(wcv), full(wo)],
        out_specs=[tok(d), tok(d)],
        out_shape=[jax.ShapeDtypeStruct((bsz, n_tiles * tm, d), F32),
                   jax.ShapeDtypeStruct((bsz, n_tiles * tm, d), BF16)],
        compiler_params=_cparams(("parallel", "parallel")),
        name="merge",
    )(x, ada, g2, yda, yml, cv, cv, cv, gt, cv_w, wda, wml, wcv, wo)


def _cmpx(v, i, j):
    a, b = v[i], v[j]
    v[i] = jnp.maximum(a, b)
    v[j] = jnp.minimum(a, b)


def _bitonic_merge_desc(v):
    n = len(v)
    j = n // 2
    while j >= 1:
        for i in range(n):
            l = i ^ j
            if l > i:
                _cmpx(v, i, l)
        j //= 2


def _bitonic_sort_desc(v):
    n = len(v)
    k = 2
    while k <= n:
        j = k // 2
        while j >= 1:
            for i in range(n):
                l = i ^ j
                if l > i:
                    if (i & k) == 0:
                        _cmpx(v, i, l)
                    else:
                        _cmpx(v, l, i)
            j //= 2
        k *= 2


def _merge_top(a, b):
    n = len(a)
    c = [jnp.maximum(a[r], b[n - 1 - r]) for r in range(n)]
    _bitonic_merge_desc(c)
    return c


def _top16_sorted(s):
    nk, tm = s.shape
    k = PEER_TOPK
    groups = nk // k
    assert groups == SUBLANES
    a = s.reshape(k, SUBLANES, tm)
    v = [a[r] for r in range(k)]
    _bitonic_sort_desc(v)
    sh = SUBLANES // 2
    while sh >= 1:
        v = _merge_top(v, [pltpu.roll(x, sh, 0) for x in v])
        sh //= 2
    return v


def _erf(x):
    return lax.erf(x)


def _peer_kernel(tiles_per_batch, n_lat_tiles, n_batch, n_i1, final, x_ref, h_ref, ada_ref, fg_ref, wq_ref, keys_ref,
                 u_ref, vt_ref, o_ref, s1_ref, s2_ref, f1_ref, cnt_ref, e2_ref, rk2_ref, t1_ref, t2_ref, cr_ref, rz_ref,
                 at0_ref, at1_ref, w0_ref, w1_ref, acc_ref, ht_ref):
    t = pl.program_id(0)
    s = pl.program_id(1)
    n_steps = pl.num_programs(1)
    n_chunks = (n_steps - 1) * PEER_SUBSTEPS
    d = x_ref.shape[-1]
    tm = x_ref.shape[0]
    nk = keys_ref.shape[2]
    dsub = keys_ref.shape[3]
    k = PEER_TOPK
    sort_w = PEER_SORT_W
    at_refs = (at0_ref, at1_ref)
    w_refs = (w0_ref, w1_ref)
    zero_b = jnp.zeros((), BF16)

    @pl.when(s == 0)
    def _():
        hb = h_ref[...]
        for half in range(tm // TOK_TILE):
            cs = slice(half * TOK_TILE, (half + 1) * TOK_TILE)
            ht_ref[:, cs] = hb[cs, :].astype(F32).T.astype(BF16)
        q = _dot(hb, wq_ref[...]).astype(BF16)
        for h in range(PEER_HEADS):
            for p in range(2):
                col = (h * 2 + p) * dsub
                (s1_ref if p == 0 else s2_ref)[h] = _dot_nt(keys_ref[h, p], q[:, col:col + dsub])
        neg = jnp.full((SUBLANES, sort_w), -jnp.inf, F32)
        for blk in range(tm // sort_w):
            ls = slice(blk * sort_w, (blk + 1) * sort_w)

            def sort_head(h, carry):
                for sref, tref in ((s1_ref, t1_ref), (s2_ref, t2_ref)):
                    v = _top16_sorted(sref[h, :, ls])
                    for r in range(k):
                        tref[blk, r, pl.ds(h, 1), :] = v[r][0:1, :]
                return carry

            lax.fori_loop(0, PEER_HEADS, sort_head, 0)
            t1 = [t1_ref[blk, r] for r in range(k)]
            t2 = [t2_ref[blk, r] for r in range(k)]
            rows = [[t1[r1] + t2[r2] for r2 in range(k // (r1 + 1))] for r1 in range(k)]
            best = rows[0]
            rest = [x for rw in rows[1:] for x in rw]
            while rest:
                grp, rest = rest[:k], rest[k:]
                grp = grp + [neg] * (k - len(grp))
                _bitonic_sort_desc(grp)
                best = _merge_top(best, grp)
            mx = best[0]
            z = jnp.exp(best[0] - mx)
            for r in range(1, k):
                z = z + jnp.exp(best[r] - mx)
            rz = 1.0 / z
            tau = best[k - 1]
            for r1 in range(k):
                cnt = jnp.where(rows[r1][0] >= tau, 1.0, 0.0)
                for r2 in range(1, k // (r1 + 1)):
                    cnt = cnt + jnp.where(rows[r1][r2] >= tau, 1.0, 0.0)
                cr_ref[blk, r1] = cnt
            rz_ref[blk] = rz

            def finish_head(h, carry):
                hs = pl.ds(h, 1)
                s1h = s1_ref[h, :, ls]
                s2h = s2_ref[h, :, ls]
                cn = jnp.zeros_like(s1h)
                rk = jnp.full_like(s2h, float(k))
                for r in reversed(range(k)):
                    cn = jnp.where(s1h == t1_ref[blk, r, hs, :], cr_ref[blk, r, hs, :], cn)
                    rk = jnp.where(s2h >= t2_ref[blk, r, hs, :], float(r), rk)
                cnt_ref[h, :, ls] = cn
                rk2_ref[h, :, ls] = rk.astype(BF16)
                f1_ref[h, :, ls] = jnp.exp(s1h - t1_ref[blk, 0, hs, :]) * rz_ref[blk, hs, :]
                e2_ref[h, :, ls] = jnp.exp(s2h - t2_ref[blk, 0, hs, :]).astype(BF16)
                return carry

            lax.fori_loop(0, PEER_HEADS, finish_head, 0)
        acc_ref[...] = jnp.zeros_like(acc_ref)

    ec = n_i1 * nk
    rb_rows = nk // 2
    n_grp = n_i1 // PEER_GROUP
    g_rows = ec // n_grp
    gpm = n_grp // PEER_MM_SLICES
    d_rows = d // PEER_MM_SLICES
    e_rows = ec // PEER_MM_SLICES

    def gate_group(grp, at_oth, w_oth, cb):
        g_off = pl.multiple_of(grp * g_rows, g_rows)
        i1s = [cb * n_i1 + grp * PEER_GROUP + jj for jj in range(PEER_GROUP)]
        cnrows = [[cnt_ref[h, pl.ds(i1, 1), :].astype(BF16) for i1 in i1s] for h in range(PEER_HEADS)]
        f1rows = [[f1_ref[h, pl.ds(i1, 1), :].astype(BF16) for i1 in i1s] for h in range(PEER_HEADS)]
        for tb in range(tm // LANES):
            ts = slice(tb * LANES, (tb + 1) * LANES)
            for rb in range(nk // rb_rows):
                rs = slice(rb * rb_rows, (rb + 1) * rb_rows)
                gs = [jnp.zeros((rb_rows, LANES), BF16) for _ in range(PEER_GROUP)]
                for h in range(PEER_HEADS):
                    rk2 = rk2_ref[h, rs, ts]
                    e2b = e2_ref[h, rs, ts]
                    for jj in range(PEER_GROUP):
                        cnr = cnrows[h][jj][:, ts]
                        f1r = f1rows[h][jj][:, ts]
                        gs[jj] = gs[jj] + jnp.where(rk2 < cnr, e2b * f1r, zero_b)
                for jj in range(PEER_GROUP):
                    r0 = pl.multiple_of(g_off + jj * nk + rb * rb_rows, rb_rows)
                    a = at_oth[pl.ds(r0, rb_rows), ts]
                    act = 0.5 * a * (1.0 + _erf(a * (2.0 ** -0.5)))
                    w_oth[pl.ds(r0, rb_rows), ts] = act.astype(BF16) * gs[jj]

    def substeps(do1, do2, do3):
        for sub in range(PEER_SUBSTEPS):
            at_cur, at_oth = at_refs[sub], at_refs[1 - sub]
            w_cur, w_oth = w_refs[sub], w_refs[1 - sub]
            cb = s * PEER_SUBSTEPS + sub - 1

            def mm_slice(ms, carry, sub=sub, at_cur=at_cur, at_oth=at_oth, w_cur=w_cur, w_oth=w_oth, cb=cb):
                if do1[sub]:
                    e_off = pl.multiple_of(ms * e_rows, e_rows)
                    u_off = pl.multiple_of((sub * ec + e_off) // 2, e_rows // 2)
                    u_rows = pltpu.bitcast(u_ref[pl.ds(u_off, e_rows // 2), :], BF16)
                    at_cur[pl.ds(e_off, e_rows), :] = _dot(u_rows, ht_ref[...])
                if do3[sub]:
                    d_off = pl.multiple_of(ms * d_rows, d_rows)
                    v_off = pl.multiple_of(d_off // 2, d_rows // 2)
                    v_rows = pltpu.bitcast(vt_ref[pl.ds(v_off, d_rows // 2), sub * ec:(sub + 1) * ec], BF16)
                    acc_ref[pl.ds(d_off, d_rows), :] += _dot(v_rows, w_cur[...])
                if do2[sub]:
                    for gi in range(gpm):
                        gate_group(ms * gpm + gi, at_oth, w_oth, cb)
                return carry

            lax.fori_loop(0, PEER_MM_SLICES, mm_slice, 0)

    first = s == 0
    last = s == n_steps - 1

    @pl.when(first)
    def _():
        substeps(do1=(True, True), do2=(False, True), do3=(False, False))

    @pl.when(jnp.logical_not(first | last))
    def _():
        substeps(do1=(True, True), do2=(True, True), do3=(True, True))

    @pl.when(last)
    def _():
        substeps(do1=(False, False), do2=(True, False), do3=(True, True))

    @pl.when(s == n_steps - 1)
    def _():
        for half in range(tm // TOK_TILE):
            g256 = t * (tm // TOK_TILE) + half
            row = jnp.where(g256 % tiles_per_batch < n_lat_tiles, g256 // tiles_per_batch, n_batch)
            g2 = ada_ref[pl.ds(row, 1), 5 * d:6 * d]
            rs = slice(half * TOK_TILE, (half + 1) * TOK_TILE)
            x2 = x_ref[rs, :] + g2 * acc_ref[:, rs].T
            if final:
                x2 = _rms(x2, fg_ref[...])
            o_ref[rs, :] = x2


def _peer(x1, h2, ada, final_g, wq, keys, u, vt, layer, tiles_per_batch, n_lat_tiles, n_batch, final):
    rows, d = x1.shape
    tm = PEER_TILE
    nk = keys.shape[2]
    n_i1 = PEER_CHUNK_I1
    ec = n_i1 * nk
    n_blocks = 2 * u.shape[1] // (PEER_SUBSTEPS * ec)
    n_sort = tm // PEER_SORT_W
    assert PEER_HEADS == SUBLANES
    full = lambda arr: pl.BlockSpec(arr.shape, lambda t, s: (0,) * arr.ndim, pipeline_mode=pl.Buffered(1))
    tok = pl.BlockSpec((tm, d), lambda t, s: (t, 0))
    return pl.pallas_call(
        functools.partial(_peer_kernel, tiles_per_batch, n_lat_tiles, n_batch, n_i1, final),
        grid=(rows // tm, n_blocks + 1),
        in_specs=[tok, tok, full(ada), full(final_g), full(wq), full(keys),
                  pl.BlockSpec((None, PEER_SUBSTEPS * ec // 2, d),
                               lambda t, s: (layer, jnp.minimum(s, n_blocks - 1), 0)),
                  pl.BlockSpec((None, d // 2, PEER_SUBSTEPS * ec), lambda t, s: (layer, 0, jnp.maximum(s - 1, 0)))],
        out_specs=tok,
        out_shape=jax.ShapeDtypeStruct((rows, d), F32),
        scratch_shapes=[pltpu.VMEM((PEER_HEADS, nk, tm), F32), pltpu.VMEM((PEER_HEADS, nk, tm), F32),
                        pltpu.VMEM((PEER_HEADS, nk, tm), F32), pltpu.VMEM((PEER_HEADS, nk, tm), F32),
                        pltpu.VMEM((PEER_HEADS, nk, tm), BF16), pltpu.VMEM((PEER_HEADS, nk, tm), BF16),
                        pltpu.VMEM((n_sort, PEER_TOPK, PEER_HEADS, PEER_SORT_W), F32),
                        pltpu.VMEM((n_sort, PEER_TOPK, PEER_HEADS, PEER_SORT_W), F32),
                        pltpu.VMEM((n_sort, PEER_TOPK, PEER_HEADS, PEER_SORT_W), F32),
                        pltpu.VMEM((n_sort, PEER_HEADS, PEER_SORT_W), F32),
                        pltpu.VMEM((ec, tm), F32), pltpu.VMEM((ec, tm), F32),
                        pltpu.VMEM((ec, tm), BF16), pltpu.VMEM((ec, tm), BF16),
                        pltpu.VMEM((d, tm), F32), pltpu.VMEM((d, tm), BF16)],
        compiler_params=_cparams(("parallel", "arbitrary")),
        name="peer",
    )(x1, h2, ada, final_g, wq, keys, u, vt)


def _rope_tables(seq, t):
    quarter = DA_QK // 4
    inv = ROPE_BASE ** (-np.arange(quarter, dtype=np.float32) / quarter)
    pos = np.arange(seq)
    ang_r = (pos // GRID_W).astype(np.float32)[:, None] * inv[None, :]
    ang_c = (pos % GRID_W).astype(np.float32)[:, None] * inv[None, :]
    half = np.concatenate([ang_r, ang_c, ang_r, ang_c], axis=1)
    cos = np.ones((t, LANES), np.float32)
    sin = np.zeros((t, LANES), np.float32)
    cos[:seq] = np.concatenate([np.cos(half), np.cos(half)], axis=1)
    sin[:seq] = np.concatenate([-np.sin(half), np.sin(half)], axis=1)
    return jnp.asarray(cos), jnp.asarray(sin)


def _pack_kernel(transpose, x_ref, o_ref):
    x = x_ref[0]
    if transpose:
        x = x.T
    o_ref[0] = pltpu.bitcast(x.astype(BF16), jnp.uint32)


def _pack_table(tab, transpose):
    depth, r, c = tab.shape
    rb = PACK_ROWS
    if transpose:
        in_spec = pl.BlockSpec((1, rb, c), lambda l, j: (l, j, 0))
        out_spec = pl.BlockSpec((1, c // 2, rb), lambda l, j: (l, 0, j))
        out_shape = (depth, c // 2, r)
    else:
        in_spec = pl.BlockSpec((1, rb, c), lambda l, j: (l, j, 0))
        out_spec = pl.BlockSpec((1, rb // 2, c), lambda l, j: (l, j, 0))
        out_shape = (depth, r // 2, c)
    return pl.pallas_call(
        functools.partial(_pack_kernel, transpose),
        grid=(depth, r // rb),
        in_specs=[in_spec],
        out_specs=out_spec,
        out_shape=jax.ShapeDtypeStruct(out_shape, jnp.uint32),
        compiler_params=_cparams(("parallel", "parallel")),
        name="pack_t" if transpose else "pack",
    )(tab)


def kernel(x, c, ctx, c_ctx, w_ada, b_ada, norm1_g, norm2_g, w_in, ml_gate_b, ml_conv_w, ml_norm_g, cv_conv_w,
           da_lam, da_subln_g, w_da, w_ml, w_cv, w_o, peer_wq, peer_keys, peer_u, peer_v, final_g):
    bsz, seq, d = x.shape
    n_ctx = ctx.shape[1]
    t = seq + n_ctx
    depth = w_ada.shape[0]
    tm = TOK_TILE
    assert seq % tm == 0 and n_ctx % tm == 0 and seq % ML_CHUNK == 0 and n_ctx % ML_CHUNK == 0
    assert (bsz * t) % PEER_TILE == 0 and (bsz * seq) % PEER_TILE == 0
    n_lat_tiles = seq // tm
    n_tiles = t // tm

    rows = -(-(bsz + 1) // SUBLANES) * SUBLANES
    cc = jnp.zeros((rows, d), F32).at[:bsz].set(c).at[bsz].set(c_ctx)
    ada = _ada(cc, w_ada, b_ada)

    o = np.cumsum([0, DA_WIDTH, DA_WIDTH, DA_WIDTH, 2 * ML_WIDTH, ML_WIDTH, ML_WIDTH, 4 * ML_HEADS,
                   CV_WIDTH, CV_WIDTH, CV_WIDTH, 3 * d])
    quarter = DA_QK // 4
    wqk = w_in[:, :, o[0]:o[2]].reshape(depth, d, 2 * DA_HEADS, 2, 2, 2, quarter)
    wqk = wqk.transpose(0, 1, 2, 5, 3, 4, 6).reshape(depth, d, 2 * DA_WIDTH)
    wda_in = jnp.concatenate([wqk, w_in[:, :, o[2]:o[3]]], axis=-1).astype(BF16)
    wml_in = w_in[:, :, o[3]:o[6]].astype(BF16)
    wmg_in = jnp.pad(w_in[:, :, o[6]:o[7]], ((0, 0), (0, 0), (0, LANES - 4 * ML_HEADS))).astype(BF16)
    wcv_in = w_in[:, :, o[7]:o[10]].astype(BF16)
    wgt_in = w_in[:, :, o[10]:o[11]].astype(BF16)
    gate_b = jnp.pad(ml_gate_b.reshape(depth, 1, 4 * ML_HEADS), ((0, 0), (0, 0), (0, LANES - 4 * ML_HEADS)))
    lam_p = da_lam
    w_da_b, w_ml_b, w_cv_b, w_o_b = (a.astype(BF16) for a in (w_da, w_ml, w_cv, w_o))
    wq_b = peer_wq.astype(BF16)
    keys_b = peer_keys.astype(BF16)
    u_b = _pack_table(peer_u, transpose=False)
    vt_b = _pack_table(peer_v, transpose=True)
    cos, sin = _rope_tables(seq, t)

    xs = jnp.concatenate([x, ctx], axis=1)
    for l in range(depth):
        need_ctx = l < depth - 1
        lam_init = 0.8 - 0.6 * math.exp(-0.3 * l)
        da, ml, mg, cv, gt = _inproj(xs, ada[l], norm1_g[l].reshape(1, d), cos, sin,
                                     wda_in[l], wml_in[l], wmg_in[l], wcv_in[l], wgt_in[l], n_lat_tiles)
        yda = _attention(da, lam_p[l], da_subln_g[l], n_lat_tiles, seq, lam_init)
        yml = _mlstm(ml, mg, ml_conv_w[l], gate_b[l], ml_norm_g[l], seq // ML_CHUNK, n_ctx // ML_CHUNK)
        n_out_tiles = n_tiles if need_ctx else n_lat_tiles
        x1, h2 = _merge(xs, ada[l], norm2_g[l].reshape(1, d), yda, yml, cv, gt, cv_conv_w[l],
                        w_da_b[l], w_ml_b[l], w_cv_b[l], w_o_b[l], n_lat_tiles, n_out_tiles)
        t_out = n_out_tiles * tm
        xs = _peer(x1.reshape(bsz * t_out, d), h2.reshape(bsz * t_out, d), ada[l], final_g.reshape(1, d),
                   wq_b[l], keys_b[l], u_b, vt_b, l, n_out_tiles, n_lat_tiles, bsz, final=not need_ctx)
        xs = xs.reshape(bsz, t_out, d)
    return xs
```

```python
import functools
import math

import numpy as np
import jax
import jax.numpy as jnp
from jax import lax
from jax.experimental import pallas as pl
from jax.experimental.pallas import tpu as pltpu

EPS = 1e-6
GRID_W = 64
DA_HEADS = 4
DA_QK = 64
DA_V = 2 * DA_QK
DA_WIDTH = DA_HEADS * DA_V
ROPE_BASE = 10000.0
ML_HEADS = 4
ML_DH = 128
ML_WIDTH = ML_HEADS * ML_DH
ML_CHUNK = 128
CV_WIDTH = 512
PEER_HEADS = 8
PEER_TOPK = 16
PEER_GROUP = 2

LANES = 128
SUBLANES = 8
ATTN_TILE = 512
TOK_TILE = 256
HALO = 16
PEER_TILE = 512
PEER_CHUNK_I1 = 8
PEER_SUBSTEPS = 2
PACK_ROWS = 512
PEER_MM_SLICES = 2
PEER_SORT_W = 2 * LANES
VMEM_LIMIT = 56 * 1024 * 1024

NEG_BIG = -1e30
BF16 = jnp.bfloat16
F32 = jnp.float32
HIGHEST = lax.Precision.HIGHEST


def _cparams(sem):
    return pltpu.CompilerParams(dimension_semantics=sem, vmem_limit_bytes=VMEM_LIMIT)


def _dot(a, b):
    return jnp.dot(a, b, preferred_element_type=F32)


def _dot_nt(a, b):
    return lax.dot_general(a, b, (((1,), (1,)), ((), ())), preferred_element_type=F32)


def _dot_tn(a, b):
    return lax.dot_general(a, b, (((0,), (0,)), ((), ())), preferred_element_type=F32)


def _rms(x, g):
    return x * lax.rsqrt(jnp.mean(x * x, axis=-1, keepdims=True) + EPS) * g


def _sigmoid(x):
    return 1.0 / (1.0 + jnp.exp(-x))


def _log_sigmoid(x):
    return jnp.minimum(x, 0.0) - jnp.log1p(jnp.exp(-jnp.abs(x)))


def _ada_kernel(c_ref, w_ref, b_ref, o_ref):
    c = c_ref[...]
    s = c * _sigmoid(c)
    o_ref[0] = jnp.dot(s, w_ref[0], preferred_element_type=F32, precision=HIGHEST) + b_ref[0]


def _ada(cc, w_ada, b_ada):
    depth, d, d6 = w_ada.shape
    rows = cc.shape[0]
    nb = d6 // d
    return pl.pallas_call(
        _ada_kernel,
        grid=(depth, nb),
        in_specs=[pl.BlockSpec((rows, d), lambda l, j: (0, 0)),
                  pl.BlockSpec((1, d, d), lambda l, j: (l, 0, j)),
                  pl.BlockSpec((1, 1, d), lambda l, j: (l, 0, j))],
        out_specs=pl.BlockSpec((1, rows, d), lambda l, j: (l, 0, j)),
        out_shape=jax.ShapeDtypeStruct((depth, rows, d6), F32),
        compiler_params=_cparams(("parallel", "parallel")),
        name="ada",
    )(cc, w_ada, b_ada.reshape(depth, 1, d6))


def _inproj_kernel(n_lat_tiles, n_batch, x_ref, ada_ref, g_ref, cos_ref, sin_ref,
                   wda_ref, wml_ref, wmg_ref, wcv_ref, wgt_ref,
                   da_ref, ml_ref, mg_ref, cv_ref, gt_ref):
    b = pl.program_id(0)
    i = pl.program_id(1)
    d = x_ref.shape[-1]
    row = jnp.where(i < n_lat_tiles, b, n_batch)
    shift = ada_ref[pl.ds(row, 1), 0:d]
    scale = ada_ref[pl.ds(row, 1), d:2 * d]
    h = _rms(x_ref[0], g_ref[...]) * (1.0 + scale) + shift
    hb = h.astype(BF16)

    zda = _dot(hb, wda_ref[...])
    cos = cos_ref[...]
    sin = sin_ref[...]
    for j in range(2 * DA_HEADS):
        blk = zda[:, j * LANES:(j + 1) * LANES]
        rot = blk * cos + pltpu.roll(blk, LANES // 2, 1) * sin
        da_ref[0, :, j * LANES:(j + 1) * LANES] = rot.astype(BF16)
    da_ref[0, :, 2 * DA_WIDTH:] = zda[:, 2 * DA_WIDTH:].astype(BF16)

    ml_ref[0] = _dot(hb, wml_ref[...]).astype(BF16)
    mg_ref[0] = _dot(hb, wmg_ref[...])

    zcv = _dot(hb, wcv_ref[...])
    cv_ref[0, :, 0:CV_WIDTH] = zcv[:, 0:CV_WIDTH].astype(BF16)
    cv_ref[0, :, CV_WIDTH:] = (zcv[:, CV_WIDTH:2 * CV_WIDTH] * zcv[:, 2 * CV_WIDTH:]).astype(BF16)

    for j in range(3):
        zg = _dot(hb, wgt_ref[:, j * d:(j + 1) * d])
        gt_ref[0, :, j * d:(j + 1) * d] = _sigmoid(zg).astype(BF16)


def _inproj(x, ada, g, cos, sin, wda, wml, wmg, wcv, wgt, n_lat_tiles):
    bsz, t, d = x.shape
    tm = TOK_TILE
    full = lambda arr: pl.BlockSpec(arr.shape, lambda b, i: (0,) * arr.ndim, pipeline_mode=pl.Buffered(1))
    tok = lambda w: pl.BlockSpec((1, tm, w), lambda b, i: (b, i, 0))
    widths =(wda.shape[1], wml.shape[1], wmg.shape[1], 2 * CV_WIDTH, wgt.shape[1])
    dts = (BF16, BF16, F32, BF16, BF16)
    return pl.pallas_call(
        functools.partial(_inproj_kernel, n_lat_tiles, bsz),
        grid=(bsz, t // tm),
        in_specs=[tok(d), full(ada), full(g),
                  pl.BlockSpec((tm, LANES), lambda b, i: (i, 0)),
                  pl.BlockSpec((tm, LANES), lambda b, i: (i, 0)),
                  full(wda), full(wml), full(wmg), full(wcv), full(wgt)],
        out_specs=[tok(w) for w in widths],
        out_shape=[jax.ShapeDtypeStruct((bsz, t, w), dt) for w, dt in zip(widths, dts)],
        compiler_params=_cparams(("parallel", "parallel")),
        name="inproj",
    )(x, ada, g, cos, sin, wda, wml, wmg, wcv, wgt)


def _attn_kernel(lam_init, q_ref, k_ref, v_ref, lam_ref, g_ref, *rest):
    o_ref, va_ref = rest[-2:]
    i = pl.program_id(2)
    lv = lam_ref[...]
    lam = (jnp.exp(jnp.sum(lv[0:1] * lv[1:2], axis=-1, keepdims=True))
           - jnp.exp(jnp.sum(lv[2:3] * lv[3:4], axis=-1, keepdims=True)) + lam_init)
    lane = lax.broadcasted_iota(jnp.int32, (1, LANES), 1)
    map0 = (lane % (LANES // 2)) < (DA_QK // 2)

    @pl.when(i == 0)
    def _():
        va_ref[:, 0:LANES] = v_ref[0]
        va_ref[:, LANES:] = jnp.ones((va_ref.shape[0], LANES), BF16)

    k = k_ref[0]
    va = va_ref[...]
    q = q_ref[0].astype(F32) * (DA_QK ** -0.5)
    q0 = jnp.where(map0, q, 0.0).astype(BF16)
    q1 = jnp.where(map0, 0.0, q).astype(BF16)
    s0 = _dot_nt(q0, k)
    s1 = _dot_nt(q1, k)
    e0 = jnp.exp((s0 - jnp.max(s0, axis=-1, keepdims=True)).astype(BF16))
    e1 = jnp.exp((s1 - jnp.max(s1, axis=-1, keepdims=True)).astype(BF16))
    o0 = _dot(e0, va)
    o1 = _dot(e1, va)
    o = o0[:, 0:LANES] / o0[:, LANES:] - lam * (o1[:, 0:LANES] / o1[:, LANES:])
    o_ref[0] = (_rms(o, g_ref[...]) * (1.0 - lam_init)).astype(o_ref.dtype)


def _attention(da, lam_p, subln_g, lam_init, tq, q_row0, n_q, kv_row0, kv_rows, prev=None):
    bsz, t, _ = da.shape
    q0 = q_row0 // tq
    kv0 = kv_row0 // kv_rows
    in_specs = [pl.BlockSpec((1, tq, LANES), lambda b, h, i: (b, q0 + i, h)),
                pl.BlockSpec((1, kv_rows, LANES), lambda b, h, i: (b, kv0, DA_HEADS + h)),
                pl.BlockSpec((1, kv_rows, LANES), lambda b, h, i: (b, kv0, 2 * DA_HEADS + h)),
                pl.BlockSpec(lam_p.shape, lambda b, h, i: (0, 0)),
                pl.BlockSpec((1, LANES), lambda b, h, i: (0, 0))]
    args = [da, da, da, lam_p, subln_g.reshape(1, LANES)]
    aliases = {}
    if prev is not None:
        in_specs.append(pl.BlockSpec(memory_space=pl.ANY))
        args.append(prev)
        aliases = {len(args) - 1: 0}
    return pl.pallas_call(
        functools.partial(_attn_kernel, lam_init),
        grid=(bsz, DA_HEADS, n_q),
        in_specs=in_specs,
        out_specs=pl.BlockSpec((1, tq, LANES), lambda b, h, i: (b, q0 + i, h)),
        out_shape=jax.ShapeDtypeStruct((bsz, t, DA_WIDTH), BF16),
        scratch_shapes=[pltpu.VMEM((kv_rows, 2 * LANES), BF16)],
        input_output_aliases=aliases,
        compiler_params=_cparams(("parallel", "parallel", "arbitrary")),
        name="diffattn",
    )(*args)


def _mlstm_kernel(ncl, ncc, ml_ref, mg_ref, cw_ref, gb_ref, ng_ref, o_ref,
                  qs_ref, ks_ref, hf_ref, hb_ref, c_ref, n_ref, m_ref):
    L = ML_CHUNK
    nc = ncl + ncc
    t_all = nc * L
    seq = ncl * L
    w = ML_WIDTH

    trow = lax.broadcasted_iota(jnp.int32, (t_all, 1), 0)
    first = (trow == 0) | (trow == seq)
    last = (trow == seq - 1) | (trow == t_all - 1)
    for j in range(2 * ML_HEADS):
        x = ml_ref[0, :, j * LANES:(j + 1) * LANES].astype(F32)
        xm = jnp.where(first, 0.0, pltpu.roll(x, 1, 0))
        xp = jnp.where(last, 0.0, pltpu.roll(x, t_all - 1, 0))
        cw = cw_ref[:, j * LANES:(j + 1) * LANES]
        y = xm * cw[0:1] + x * cw[1:2] + xp * cw[2:3]
        y = y * _sigmoid(y)
        if j < ML_HEADS:
            qs_ref[:, j * LANES:(j + 1) * LANES] = y.astype(BF16)
        else:
            jj = j - ML_HEADS
            ks_ref[:, jj * LANES:(jj + 1) * LANES] = (y * (ML_DH ** -0.5)).astype(BF16)

    c_ref[...] = jnp.zeros_like(c_ref)
    n_ref[...] = jnp.zeros_like(n_ref)
    m_ref[...] = jnp.zeros_like(m_ref)

    r_i = lax.broadcasted_iota(jnp.int32, (L, L), 0)
    c_i = lax.broadcasted_iota(jnp.int32, (L, L), 1)
    lower = r_i >= c_i
    upper = r_i <= c_i
    lower_f = lower.astype(F32)
    upper_f = upper.astype(F32)

    def step(it, carry):
        for dr in range(2):
            if dr == 0:
                ch = (it + ncl) % nc
                seen, tri_col, tri_row, end = lower, lower_f, upper_f, L - 1
                h_ref = hf_ref
            else:
                ch = nc - 1 - it
                seen, tri_col, tri_row, end = upper, upper_f, lower_f, 0
                h_ref = hb_ref
            r0 = pl.multiple_of(ch * L, L)
            g = mg_ref[0, pl.ds(r0, L), :] + gb_ref[...]
            gt = g.T
            bcol_all = jnp.dot(tri_col, _log_sigmoid(g), preferred_element_type=F32, precision=HIGHEST)
            brow_all = jnp.dot(_log_sigmoid(gt), tri_row, preferred_element_type=F32, precision=HIGHEST)
            for h in range(ML_HEADS):
                idx = dr * ML_HEADS + h
                icol = (2 * dr) * ML_HEADS + h
                fcol = (2 * dr + 1) * ML_HEADS + h
                q = qs_ref[pl.ds(r0, L), h * LANES:(h + 1) * LANES]
                k = ks_ref[pl.ds(r0, L), h * LANES:(h + 1) * LANES]
                v = ml_ref[0, pl.ds(r0, L), 2 * w + h * LANES:2 * w + (h + 1) * LANES]
                b_col = bcol_all[:, fcol:fcol + 1]
                ig_col = g[:, icol:icol + 1]
                b_row = brow_all[fcol:fcol + 1, :]
                ig_row = gt[icol:icol + 1, :]
                m_prev = m_ref[idx][:, 0:1]
                cmat = c_ref[idx]
                nrow = n_ref[idx]

                log_d = jnp.where(seen, b_col - b_row + ig_row, NEG_BIG)
                m_path = b_col + m_prev
                m_t = jnp.maximum(m_path, jnp.max(log_d, axis=-1, keepdims=True))
                dmat = jnp.exp(log_d - m_t)
                carry_w = jnp.exp(m_path - m_t)
                wmat = _dot_nt(q, k) * dmat
                num = _dot(wmat.astype(BF16), v) + carry_w * _dot(q, cmat.astype(BF16))
                den = (jnp.sum(wmat, axis=-1, keepdims=True)
                       + carry_w * jnp.sum(q.astype(F32) * nrow, axis=-1, keepdims=True))
                hval = num / jnp.maximum(jnp.abs(den), jnp.exp(-m_t))
                h_ref[pl.ds(r0, L), h * LANES:(h + 1) * LANES] = hval

                b_end = b_col[end:end + 1, :]
                log_w = b_end - b_col + ig_col
                m_new = jnp.maximum(b_end + m_prev, jnp.max(log_w, axis=0, keepdims=True))
                wk = jnp.exp(log_w - m_new)
                decay = jnp.exp(b_end + m_prev - m_new)
                kw = k.astype(F32) * wk
                c_ref[idx] = decay * cmat + _dot_tn(kw.astype(BF16), v)
                n_ref[idx] = decay * nrow + jnp.sum(kw, axis=0, keepdims=True)
                m_ref[idx] = jnp.broadcast_to(m_new, (1, LANES))
        return carry

    lax.fori_loop(0, nc, step, 0)

    for h in range(ML_HEADS):
        sl = slice(h * LANES, (h + 1) * LANES)
        hs = hf_ref[:, sl] + hb_ref[:, sl]
        y = _rms(hs, ng_ref[:, sl])
        og = ml_ref[0, :, 3 * w + h * LANES:3 * w + (h + 1) * LANES].astype(F32)
        o_ref[0, :, sl] = (y * _sigmoid(og)).astype(o_ref.dtype)


def _mlstm(ml, mg, conv_w, gate_b_row, norm_g, ncl, ncc):
    bsz, t, wtot = ml.shape
    nstate = 2 * ML_HEADS
    return pl.pallas_call(
        functools.partial(_mlstm_kernel, ncl, ncc),
        grid=(bsz,),
        in_specs=[pl.BlockSpec((1, t, wtot), lambda b: (b, 0, 0)),
                  pl.BlockSpec((1, t, LANES), lambda b: (b, 0, 0)),
                  pl.BlockSpec(conv_w.shape, lambda b: (0, 0)),
                  pl.BlockSpec((1, LANES), lambda b: (0, 0)),
                  pl.BlockSpec((1, ML_WIDTH), lambda b: (0, 0))],
        out_specs=pl.BlockSpec((1, t, ML_WIDTH), lambda b: (b, 0, 0)),
        out_shape=jax.ShapeDtypeStruct((bsz, t, ML_WIDTH), BF16),
        scratch_shapes=[pltpu.VMEM((t, ML_WIDTH), BF16), pltpu.VMEM((t, ML_WIDTH), BF16),
                        pltpu.VMEM((t, ML_WIDTH), F32), pltpu.VMEM((t, ML_WIDTH), F32),
                        pltpu.VMEM((nstate, ML_DH, ML_DH), F32),
                        pltpu.VMEM((nstate, 1, ML_DH), F32),
                        pltpu.VMEM((nstate, 1, LANES), F32)],
        compiler_params=_cparams(("parallel",)),
        name="mlstm",
    )(ml, mg, conv_w, gate_b_row, norm_g.reshape(1, ML_WIDTH))


def _merge_kernel(n_lat_tiles, n_batch, x_ref, ada_ref, g2_ref, yda_ref, yml_ref, cv_ref, pprev_ref, pnext_ref,
                  gt_ref, cw_ref, wda_ref, wml_ref, wcv_ref, wo_ref, x1_ref, h2_ref):
    b = pl.program_id(0)
    i = pl.program_id(1)
    n_tiles = pl.num_programs(1)
    d = x_ref.shape[-1]
    tm = x_ref.shape[1]
    row = jnp.where(i < n_lat_tiles, b, n_batch)

    bg = cv_ref[0, :, 0:CV_WIDTH].astype(F32)
    p = cv_ref[0, :, CV_WIDTH:].astype(F32)
    seq_start = (i == 0) | (i == n_lat_tiles)
    seq_end = (i == n_lat_tiles - 1) | (i == n_tiles - 1)
    prev_row = jnp.where(seq_start, 0.0, pprev_ref[0, HALO - 1:HALO, :].astype(F32))
    next_row = jnp.where(seq_end, 0.0, pnext_ref[0, 0:1, :].astype(F32))
    trow = lax.broadcasted_iota(jnp.int32, (tm, 1), 0)
    pm = jnp.where(trow == 0, prev_row, pltpu.roll(p, 1, 0))
    pp = jnp.where(trow == tm - 1, next_row, pltpu.roll(p, tm - 1, 0))
    cw = cw_ref[...]
    ycv = bg * (pm * cw[0:1] + p * cw[1:2] + pp * cw[2:3])

    m = (gt_ref[0, :, 0:d].astype(F32) * _dot(yda_ref[0], wda_ref[...])
         + gt_ref[0, :, d:2 * d].astype(F32) * _dot(yml_ref[0], wml_ref[...])
         + gt_ref[0, :, 2 * d:].astype(F32) * _dot(ycv.astype(BF16), wcv_ref[...]))
    out = _dot(m.astype(BF16), wo_ref[...])
    g1 = ada_ref[pl.ds(row, 1), 2 * d:3 * d]
    x1 = x_ref[0] + g1 * out
    x1_ref[0] = x1
    sh2 = ada_ref[pl.ds(row, 1), 3 * d:4 * d]
    sc2 = ada_ref[pl.ds(row, 1), 4 * d:5 * d]
    h2_ref[0] = (_rms(x1, g2_ref[...]) * (1.0 + sc2) + sh2).astype(BF16)


def _merge(x, ada, g2, yda, yml, cv, gt, cv_w, wda, wml, wcv, wo, n_lat_tiles, n_tiles):
    bsz, t, d = x.shape
    tm = TOK_TILE
    r = tm // HALO
    nrb = t // HALO
    full = lambda arr: pl.BlockSpec(arr.shape, lambda b, i: (0,) * arr.ndim, pipeline_mode=pl.Buffered(1))
    tok = lambda w: pl.BlockSpec((1, tm, w), lambda b, i: (b, i, 0))
    return pl.pallas_call(
        functools.partial(_merge_kernel, n_lat_tiles, bsz),
        grid=(bsz, n_tiles),
        in_specs=[tok(d), full(ada), full(g2), tok(DA_WIDTH), tok(ML_WIDTH), tok(2 * CV_WIDTH),
                  pl.BlockSpec((1, HALO, CV_WIDTH), lambda b, i: (b, jnp.maximum(i * r - 1, 0), 1)),
                  pl.BlockSpec((1, HALO, CV_WIDTH), lambda b, i: (b, jnp.minimum((i + 1) * r, nrb - 1), 1)),
                  tok(3 * d), full(cv_w), full(wda), full(wml), full(wcv), full(wo)],
        out_specs=[tok(d), tok(d)],
        out_shape=[jax.ShapeDtypeStruct((bsz, n_tiles * tm, d), F32),
                   jax.ShapeDtypeStruct((bsz, n_tiles * tm, d), BF16)],
        compiler_params=_cparams(("parallel", "parallel")),
        name="merge",
    )(x, ada, g2, yda, yml, cv, cv, cv, gt, cv_w, wda, wml, wcv, wo)


def _cmpx(v, i, j):
    a, b = v[i], v[j]
    v[i] = jnp.maximum(a, b)
    v[j] = jnp.minimum(a, b)


def _bitonic_merge_desc(v):
    n = len(v)
    j = n // 2
    while j >= 1:
        for i in range(n):
            l = i ^ j
            if l > i:
                _cmpx(v, i, l)
        j //= 2


def _bitonic_sort_desc(v):
    n = len(v)
    k = 2
    while k <= n:
        j = k // 2
        while j >= 1:
            for i in range(n):
                l = i ^ j
                if l > i:
                    if (i & k) == 0:
                        _cmpx(v, i, l)
                    else:
                        _cmpx(v, l, i)
            j //= 2
        k *= 2


def _merge_top(a, b):
    n = len(a)
    c = [jnp.maximum(a[r], b[n - 1 - r]) for r in range(n)]
    _bitonic_merge_desc(c)
    return c


def _top16_sorted(s):
    nk, tm = s.shape
    k = PEER_TOPK
    groups = nk // k
    assert groups == SUBLANES
    a = s.reshape(k, SUBLANES, tm)
    v = [a[r] for r in range(k)]
    _bitonic_sort_desc(v)
    sh = SUBLANES // 2
    while sh >= 1:
        v = _merge_top(v, [pltpu.roll(x, sh, 0) for x in v])
        sh //= 2
    return v


def _erf(x):
    return lax.erf(x)


def _peer_kernel(tiles_per_batch, n_lat_tiles, n_batch, n_i1, final, x_ref, h_ref, ada_ref, fg_ref, wq_ref, keys_ref,
                 u_ref, vt_ref, o_ref, s1_ref, s2_ref, f1_ref, cnt_ref, e2_ref, rk2_ref, t1_ref, t2_ref, cr_ref, rz_ref,
                 at0_ref, at1_ref, w0_ref, w1_ref, acc_ref, ht_ref):
    t = pl.program_id(0)
    s = pl.program_id(1)
    n_steps = pl.num_programs(1)
    n_chunks = (n_steps - 1) * PEER_SUBSTEPS
    d = x_ref.shape[-1]
    tm = x_ref.shape[0]
    nk = keys_ref.shape[2]
    dsub = keys_ref.shape[3]
    k = PEER_TOPK
    sort_w = PEER_SORT_W
    at_refs = (at0_ref, at1_ref)
    w_refs = (w0_ref, w1_ref)
    zero_b = jnp.zeros((), BF16)

    @pl.when(s == 0)
    def _():
        hb = h_ref[...]
        for half in range(tm // TOK_TILE):
            cs = slice(half * TOK_TILE, (half + 1) * TOK_TILE)
            ht_ref[:, cs] = hb[cs, :].astype(F32).T.astype(BF16)
        q = _dot(hb, wq_ref[...]).astype(BF16)
        for h in range(PEER_HEADS):
            for p in range(2):
                col = (h * 2 + p) * dsub
                (s1_ref if p == 0 else s2_ref)[h] = _dot_nt(keys_ref[h, p], q[:, col:col + dsub])
        neg = jnp.full((SUBLANES, sort_w), -jnp.inf, F32)
        for blk in range(tm // sort_w):
            ls = slice(blk * sort_w, (blk + 1) * sort_w)

            def sort_head(h, carry):
                for sref, tref in ((s1_ref, t1_ref), (s2_ref, t2_ref)):
                    v = _top16_sorted(sref[h, :, ls])
                    for r in range(k):
                        tref[blk, r, pl.ds(h, 1), :] = v[r][0:1, :]
                return carry

            lax.fori_loop(0, PEER_HEADS, sort_head, 0)
            t1 = [t1_ref[blk, r] for r in range(k)]
            t2 = [t2_ref[blk, r] for r in range(k)]
            rows = [[t1[r1] + t2[r2] for r2 in range(k // (r1 + 1))] for r1 in range(k)]
            best = rows[0]
            rest = [x for rw in rows[1:] for x in rw]
            while rest:
                grp, rest = rest[:k], rest[k:]
                grp = grp + [neg] * (k - len(grp))
                _bitonic_sort_desc(grp)
                best = _merge_top(best, grp)
            mx = best[0]
            z = jnp.exp(best[0] - mx)
            for r in range(1, k):
                z = z + jnp.exp(best[r] - mx)
            rz = 1.0 / z
            tau = best[k - 1]
            for r1 in range(k):
                cnt = jnp.where(rows[r1][0] >= tau, 1.0, 0.0)
                for r2 in range(1, k // (r1 + 1)):
                    cnt = cnt + jnp.where(rows[r1][r2] >= tau, 1.0, 0.0)
                cr_ref[blk, r1] = cnt
            rz_ref[blk] = rz

            def finish_head(h, carry):
                hs = pl.ds(h, 1)
                s1h = s1_ref[h, :, ls]
                s2h = s2_ref[h, :, ls]
                cn = jnp.zeros_like(s1h)
                rk = jnp.full_like(s2h, float(k))
                for r in reversed(range(k)):
                    cn = jnp.where(s1h == t1_ref[blk, r, hs, :], cr_ref[blk, r, hs, :], cn)
                    rk = jnp.where(s2h >= t2_ref[blk, r, hs, :], float(r), rk)
                cnt_ref[h, :, ls] = cn
                rk2_ref[h, :, ls] = rk.astype(BF16)
                f1_ref[h, :, ls] = jnp.exp(s1h - t1_ref[blk, 0, hs, :]) * rz_ref[blk, hs, :]
                e2_ref[h, :, ls] = jnp.exp(s2h - t2_ref[blk, 0, hs, :]).astype(BF16)
                return carry

            lax.fori_loop(0, PEER_HEADS, finish_head, 0)
        acc_ref[...] = jnp.zeros_like(acc_ref)

    ec = n_i1 * nk
    rb_rows = nk // 2
    n_grp = n_i1 // PEER_GROUP
    g_rows = ec // n_grp
    gpm = n_grp // PEER_MM_SLICES
    d_rows = d // PEER_MM_SLICES
    e_rows = ec // PEER_MM_SLICES

    def gate_group(grp, at_oth, w_oth, cb):
        g_off = pl.multiple_of(grp * g_rows, g_rows)
        i1s = [cb * n_i1 + grp * PEER_GROUP + jj for jj in range(PEER_GROUP)]
        cnrows = [[cnt_ref[h, pl.ds(i1, 1), :].astype(BF16) for i1 in i1s] for h in range(PEER_HEADS)]
        f1rows = [[f1_ref[h, pl.ds(i1, 1), :].astype(BF16) for i1 in i1s] for h in range(PEER_HEADS)]
        for tb in range(tm // LANES):
            ts = slice(tb * LANES, (tb + 1) * LANES)
            for rb in range(nk // rb_rows):
                rs = slice(rb * rb_rows, (rb + 1) * rb_rows)
                gs = [jnp.zeros((rb_rows, LANES), BF16) for _ in range(PEER_GROUP)]
                for h in range(PEER_HEADS):
                    rk2 = rk2_ref[h, rs, ts]
                    e2b = e2_ref[h, rs, ts]
                    for jj in range(PEER_GROUP):
                        cnr = cnrows[h][jj][:, ts]
                        f1r = f1rows[h][jj][:, ts]
                        gs[jj] = gs[jj] + jnp.where(rk2 < cnr, e2b * f1r, zero_b)
                for jj in range(PEER_GROUP):
                    r0 = pl.multiple_of(g_off + jj * nk + rb * rb_rows, rb_rows)
                    a = at_oth[pl.ds(r0, rb_rows), ts].astype(BF16)
                    act = (0.5 * a) * (1.0 + _erf(a * (2.0 ** -0.5)))
                    w_oth[pl.ds(r0, rb_rows), ts] = act * gs[jj]

    def substeps(do1, do2, do3):
        for sub in range(PEER_SUBSTEPS):
            at_cur, at_oth = at_refs[sub], at_refs[1 - sub]
            w_cur, w_oth = w_refs[sub], w_refs[1 - sub]
            cb = s * PEER_SUBSTEPS + sub - 1

            def mm_slice(ms, carry, sub=sub, at_cur=at_cur, at_oth=at_oth, w_cur=w_cur, w_oth=w_oth, cb=cb):
                if do1[sub]:
                    e_off = pl.multiple_of(ms * e_rows, e_rows)
                    u_off = pl.multiple_of((sub * ec + e_off) // 2, e_rows // 2)
                    u_rows = pltpu.bitcast(u_ref[pl.ds(u_off, e_rows // 2), :], BF16)
                    at_cur[pl.ds(e_off, e_rows), :] = _dot(u_rows, ht_ref[...])
                if do3[sub]:
                    d_off = pl.multiple_of(ms * d_rows, d_rows)
                    v_off = pl.multiple_of(d_off // 2, d_rows // 2)
                    v_rows = pltpu.bitcast(vt_ref[pl.ds(v_off, d_rows // 2), sub * ec:(sub + 1) * ec], BF16)
                    acc_ref[pl.ds(d_off, d_rows), :] += _dot(v_rows, w_cur[...])
                if do2[sub]:
                    for gi in range(gpm):
                        gate_group(ms * gpm + gi, at_oth, w_oth, cb)
                return carry

            lax.fori_loop(0, PEER_MM_SLICES, mm_slice, 0)

    first = s == 0
    last = s == n_steps - 1

    @pl.when(first)
    def _():
        substeps(do1=(True, True), do2=(False, True), do3=(False, False))

    @pl.when(jnp.logical_not(first | last))
    def _():
        substeps(do1=(True, True), do2=(True, True), do3=(True, True))

    @pl.when(last)
    def _():
        substeps(do1=(False, False), do2=(True, False), do3=(True, True))

    @pl.when(s == n_steps - 1)
    def _():
        for half in range(tm // TOK_TILE):
            g256 = t * (tm // TOK_TILE) + half
            row = jnp.where(g256 % tiles_per_batch < n_lat_tiles, g256 // tiles_per_batch, n_batch)
            g2 = ada_ref[pl.ds(row, 1), 5 * d:6 * d]
            rs = slice(half * TOK_TILE, (half + 1) * TOK_TILE)
            x2 = x_ref[rs, :] + g2 * acc_ref[:, rs].T
            if final:
                x2 = _rms(x2, fg_ref[...])
            o_ref[rs, :] = x2


def _peer(x1, h2, ada, final_g, wq, keys, u, vt, layer, tiles_per_batch, n_lat_tiles, n_batch, final):
    rows, d = x1.shape
    tm = PEER_TILE
    nk = keys.shape[2]
    n_i1 = PEER_CHUNK_I1
    ec = n_i1 * nk
    n_blocks = 2 * u.shape[1] // (PEER_SUBSTEPS * ec)
    n_sort = tm // PEER_SORT_W
    assert PEER_HEADS == SUBLANES
    full = lambda arr: pl.BlockSpec(arr.shape, lambda t, s: (0,) * arr.ndim, pipeline_mode=pl.Buffered(1))
    tok = pl.BlockSpec((tm, d), lambda t, s: (t, 0))
    return pl.pallas_call(
        functools.partial(_peer_kernel, tiles_per_batch, n_lat_tiles, n_batch, n_i1, final),
        grid=(rows // tm, n_blocks + 1),
        in_specs=[tok, tok, full(ada), full(final_g), full(wq), full(keys),
                  pl.BlockSpec((None, PEER_SUBSTEPS * ec // 2, d),
                               lambda t, s: (layer, jnp.minimum(s, n_blocks - 1), 0)),
                  pl.BlockSpec((None, d // 2, PEER_SUBSTEPS * ec), lambda t, s: (layer, 0, jnp.maximum(s - 1, 0)))],
        out_specs=tok,
        out_shape=jax.ShapeDtypeStruct((rows, d), F32),
        scratch_shapes=[pltpu.VMEM((PEER_HEADS, nk, tm), F32), pltpu.VMEM((PEER_HEADS, nk, tm), F32),
                        pltpu.VMEM((PEER_HEADS, nk, tm), F32), pltpu.VMEM((PEER_HEADS, nk, tm), F32),
                        pltpu.VMEM((PEER_HEADS, nk, tm), BF16), pltpu.VMEM((PEER_HEADS, nk, tm), BF16),
                        pltpu.VMEM((n_sort, PEER_TOPK, PEER_HEADS, PEER_SORT_W), F32),
                        pltpu.VMEM((n_sort, PEER_TOPK, PEER_HEADS, PEER_SORT_W), F32),
                        pltpu.VMEM((n_sort, PEER_TOPK, PEER_HEADS, PEER_SORT_W), F32),
                        pltpu.VMEM((n_sort, PEER_HEADS, PEER_SORT_W), F32),
                        pltpu.VMEM((ec, tm), F32), pltpu.VMEM((ec, tm), F32),
                        pltpu.VMEM((ec, tm), BF16), pltpu.VMEM((ec, tm), BF16),
                        pltpu.VMEM((d, tm), F32), pltpu.VMEM((d, tm), BF16)],
        compiler_params=_cparams(("parallel", "arbitrary")),
        name="peer",
    )(x1, h2, ada, final_g, wq, keys, u, vt)


def _rope_tables(seq, t):
    quarter = DA_QK // 4
    inv = ROPE_BASE ** (-np.arange(quarter, dtype=np.float32) / quarter)
    pos = np.arange(seq)
    ang_r = (pos // GRID_W).astype(np.float32)[:, None] * inv[None, :]
    ang_c = (pos % GRID_W).astype(np.float32)[:, None] * inv[None, :]
    half = np.concatenate([ang_r, ang_c, ang_r, ang_c], axis=1)
    cos = np.ones((t, LANES), np.float32)
    sin = np.zeros((t, LANES), np.float32)
    cos[:seq] = np.concatenate([np.cos(half), np.cos(half)], axis=1)
    sin[:seq] = np.concatenate([-np.sin(half), np.sin(half)], axis=1)
    return jnp.asarray(cos), jnp.asarray(sin)


def _pack_kernel(transpose, x_ref, o_ref):
    x = x_ref[0]
    if transpose:
        x = x.T
    o_ref[0] = pltpu.bitcast(x.astype(BF16), jnp.uint32)


def _pack_table(tab, transpose):
    depth, r, c = tab.shape
    rb = PACK_ROWS
    if transpose:
        in_spec = pl.BlockSpec((1, rb, c), lambda l, j: (l, j, 0))
        out_spec = pl.BlockSpec((1, c // 2, rb), lambda l, j: (l, 0, j))
        out_shape = (depth, c // 2, r)
    else:
        in_spec = pl.BlockSpec((1, rb, c), lambda l, j: (l, j, 0))
        out_spec = pl.BlockSpec((1, rb // 2, c), lambda l, j: (l, j, 0))
        out_shape = (depth, r // 2, c)
    return pl.pallas_call(
        functools.partial(_pack_kernel, transpose),
        grid=(depth, r // rb),
        in_specs=[in_spec],
        out_specs=out_spec,
        out_shape=jax.ShapeDtypeStruct(out_shape, jnp.uint32),
        compiler_params=_cparams(("parallel", "parallel")),
        name="pack_t" if transpose else "pack",
    )(tab)


def kernel(x, c, ctx, c_ctx, w_ada, b_ada, norm1_g, norm2_g, w_in, ml_gate_b, ml_conv_w, ml_norm_g, cv_conv_w,
           da_lam, da_subln_g, w_da, w_ml, w_cv, w_o, peer_wq, peer_keys, peer_u, peer_v, final_g):
    bsz, seq, d = x.shape
    n_ctx = ctx.shape[1]
    t = seq + n_ctx
    depth = w_ada.shape[0]
    tm = TOK_TILE
    assert seq % tm == 0 and n_ctx % tm == 0 and seq % ML_CHUNK == 0 and n_ctx % ML_CHUNK == 0
    assert (bsz * t) % PEER_TILE == 0 and (bsz * seq) % PEER_TILE == 0
    assert seq % ATTN_TILE == 0 and seq % n_ctx == 0
    n_lat_tiles = seq // tm
    n_tiles = t // tm

    rows = -(-(bsz + 1) // SUBLANES) * SUBLANES
    cc = jnp.zeros((rows, d), F32).at[:bsz].set(c).at[bsz].set(c_ctx)
    ada = _ada(cc, w_ada, b_ada)

    o = np.cumsum([0, DA_WIDTH, DA_WIDTH, DA_WIDTH, 2 * ML_WIDTH, ML_WIDTH, ML_WIDTH, 4 * ML_HEADS,
                   CV_WIDTH, CV_WIDTH, CV_WIDTH, 3 * d])
    quarter = DA_QK // 4
    wqk = w_in[:, :, o[0]:o[2]].reshape(depth, d, 2 * DA_HEADS, 2, 2, 2, quarter)
    wqk = wqk.transpose(0, 1, 2, 5, 3, 4, 6).reshape(depth, d, 2 * DA_WIDTH)
    wda_in = jnp.concatenate([wqk, w_in[:, :, o[2]:o[3]]], axis=-1).astype(BF16)
    wml_in = w_in[:, :, o[3]:o[6]].astype(BF16)
    wmg_in = jnp.pad(w_in[:, :, o[6]:o[7]], ((0, 0), (0, 0), (0, LANES - 4 * ML_HEADS))).astype(BF16)
    wcv_in = w_in[:, :, o[7]:o[10]].astype(BF16)
    wgt_in = w_in[:, :, o[10]:o[11]].astype(BF16)
    gate_b = jnp.pad(ml_gate_b.reshape(depth, 1, 4 * ML_HEADS), ((0, 0), (0, 0), (0, LANES - 4 * ML_HEADS)))
    lam_p = da_lam
    w_da_b, w_ml_b, w_cv_b, w_o_b = (a.astype(BF16) for a in (w_da, w_ml, w_cv, w_o))
    wq_b = peer_wq.astype(BF16)
    keys_b = peer_keys.astype(BF16)
    u_b = _pack_table(peer_u, transpose=False)
    vt_b = _pack_table(peer_v, transpose=True)
    cos, sin = _rope_tables(seq, t)

    xs = jnp.concatenate([x, ctx], axis=1)
    for l in range(depth):
        need_ctx = l < depth - 1
        lam_init = 0.8 - 0.6 * math.exp(-0.3 * l)
        da, ml, mg, cv, gt = _inproj(xs, ada[l], norm1_g[l].reshape(1, d), cos, sin,
                                     wda_in[l], wml_in[l], wmg_in[l], wcv_in[l], wgt_in[l], n_lat_tiles)
        yda = _attention(da, lam_p[l], da_subln_g[l], lam_init, ATTN_TILE, 0, seq // ATTN_TILE, 0, t)
        if need_ctx:
            yda = _attention(da, lam_p[l], da_subln_g[l], lam_init, n_ctx, seq, 1, seq, n_ctx, prev=yda)
        yml = _mlstm(ml, mg, ml_conv_w[l], gate_b[l], ml_norm_g[l], seq // ML_CHUNK, n_ctx // ML_CHUNK)
        n_out_tiles = n_tiles if need_ctx else n_lat_tiles
        x1, h2 = _merge(xs, ada[l], norm2_g[l].reshape(1, d), yda, yml, cv, gt, cv_conv_w[l],
                        w_da_b[l], w_ml_b[l], w_cv_b[l], w_o_b[l], n_lat_tiles, n_out_tiles)
        t_out = n_out_tiles * tm
        xs = _peer(x1.reshape(bsz * t_out, d), h2.reshape(bsz * t_out, d), ada[l], final_g.reshape(1, d),
                   wq_b[l], keys_b[l], u_b, vt_b, l, n_out_tiles, n_lat_tiles, bsz, final=not need_ctx)
        xs = xs.reshape(bsz, t_out, d)
    return xs
```

```python
import functools
import math

import numpy as np
import jax
import jax.numpy as jnp
from jax import lax
from jax.experimental import pallas as pl
from jax.experimental.pallas import tpu as pltpu

EPS = 1e-6
GRID_W = 64
DA_HEADS = 4
DA_QK = 64
DA_V = 2 * DA_QK
DA_WIDTH = DA_HEADS * DA_V
ROPE_BASE = 10000.0
ML_HEADS = 4
ML_DH = 128
ML_WIDTH = ML_HEADS * ML_DH
ML_CHUNK = 128
CV_WIDTH = 512
PEER_HEADS = 8
PEER_TOPK = 16
PEER_GROUP = 2

LANES = 128
SUBLANES = 8
ATTN_TILE = 1024
TOK_TILE = 256
HALO = 16
PEER_TILE = 512
PEER_CHUNK_I1 = 8
PEER_SUBSTEPS = 2
PACK_ROWS = 512
PEER_MM_SLICES = 2
PEER_SORT_W = 2 * LANES
VMEM_LIMIT = 56 * 1024 * 1024

NEG_BIG = -1e30
BF16 = jnp.bfloat16
F32 = jnp.float32
HIGHEST = lax.Precision.HIGHEST


def _cparams(sem):
    return pltpu.CompilerParams(dimension_semantics=sem, vmem_limit_bytes=VMEM_LIMIT)


def _dot(a, b):
    return jnp.dot(a, b, preferred_element_type=F32)


def _dot_nt(a, b):
    return lax.dot_general(a, b, (((1,), (1,)), ((), ())), preferred_element_type=F32)


def _dot_tn(a, b):
    return lax.dot_general(a, b, (((0,), (0,)), ((), ())), preferred_element_type=F32)


def _rms(x, g):
    return x * lax.rsqrt(jnp.mean(x * x, axis=-1, keepdims=True) + EPS) * g


def _sigmoid(x):
    return 1.0 / (1.0 + jnp.exp(-x))


def _log_sigmoid(x):
    return jnp.minimum(x, 0.0) - jnp.log1p(jnp.exp(-jnp.abs(x)))


def _ada_kernel(c_ref, w_ref, b_ref, o_ref):
    c = c_ref[...]
    s = c * _sigmoid(c)
    o_ref[0] = jnp.dot(s, w_ref[0], preferred_element_type=F32, precision=HIGHEST) + b_ref[0]


def _ada(cc, w_ada, b_ada):
    depth, d, d6 = w_ada.shape
    rows = cc.shape[0]
    nb = d6 // d
    return pl.pallas_call(
        _ada_kernel,
        grid=(depth, nb),
        in_specs=[pl.BlockSpec((rows, d), lambda l, j: (0, 0)),
                  pl.BlockSpec((1, d, d), lambda l, j: (l, 0, j)),
                  pl.BlockSpec((1, 1, d), lambda l, j: (l, 0, j))],
        out_specs=pl.BlockSpec((1, rows, d), lambda l, j: (l, 0, j)),
        out_shape=jax.ShapeDtypeStruct((depth, rows, d6), F32),
        compiler_params=_cparams(("parallel", "parallel")),
        name="ada",
    )(cc, w_ada, b_ada.reshape(depth, 1, d6))


def _inproj_kernel(n_lat_tiles, n_batch, x_ref, ada_ref, g_ref, cos_ref, sin_ref,
                   wda_ref, wml_ref, wmg_ref, wcv_ref, wgt_ref,
                   da_ref, ml_ref, mg_ref, cv_ref, gt_ref):
    b = pl.program_id(0)
    i = pl.program_id(1)
    d = x_ref.shape[-1]
    row = jnp.where(i < n_lat_tiles, b, n_batch)
    shift = ada_ref[pl.ds(row, 1), 0:d]
    scale = ada_ref[pl.ds(row, 1), d:2 * d]
    h = _rms(x_ref[0], g_ref[...]) * (1.0 + scale) + shift
    hb = h.astype(BF16)

    zda = _dot(hb, wda_ref[...])
    cos = cos_ref[...]
    sin = sin_ref[...]
    for j in range(2 * DA_HEADS):
        blk = zda[:, j * LANES:(j + 1) * LANES]
        rot = blk * cos + pltpu.roll(blk, LANES // 2, 1) * sin
        da_ref[0, :, j * LANES:(j + 1) * LANES] = rot.astype(BF16)
    da_ref[0, :, 2 * DA_WIDTH:] = zda[:, 2 * DA_WIDTH:].astype(BF16)

    ml_ref[0] = _dot(hb, wml_ref[...]).astype(BF16)
    mg_ref[0] = _dot(hb, wmg_ref[...])

    zcv = _dot(hb, wcv_ref[...])
    cv_ref[0, :, 0:CV_WIDTH] = zcv[:, 0:CV_WIDTH].astype(BF16)
    cv_ref[0, :, CV_WIDTH:] = (zcv[:, CV_WIDTH:2 * CV_WIDTH] * zcv[:, 2 * CV_WIDTH:]).astype(BF16)

    for j in range(3):
        zg = _dot(hb, wgt_ref[:, j * d:(j + 1) * d])
        gt_ref[0, :, j * d:(j + 1) * d] = _sigmoid(zg).astype(BF16)


def _inproj(x, ada, g, cos, sin, wda, wml, wmg, wcv, wgt, n_lat_tiles):
    bsz, t, d = x.shape
    tm = TOK_TILE
    full = lambda arr: pl.BlockSpec(arr.shape, lambda b, i: (0,) * arr.ndim, pipeline_mode=pl.Buffered(1))
    tok = lambda w: pl.BlockSpec((1, tm, w), lambda b, i: (b, i, 0))
    widths =(wda.shape[1], wml.shape[1], wmg.shape[1], 2 * CV_WIDTH, wgt.shape[1])
    dts = (BF16, BF16, F32, BF16, BF16)
    return pl.pallas_call(
        functools.partial(_inproj_kernel, n_lat_tiles, bsz),
        grid=(bsz, t // tm),
        in_specs=[tok(d), full(ada), full(g),
                  pl.BlockSpec((tm, LANES), lambda b, i: (i, 0)),
                  pl.BlockSpec((tm, LANES), lambda b, i: (i, 0)),
                  full(wda), full(wml), full(wmg), full(wcv), full(wgt)],
        out_specs=[tok(w) for w in widths],
        out_shape=[jax.ShapeDtypeStruct((bsz, t, w), dt) for w, dt in zip(widths, dts)],
        compiler_params=_cparams(("parallel", "parallel")),
        name="inproj",
    )(x, ada, g, cos, sin, wda, wml, wmg, wcv, wgt)


def _attn_kernel(lam_init, q_ref, k_ref, v_ref, lam_ref, g_ref, *rest):
    o_ref, va_ref = rest[-2:]
    i = pl.program_id(2)
    lv = lam_ref[...]
    lam = (jnp.exp(jnp.sum(lv[0:1] * lv[1:2], axis=-1, keepdims=True))
           - jnp.exp(jnp.sum(lv[2:3] * lv[3:4], axis=-1, keepdims=True)) + lam_init)
    lane = lax.broadcasted_iota(jnp.int32, (1, LANES), 1)
    map0 = (lane % (LANES // 2)) < (DA_QK // 2)

    @pl.when(i == 0)
    def _():
        va_ref[:, 0:LANES] = v_ref[0]
        va_ref[:, LANES:] = jnp.ones((va_ref.shape[0], LANES), BF16)

    k = k_ref[0]
    va = va_ref[...]
    q = q_ref[0].astype(F32) * (DA_QK ** -0.5)
    q0 = jnp.where(map0, q, 0.0).astype(BF16)
    q1 = jnp.where(map0, 0.0, q).astype(BF16)
    s0 = _dot_nt(q0, k)
    s1 = _dot_nt(q1, k)
    e0 = jnp.exp((s0 - jnp.max(s0, axis=-1, keepdims=True)).astype(BF16))
    e1 = jnp.exp((s1 - jnp.max(s1, axis=-1, keepdims=True)).astype(BF16))
    o0 = _dot(e0, va)
    o1 = _dot(e1, va)
    o = o0[:, 0:LANES] / o0[:, LANES:] - lam * (o1[:, 0:LANES] / o1[:, LANES:])
    o_ref[0] = (_rms(o, g_ref[...]) * (1.0 - lam_init)).astype(o_ref.dtype)


def _attention(da, lam_p, subln_g, lam_init, tq, q_row0, n_q, kv_row0, kv_rows, prev=None):
    bsz, t, _ = da.shape
    q0 = q_row0 // tq
    kv0 = kv_row0 // kv_rows
    in_specs = [pl.BlockSpec((1, tq, LANES), lambda b, h, i: (b, q0 + i, h)),
                pl.BlockSpec((1, kv_rows, LANES), lambda b, h, i: (b, kv0, DA_HEADS + h)),
                pl.BlockSpec((1, kv_rows, LANES), lambda b, h, i: (b, kv0, 2 * DA_HEADS + h)),
                pl.BlockSpec(lam_p.shape, lambda b, h, i: (0, 0)),
                pl.BlockSpec((1, LANES), lambda b, h, i: (0, 0))]
    args = [da, da, da, lam_p, subln_g.reshape(1, LANES)]
    aliases = {}
    if prev is not None:
        in_specs.append(pl.BlockSpec(memory_space=pl.ANY))
        args.append(prev)
        aliases = {len(args) - 1: 0}
    return pl.pallas_call(
        functools.partial(_attn_kernel, lam_init),
        grid=(bsz, DA_HEADS, n_q),
        in_specs=in_specs,
        out_specs=pl.BlockSpec((1, tq, LANES), lambda b, h, i: (b, q0 + i, h)),
        out_shape=jax.ShapeDtypeStruct((bsz, t, DA_WIDTH), BF16),
        scratch_shapes=[pltpu.VMEM((kv_rows, 2 * LANES), BF16)],
        input_output_aliases=aliases,
        compiler_params=_cparams(("parallel", "parallel", "arbitrary")),
        name="diffattn",
    )(*args)


def _mlstm_kernel(ncl, ncc, ml_ref, mg_ref, cw_ref, gb_ref, ng_ref, o_ref,
                  qs_ref, ks_ref, hf_ref, hb_ref, c_ref, n_ref, m_ref):
    L = ML_CHUNK
    nc = ncl + ncc
    t_all = nc * L
    seq = ncl * L
    w = ML_WIDTH

    trow = lax.broadcasted_iota(jnp.int32, (t_all, 1), 0)
    first = (trow == 0) | (trow == seq)
    last = (trow == seq - 1) | (trow == t_all - 1)
    for j in range(2 * ML_HEADS):
        x = ml_ref[0, :, j * LANES:(j + 1) * LANES].astype(F32)
        xm = jnp.where(first, 0.0, pltpu.roll(x, 1, 0))
        xp = jnp.where(last, 0.0, pltpu.roll(x, t_all - 1, 0))
        cw = cw_ref[:, j * LANES:(j + 1) * LANES]
        y = xm * cw[0:1] + x * cw[1:2] + xp * cw[2:3]
        y = y * _sigmoid(y)
        if j < ML_HEADS:
            qs_ref[:, j * LANES:(j + 1) * LANES] = y.astype(BF16)
        else:
            jj = j - ML_HEADS
            ks_ref[:, jj * LANES:(jj + 1) * LANES] = (y * (ML_DH ** -0.5)).astype(BF16)

    c_ref[...] = jnp.zeros_like(c_ref)
    n_ref[...] = jnp.zeros_like(n_ref)
    m_ref[...] = jnp.zeros_like(m_ref)

    r_i = lax.broadcasted_iota(jnp.int32, (L, L), 0)
    c_i = lax.broadcasted_iota(jnp.int32, (L, L), 1)
    lower = r_i >= c_i
    upper = r_i <= c_i
    lower_f = lower.astype(F32)
    upper_f = upper.astype(F32)

    def step(it, carry):
        for dr in range(2):
            if dr == 0:
                ch = (it + ncl) % nc
                seen, tri_col, tri_row, end = lower, lower_f, upper_f, L - 1
                h_ref = hf_ref
            else:
                ch = nc - 1 - it
                seen, tri_col, tri_row, end = upper, upper_f, lower_f, 0
                h_ref = hb_ref
            r0 = pl.multiple_of(ch * L, L)
            g = mg_ref[0, pl.ds(r0, L), :] + gb_ref[...]
            gt = g.T
            bcol_all = jnp.dot(tri_col, _log_sigmoid(g), preferred_element_type=F32, precision=HIGHEST)
            brow_all = jnp.dot(_log_sigmoid(gt), tri_row, preferred_element_type=F32, precision=HIGHEST)
            for h in range(ML_HEADS):
                idx = dr * ML_HEADS + h
                icol = (2 * dr) * ML_HEADS + h
                fcol = (2 * dr + 1) * ML_HEADS + h
                q = qs_ref[pl.ds(r0, L), h * LANES:(h + 1) * LANES]
                k = ks_ref[pl.ds(r0, L), h * LANES:(h + 1) * LANES]
                v = ml_ref[0, pl.ds(r0, L), 2 * w + h * LANES:2 * w + (h + 1) * LANES]
                b_col = bcol_all[:, fcol:fcol + 1]
                ig_col = g[:, icol:icol + 1]
                b_row = brow_all[fcol:fcol + 1, :]
                ig_row = gt[icol:icol + 1, :]
                m_prev = m_ref[idx][:, 0:1]
                cmat = c_ref[idx]
                nrow = n_ref[idx]

                log_d = jnp.where(seen, b_col - b_row + ig_row, NEG_BIG)
                m_path = b_col + m_prev
                m_t = jnp.maximum(m_path, jnp.max(log_d, axis=-1, keepdims=True))
                dmat = jnp.exp(log_d - m_t)
                carry_w = jnp.exp(m_path - m_t)
                wmat = _dot_nt(q, k) * dmat
                num = _dot(wmat.astype(BF16), v) + carry_w * _dot(q, cmat.astype(BF16))
                den = (jnp.sum(wmat, axis=-1, keepdims=True)
                       + carry_w * jnp.sum(q.astype(F32) * nrow, axis=-1, keepdims=True))
                hval = num / jnp.maximum(jnp.abs(den), jnp.exp(-m_t))
                h_ref[pl.ds(r0, L), h * LANES:(h + 1) * LANES] = hval

                b_end = b_col[end:end + 1, :]
                log_w = b_end - b_col + ig_col
                m_new = jnp.maximum(b_end + m_prev, jnp.max(log_w, axis=0, keepdims=True))
                wk = jnp.exp(log_w - m_new)
                decay = jnp.exp(b_end + m_prev - m_new)
                kw = k.astype(F32) * wk
                c_ref[idx] = decay * cmat + _dot_tn(kw.astype(BF16), v)
                n_ref[idx] = decay * nrow + jnp.sum(kw, axis=0, keepdims=True)
                m_ref[idx] = jnp.broadcast_to(m_new, (1, LANES))
        return carry

    lax.fori_loop(0, nc, step, 0)

    for h in range(ML_HEADS):
        sl = slice(h * LANES, (h + 1) * LANES)
        hs = hf_ref[:, sl] + hb_ref[:, sl]
        y = _rms(hs, ng_ref[:, sl])
        og = ml_ref[0, :, 3 * w + h * LANES:3 * w + (h + 1) * LANES].astype(F32)
        o_ref[0, :, sl] = (y * _sigmoid(og)).astype(o_ref.dtype)


def _mlstm(ml, mg, conv_w, gate_b_row, norm_g, ncl, ncc):
    bsz, t, wtot = ml.shape
    nstate = 2 * ML_HEADS
    return pl.pallas_call(
        functools.partial(_mlstm_kernel, ncl, ncc),
        grid=(bsz,),
        in_specs=[pl.BlockSpec((1, t, wtot), lambda b: (b, 0, 0)),
                  pl.BlockSpec((1, t, LANES), lambda b: (b, 0, 0)),
                  pl.BlockSpec(conv_w.shape, lambda b: (0, 0)),
                  pl.BlockSpec((1, LANES), lambda b: (0, 0)),
                  pl.BlockSpec((1, ML_WIDTH), lambda b: (0, 0))],
        out_specs=pl.BlockSpec((1, t, ML_WIDTH), lambda b: (b, 0, 0)),
        out_shape=jax.ShapeDtypeStruct((bsz, t, ML_WIDTH), BF16),
        scratch_shapes=[pltpu.VMEM((t, ML_WIDTH), BF16), pltpu.VMEM((t, ML_WIDTH), BF16),
                        pltpu.VMEM((t, ML_WIDTH), F32), pltpu.VMEM((t, ML_WIDTH), F32),
                        pltpu.VMEM((nstate, ML_DH, ML_DH), F32),
                        pltpu.VMEM((nstate, 1, ML_DH), F32),
                        pltpu.VMEM((nstate, 1, LANES), F32)],
        compiler_params=_cparams(("parallel",)),
        name="mlstm",
    )(ml, mg, conv_w, gate_b_row, norm_g.reshape(1, ML_WIDTH))


def _merge_kernel(n_lat_tiles, n_batch, x_ref, ada_ref, g2_ref, yda_ref, yml_ref, cv_ref, pprev_ref, pnext_ref,
                  gt_ref, cw_ref, wda_ref, wml_ref, wcv_ref, wo_ref, x1_ref, h2_ref):
    b = pl.program_id(0)
    i = pl.program_id(1)
    n_tiles = pl.num_programs(1)
    d = x_ref.shape[-1]
    tm = x_ref.shape[1]
    row = jnp.where(i < n_lat_tiles, b, n_batch)

    bg = cv_ref[0, :, 0:CV_WIDTH].astype(F32)
    p = cv_ref[0, :, CV_WIDTH:].astype(F32)
    seq_start = (i == 0) | (i == n_lat_tiles)
    seq_end = (i == n_lat_tiles - 1) | (i == n_tiles - 1)
    prev_row = jnp.where(seq_start, 0.0, pprev_ref[0, HALO - 1:HALO, :].astype(F32))
    next_row = jnp.where(seq_end, 0.0, pnext_ref[0, 0:1, :].astype(F32))
    trow = lax.broadcasted_iota(jnp.int32, (tm, 1), 0)
    pm = jnp.where(trow == 0, prev_row, pltpu.roll(p, 1, 0))
    pp = jnp.where(trow == tm - 1, next_row, pltpu.roll(p, tm - 1, 0))
    cw = cw_ref[...]
    ycv = bg * (pm * cw[0:1] + p * cw[1:2] + pp * cw[2:3])

    m = (gt_ref[0, :, 0:d].astype(F32) * _dot(yda_ref[0], wda_ref[...])
         + gt_ref[0, :, d:2 * d].astype(F32) * _dot(yml_ref[0], wml_ref[...])
         + gt_ref[0, :, 2 * d:].astype(F32) * _dot(ycv.astype(BF16), wcv_ref[...]))
    out = _dot(m.astype(BF16), wo_ref[...])
    g1 = ada_ref[pl.ds(row, 1), 2 * d:3 * d]
    x1 = x_ref[0] + g1 * out
    x1_ref[0] = x1
    sh2 = ada_ref[pl.ds(row, 1), 3 * d:4 * d]
    sc2 = ada_ref[pl.ds(row, 1), 4 * d:5 * d]
    h2_ref[0] = (_rms(x1, g2_ref[...]) * (1.0 + sc2) + sh2).astype(BF16)


def _merge(x, ada, g2, yda, yml, cv, gt, cv_w, wda, wml, wcv, wo, n_lat_tiles, n_tiles):
    bsz, t, d = x.shape
    tm = TOK_TILE
    r = tm // HALO
    nrb = t // HALO
    full = lambda arr: pl.BlockSpec(arr.shape, lambda b, i: (0,) * arr.ndim, pipeline_mode=pl.Buffered(1))
    tok = lambda w: pl.BlockSpec((1, tm, w), lambda b, i: (b, i, 0))
    return pl.pallas_call(
        functools.partial(_merge_kernel, n_lat_tiles, bsz),
        grid=(bsz, n_tiles),
        in_specs=[tok(d), full(ada), full(g2), tok(DA_WIDTH), tok(ML_WIDTH), tok(2 * CV_WIDTH),
                  pl.BlockSpec((1, HALO, CV_WIDTH), lambda b, i: (b, jnp.maximum(i * r - 1, 0), 1)),
                  pl.BlockSpec((1, HALO, CV_WIDTH), lambda b, i: (b, jnp.minimum((i + 1) * r, nrb - 1), 1)),
                  tok(3 * d), full(cv_w), full(wda), full(wml), full(wcv), full(wo)],
        out_specs=[tok(d), tok(d)],
        out_shape=[jax.ShapeDtypeStruct((bsz, n_tiles * tm, d), F32),
                   jax.ShapeDtypeStruct((bsz, n_tiles * tm, d), BF16)],
        compiler_params=_cparams(("parallel", "parallel")),
        name="merge",
    )(x, ada, g2, yda, yml, cv, cv, cv, gt, cv_w, wda, wml, wcv, wo)


def _cmpx(v, i, j):
    a, b = v[i], v[j]
    v[i] = jnp.maximum(a, b)
    v[j] = jnp.minimum(a, b)


def _bitonic_merge_desc(v):
    n = len(v)
    j = n // 2
    while j >= 1:
        for i in range(n):
            l = i ^ j
            if l > i:
                _cmpx(v, i, l)
        j //= 2


def _bitonic_sort_desc(v):
    n = len(v)
    k = 2
    while k <= n:
        j = k // 2
        while j >= 1:
            for i in range(n):
                l = i ^ j
                if l > i:
                    if (i & k) == 0:
                        _cmpx(v, i, l)
                    else:
                        _cmpx(v, l, i)
            j //= 2
        k *= 2


def _merge_top(a, b):
    n = len(a)
    c = [jnp.maximum(a[r], b[n - 1 - r]) for r in range(n)]
    _bitonic_merge_desc(c)
    return c


def _top16_sorted(s):
    nk, tm = s.shape
    k = PEER_TOPK
    groups = nk // k
    assert groups == SUBLANES
    a = s.reshape(k, SUBLANES, tm)
    v = [a[r] for r in range(k)]
    _bitonic_sort_desc(v)
    sh = SUBLANES // 2
    while sh >= 1:
        v = _merge_top(v, [pltpu.roll(x, sh, 0) for x in v])
        sh //= 2
    return v


def _erf(x):
    return lax.erf(x)


def _peer_kernel(tiles_per_batch, n_lat_tiles, n_batch, n_i1, final, x_ref, h_ref, ada_ref, fg_ref, wq_ref, keys_ref,
                 u_ref, vt_ref, o_ref, s1_ref, s2_ref, f1_ref, cnt_ref, e2_ref, rk2_ref, t1_ref, t2_ref, cr_ref, rz_ref,
                 at0_ref, at1_ref, w0_ref, w1_ref, acc_ref, ht_ref):
    t = pl.program_id(0)
    s = pl.program_id(1)
    n_steps = pl.num_programs(1)
    n_chunks = (n_steps - 1) * PEER_SUBSTEPS
    d = x_ref.shape[-1]
    tm = x_ref.shape[0]
    nk = keys_ref.shape[2]
    dsub = keys_ref.shape[3]
    k = PEER_TOPK
    sort_w = PEER_SORT_W
    at_refs = (at0_ref, at1_ref)
    w_refs = (w0_ref, w1_ref)
    zero_b = jnp.zeros((), BF16)

    @pl.when(s == 0)
    def _():
        hb = h_ref[...]
        for half in range(tm // TOK_TILE):
            cs = slice(half * TOK_TILE, (half + 1) * TOK_TILE)
            ht_ref[:, cs] = hb[cs, :].astype(F32).T.astype(BF16)
        q = _dot(hb, wq_ref[...]).astype(BF16)
        for h in range(PEER_HEADS):
            for p in range(2):
                col = (h * 2 + p) * dsub
                (s1_ref if p == 0 else s2_ref)[h] = _dot_nt(keys_ref[h, p], q[:, col:col + dsub])
        neg = jnp.full((SUBLANES, sort_w), -jnp.inf, F32)
        for blk in range(tm // sort_w):
            ls = slice(blk * sort_w, (blk + 1) * sort_w)

            def sort_head(h, carry):
                for sref, tref in ((s1_ref, t1_ref), (s2_ref, t2_ref)):
                    v = _top16_sorted(sref[h, :, ls])
                    for r in range(k):
                        tref[blk, r, pl.ds(h, 1), :] = v[r][0:1, :]
                return carry

            lax.fori_loop(0, PEER_HEADS, sort_head, 0)
            t1 = [t1_ref[blk, r] for r in range(k)]
            t2 = [t2_ref[blk, r] for r in range(k)]
            rows = [[t1[r1] + t2[r2] for r2 in range(k // (r1 + 1))] for r1 in range(k)]
            best = rows[0]
            rest = [x for rw in rows[1:] for x in rw]
            while rest:
                grp, rest = rest[:k], rest[k:]
                grp = grp + [neg] * (k - len(grp))
                _bitonic_sort_desc(grp)
                best = _merge_top(best, grp)
            mx = best[0]
            z = jnp.exp(best[0] - mx)
            for r in range(1, k):
                z = z + jnp.exp(best[r] - mx)
            rz = 1.0 / z
            tau = best[k - 1]
            for r1 in range(k):
                cnt = jnp.where(rows[r1][0] >= tau, 1.0, 0.0)
                for r2 in range(1, k // (r1 + 1)):
                    cnt = cnt + jnp.where(rows[r1][r2] >= tau, 1.0, 0.0)
                cr_ref[blk, r1] = cnt
            rz_ref[blk] = rz

            def finish_head(h, carry):
                hs = pl.ds(h, 1)
                s1h = s1_ref[h, :, ls]
                s2h = s2_ref[h, :, ls]
                cn = jnp.zeros_like(s1h)
                rk = jnp.full_like(s2h, float(k))
                for r in reversed(range(k)):
                    cn = jnp.where(s1h == t1_ref[blk, r, hs, :], cr_ref[blk, r, hs, :], cn)
                    rk = jnp.where(s2h >= t2_ref[blk, r, hs, :], float(r), rk)
                cnt_ref[h, :, ls] = cn
                rk2_ref[h, :, ls] = rk.astype(BF16)
                f1_ref[h, :, ls] = jnp.exp(s1h - t1_ref[blk, 0, hs, :]) * rz_ref[blk, hs, :]
                e2_ref[h, :, ls] = jnp.exp(s2h - t2_ref[blk, 0, hs, :]).astype(BF16)
                return carry

            lax.fori_loop(0, PEER_HEADS, finish_head, 0)
        acc_ref[...] = jnp.zeros_like(acc_ref)

    ec = n_i1 * nk
    rb_rows = nk // 2
    n_grp = n_i1 // PEER_GROUP
    g_rows = ec // n_grp
    gpm = n_grp // PEER_MM_SLICES
    d_rows = d // PEER_MM_SLICES
    e_rows = ec // PEER_MM_SLICES

    def gate_group(grp, at_oth, w_oth, cb):
        g_off = pl.multiple_of(grp * g_rows, g_rows)
        i1s = [cb * n_i1 + grp * PEER_GROUP + jj for jj in range(PEER_GROUP)]
        cnrows = [[cnt_ref[h, pl.ds(i1, 1), :].astype(BF16) for i1 in i1s] for h in range(PEER_HEADS)]
        f1rows = [[f1_ref[h, pl.ds(i1, 1), :].astype(BF16) for i1 in i1s] for h in range(PEER_HEADS)]
        for tb in range(tm // LANES):
            ts = slice(tb * LANES, (tb + 1) * LANES)
            for rb in range(nk // rb_rows):
                rs = slice(rb * rb_rows, (rb + 1) * rb_rows)
                gs = [jnp.zeros((rb_rows, LANES), BF16) for _ in range(PEER_GROUP)]
                for h in range(PEER_HEADS):
                    rk2 = rk2_ref[h, rs, ts]
                    e2b = e2_ref[h, rs, ts]
                    for jj in range(PEER_GROUP):
                        cnr = cnrows[h][jj][:, ts]
                        f1r = f1rows[h][jj][:, ts]
                        gs[jj] = gs[jj] + jnp.where(rk2 < cnr, e2b * f1r, zero_b)
                for jj in range(PEER_GROUP):
                    r0 = pl.multiple_of(g_off + jj * nk + rb * rb_rows, rb_rows)
                    a = at_oth[pl.ds(r0, rb_rows), ts].astype(BF16)
                    act = (0.5 * a) * (1.0 + _erf(a * (2.0 ** -0.5)))
                    w_oth[pl.ds(r0, rb_rows), ts] = act * gs[jj]

    def substeps(do1, do2, do3):
        for sub in range(PEER_SUBSTEPS):
            at_cur, at_oth = at_refs[sub], at_refs[1 - sub]
            w_cur, w_oth = w_refs[sub], w_refs[1 - sub]
            cb = s * PEER_SUBSTEPS + sub - 1

            def mm_slice(ms, carry, sub=sub, at_cur=at_cur, at_oth=at_oth, w_cur=w_cur, w_oth=w_oth, cb=cb):
                if do1[sub]:
                    e_off = pl.multiple_of(ms * e_rows, e_rows)
                    u_off = pl.multiple_of((sub * ec + e_off) // 2, e_rows // 2)
                    u_rows = pltpu.bitcast(u_ref[pl.ds(u_off, e_rows // 2), :], BF16)
                    at_cur[pl.ds(e_off, e_rows), :] = _dot(u_rows, ht_ref[...])
                if do3[sub]:
                    d_off = pl.multiple_of(ms * d_rows, d_rows)
                    v_off = pl.multiple_of(d_off // 2, d_rows // 2)
                    v_rows = pltpu.bitcast(vt_ref[pl.ds(v_off, d_rows // 2), sub * ec:(sub + 1) * ec], BF16)
                    acc_ref[pl.ds(d_off, d_rows), :] += _dot(v_rows, w_cur[...])
                if do2[sub]:
                    for gi in range(gpm):
                        gate_group(ms * gpm + gi, at_oth, w_oth, cb)
                return carry

            lax.fori_loop(0, PEER_MM_SLICES, mm_slice, 0)

    first = s == 0
    last = s == n_steps - 1

    @pl.when(first)
    def _():
        substeps(do1=(True, True), do2=(False, True), do3=(False, False))

    @pl.when(jnp.logical_not(first | last))
    def _():
        substeps(do1=(True, True), do2=(True, True), do3=(True, True))

    @pl.when(last)
    def _():
        substeps(do1=(False, False), do2=(True, False), do3=(True, True))

    @pl.when(s == n_steps - 1)
    def _():
        for half in range(tm // TOK_TILE):
            g256 = t * (tm // TOK_TILE) + half
            row = jnp.where(g256 % tiles_per_batch < n_lat_tiles, g256 // tiles_per_batch, n_batch)
            g2 = ada_ref[pl.ds(row, 1), 5 * d:6 * d]
            rs = slice(half * TOK_TILE, (half + 1) * TOK_TILE)
            x2 = x_ref[rs, :] + g2 * acc_ref[:, rs].T
            if final:
                x2 = _rms(x2, fg_ref[...])
            o_ref[rs, :] = x2


def _peer(x1, h2, ada, final_g, wq, keys, u, vt, layer, tiles_per_batch, n_lat_tiles, n_batch, final):
    rows, d = x1.shape
    tm = PEER_TILE
    nk = keys.shape[2]
    n_i1 = PEER_CHUNK_I1
    ec = n_i1 * nk
    n_blocks = 2 * u.shape[1] // (PEER_SUBSTEPS * ec)
    n_sort = tm // PEER_SORT_W
    assert PEER_HEADS == SUBLANES
    full = lambda arr: pl.BlockSpec(arr.shape, lambda t, s: (0,) * arr.ndim, pipeline_mode=pl.Buffered(1))
    tok = pl.BlockSpec((tm, d), lambda t, s: (t, 0))
    return pl.pallas_call(
        functools.partial(_peer_kernel, tiles_per_batch, n_lat_tiles, n_batch, n_i1, final),
        grid=(rows // tm, n_blocks + 1),
        in_specs=[tok, tok, full(ada), full(final_g), full(wq), full(keys),
                  pl.BlockSpec((None, PEER_SUBSTEPS * ec // 2, d),
                               lambda t, s: (layer, jnp.minimum(s, n_blocks - 1), 0)),
                  pl.BlockSpec((None, d // 2, PEER_SUBSTEPS * ec), lambda t, s: (layer, 0, jnp.maximum(s - 1, 0)))],
        out_specs=tok,
        out_shape=jax.ShapeDtypeStruct((rows, d), F32),
        scratch_shapes=[pltpu.VMEM((PEER_HEADS, nk, tm), F32), pltpu.VMEM((PEER_HEADS, nk, tm), F32),
                        pltpu.VMEM((PEER_HEADS, nk, tm), F32), pltpu.VMEM((PEER_HEADS, nk, tm), F32),
                        pltpu.VMEM((PEER_HEADS, nk, tm), BF16), pltpu.VMEM((PEER_HEADS, nk, tm), BF16),
                        pltpu.VMEM((n_sort, PEER_TOPK, PEER_HEADS, PEER_SORT_W), F32),
                        pltpu.VMEM((n_sort, PEER_TOPK, PEER_HEADS, PEER_SORT_W), F32),
                        pltpu.VMEM((n_sort, PEER_TOPK, PEER_HEADS, PEER_SORT_W), F32),
                        pltpu.VMEM((n_sort, PEER_HEADS, PEER_SORT_W), F32),
                        pltpu.VMEM((ec, tm), F32), pltpu.VMEM((ec, tm), F32),
                        pltpu.VMEM((ec, tm), BF16), pltpu.VMEM((ec, tm), BF16),
                        pltpu.VMEM((d, tm), F32), pltpu.VMEM((d, tm), BF16)],
        compiler_params=_cparams(("parallel", "arbitrary")),
        name="peer",
    )(x1, h2, ada, final_g, wq, keys, u, vt)


def _rope_tables(seq, t):
    quarter = DA_QK // 4
    inv = ROPE_BASE ** (-np.arange(quarter, dtype=np.float32) / quarter)
    pos = np.arange(seq)
    ang_r = (pos // GRID_W).astype(np.float32)[:, None] * inv[None, :]
    ang_c = (pos % GRID_W).astype(np.float32)[:, None] * inv[None, :]
    half = np.concatenate([ang_r, ang_c, ang_r, ang_c], axis=1)
    cos = np.ones((t, LANES), np.float32)
    sin = np.zeros((t, LANES), np.float32)
    cos[:seq] = np.concatenate([np.cos(half), np.cos(half)], axis=1)
    sin[:seq] = np.concatenate([-np.sin(half), np.sin(half)], axis=1)
    return jnp.asarray(cos), jnp.asarray(sin)


def _pack_kernel(transpose, x_ref, o_ref):
    x = x_ref[0]
    if transpose:
        x = x.T
    o_ref[0] = pltpu.bitcast(x.astype(BF16), jnp.uint32)


def _pack_table(tab, transpose):
    depth, r, c = tab.shape
    rb = PACK_ROWS
    if transpose:
        in_spec = pl.BlockSpec((1, rb, c), lambda l, j: (l, j, 0))
        out_spec = pl.BlockSpec((1, c // 2, rb), lambda l, j: (l, 0, j))
        out_shape = (depth, c // 2, r)
    else:
        in_spec = pl.BlockSpec((1, rb, c), lambda l, j: (l, j, 0))
        out_spec = pl.BlockSpec((1, rb // 2, c), lambda l, j: (l, j, 0))
        out_shape = (depth, r // 2, c)
    return pl.pallas_call(
        functools.partial(_pack_kernel, transpose),
        grid=(depth, r // rb),
        in_specs=[in_spec],
        out_specs=out_spec,
        out_shape=jax.ShapeDtypeStruct(out_shape, jnp.uint32),
        compiler_params=_cparams(("parallel", "parallel")),
        name="pack_t" if transpose else "pack",
    )(tab)


def kernel(x, c, ctx, c_ctx, w_ada, b_ada, norm1_g, norm2_g, w_in, ml_gate_b, ml_conv_w, ml_norm_g, cv_conv_w,
           da_lam, da_subln_g, w_da, w_ml, w_cv, w_o, peer_wq, peer_keys, peer_u, peer_v, final_g):
    bsz, seq, d = x.shape
    n_ctx = ctx.shape[1]
    t = seq + n_ctx
    depth = w_ada.shape[0]
    tm = TOK_TILE
    assert seq % tm == 0 and n_ctx % tm == 0 and seq % ML_CHUNK == 0 and n_ctx % ML_CHUNK == 0
    assert (bsz * t) % PEER_TILE == 0 and (bsz * seq) % PEER_TILE == 0
    assert seq % ATTN_TILE == 0 and seq % n_ctx == 0
    n_lat_tiles = seq // tm
    n_tiles = t // tm

    rows = -(-(bsz + 1) // SUBLANES) * SUBLANES
    cc = jnp.zeros((rows, d), F32).at[:bsz].set(c).at[bsz].set(c_ctx)
    ada = _ada(cc, w_ada, b_ada)

    o = np.cumsum([0, DA_WIDTH, DA_WIDTH, DA_WIDTH, 2 * ML_WIDTH, ML_WIDTH, ML_WIDTH, 4 * ML_HEADS,
                   CV_WIDTH, CV_WIDTH, CV_WIDTH, 3 * d])
    quarter = DA_QK // 4
    wqk = w_in[:, :, o[0]:o[2]].reshape(depth, d, 2 * DA_HEADS, 2, 2, 2, quarter)
    wqk = wqk.transpose(0, 1, 2, 5, 3, 4, 6).reshape(depth, d, 2 * DA_WIDTH)
    wda_in = jnp.concatenate([wqk, w_in[:, :, o[2]:o[3]]], axis=-1).astype(BF16)
    wml_in = w_in[:, :, o[3]:o[6]].astype(BF16)
    wmg_in = jnp.pad(w_in[:, :, o[6]:o[7]], ((0, 0), (0, 0), (0, LANES - 4 * ML_HEADS))).astype(BF16)
    wcv_in = w_in[:, :, o[7]:o[10]].astype(BF16)
    wgt_in = w_in[:, :, o[10]:o[11]].astype(BF16)
    gate_b = jnp.pad(ml_gate_b.reshape(depth, 1, 4 * ML_HEADS), ((0, 0), (0, 0), (0, LANES - 4 * ML_HEADS)))
    lam_p = da_lam
    w_da_b, w_ml_b, w_cv_b, w_o_b = (a.astype(BF16) for a in (w_da, w_ml, w_cv, w_o))
    wq_b = peer_wq.astype(BF16)
    keys_b = peer_keys.astype(BF16)
    u_b = _pack_table(peer_u, transpose=False)
    vt_b = _pack_table(peer_v, transpose=True)
    cos, sin = _rope_tables(seq, t)

    xs = jnp.concatenate([x, ctx], axis=1)
    for l in range(depth):
        need_ctx = l < depth - 1
        lam_init = 0.8 - 0.6 * math.exp(-0.3 * l)
        da, ml, mg, cv, gt = _inproj(xs, ada[l], norm1_g[l].reshape(1, d), cos, sin,
                                     wda_in[l], wml_in[l], wmg_in[l], wcv_in[l], wgt_in[l], n_lat_tiles)
        yda = _attention(da, lam_p[l], da_subln_g[l], lam_init, ATTN_TILE, 0, seq // ATTN_TILE, 0, t)
        if need_ctx:
            yda = _attention(da, lam_p[l], da_subln_g[l], lam_init, n_ctx, seq, 1, seq, n_ctx, prev=yda)
        yml = _mlstm(ml, mg, ml_conv_w[l], gate_b[l], ml_norm_g[l], seq // ML_CHUNK, n_ctx // ML_CHUNK)
        n_out_tiles = n_tiles if need_ctx else n_lat_tiles
        x1, h2 = _merge(xs, ada[l], norm2_g[l].reshape(1, d), yda, yml, cv, gt, cv_conv_w[l],
                        w_da_b[l], w_ml_b[l], w_cv_b[l], w_o_b[l], n_lat_tiles, n_out_tiles)
        t_out = n_out_tiles * tm
        xs = _peer(x1.reshape(bsz * t_out, d), h2.reshape(bsz * t_out, d), ada[l], final_g.reshape(1, d),
                   wq_b[l], keys_b[l], u_b, vt_b, l, n_out_tiles, n_lat_tiles, bsz, final=not need_ctx)
        xs = xs.reshape(bsz, t_out, d)
    return xs
```

```python
import functools
import math

import numpy as np
import jax
import jax.numpy as jnp
from jax import lax
from jax.experimental import pallas as pl
from jax.experimental.pallas import tpu as pltpu

EPS = 1e-6
GRID_W = 64
DA_HEADS = 4
DA_QK = 64
DA_V = 2 * DA_QK
DA_WIDTH = DA_HEADS * DA_V
ROPE_BASE = 10000.0
ML_HEADS = 4
ML_DH = 128
ML_WIDTH = ML_HEADS * ML_DH
ML_CHUNK = 128
CV_WIDTH = 512
PEER_HEADS = 8
PEER_TOPK = 16
PEER_GROUP = 2

LANES = 128
SUBLANES = 8
ATTN_TILE = 1024
TOK_TILE = 256
HALO = 16
PEER_TILE = 512
PEER_CHUNK_I1 = 8
PEER_SUBSTEPS = 2
PACK_ROWS = 512
PEER_MM_SLICES = 2
PEER_SORT_W = 2 * LANES
VMEM_LIMIT = 56 * 1024 * 1024

NEG_BIG = -1e30
BF16 = jnp.bfloat16
F32 = jnp.float32
HIGHEST = lax.Precision.HIGHEST


def _cparams(sem):
    return pltpu.CompilerParams(dimension_semantics=sem, vmem_limit_bytes=VMEM_LIMIT)


def _dot(a, b):
    return jnp.dot(a, b, preferred_element_type=F32)


def _dot_nt(a, b):
    return lax.dot_general(a, b, (((1,), (1,)), ((), ())), preferred_element_type=F32)


def _dot_tn(a, b):
    return lax.dot_general(a, b, (((0,), (0,)), ((), ())), preferred_element_type=F32)


def _rms(x, g):
    return x * lax.rsqrt(jnp.mean(x * x, axis=-1, keepdims=True) + EPS) * g


def _sigmoid(x):
    return 1.0 / (1.0 + jnp.exp(-x))


def _log_sigmoid(x):
    return jnp.minimum(x, 0.0) - jnp.log1p(jnp.exp(-jnp.abs(x)))


def _ada_kernel(c_ref, w_ref, b_ref, o_ref):
    c = c_ref[...]
    s = c * _sigmoid(c)
    o_ref[0] = jnp.dot(s, w_ref[0], preferred_element_type=F32, precision=HIGHEST) + b_ref[0]


def _ada(cc, w_ada, b_ada):
    depth, d, d6 = w_ada.shape
    rows = cc.shape[0]
    nb = d6 // d
    return pl.pallas_call(
        _ada_kernel,
        grid=(depth, nb),
        in_specs=[pl.BlockSpec((rows, d), lambda l, j: (0, 0)),
                  pl.BlockSpec((1, d, d), lambda l, j: (l, 0, j)),
                  pl.BlockSpec((1, 1, d), lambda l, j: (l, 0, j))],
        out_specs=pl.BlockSpec((1, rows, d), lambda l, j: (l, 0, j)),
        out_shape=jax.ShapeDtypeStruct((depth, rows, d6), F32),
        compiler_params=_cparams(("parallel", "parallel")),
        name="ada",
    )(cc, w_ada, b_ada.reshape(depth, 1, d6))


def _inproj_kernel(n_lat_tiles, n_batch, x_ref, ada_ref, g_ref, cos_ref, sin_ref,
                   wda_ref, wml_ref, wmg_ref, wcv_ref, wgt_ref,
                   da_ref, ml_ref, mg_ref, cv_ref, gt_ref):
    b = pl.program_id(0)
    i = pl.program_id(1)
    d = x_ref.shape[-1]
    row = jnp.where(i < n_lat_tiles, b, n_batch)
    shift = ada_ref[pl.ds(row, 1), 0:d]
    scale = ada_ref[pl.ds(row, 1), d:2 * d]
    h = _rms(x_ref[0], g_ref[...]) * (1.0 + scale) + shift
    hb = h.astype(BF16)

    zda = _dot(hb, wda_ref[...])
    cos = cos_ref[...]
    sin = sin_ref[...]
    for j in range(2 * DA_HEADS):
        blk = zda[:, j * LANES:(j + 1) * LANES]
        rot = blk * cos + pltpu.roll(blk, LANES // 2, 1) * sin
        da_ref[0, :, j * LANES:(j + 1) * LANES] = rot.astype(BF16)
    da_ref[0, :, 2 * DA_WIDTH:] = zda[:, 2 * DA_WIDTH:].astype(BF16)

    ml_ref[0] = _dot(hb, wml_ref[...]).astype(BF16)
    mg_ref[0] = _dot(hb, wmg_ref[...])

    zcv = _dot(hb, wcv_ref[...])
    cv_ref[0, :, 0:CV_WIDTH] = zcv[:, 0:CV_WIDTH].astype(BF16)
    cv_ref[0, :, CV_WIDTH:] = (zcv[:, CV_WIDTH:2 * CV_WIDTH] * zcv[:, 2 * CV_WIDTH:]).astype(BF16)

    for j in range(3):
        zg = _dot(hb, wgt_ref[:, j * d:(j + 1) * d])
        gt_ref[0, :, j * d:(j + 1) * d] = _sigmoid(zg).astype(BF16)


def _inproj(x, ada, g, cos, sin, wda, wml, wmg, wcv, wgt, n_lat_tiles):
    bsz, t, d = x.shape
    tm = TOK_TILE
    full = lambda arr: pl.BlockSpec(arr.shape, lambda b, i: (0,) * arr.ndim, pipeline_mode=pl.Buffered(1))
    tok = lambda w: pl.BlockSpec((1, tm, w), lambda b, i: (b, i, 0))
    widths =(wda.shape[1], wml.shape[1], wmg.shape[1], 2 * CV_WIDTH, wgt.shape[1])
    dts = (BF16, BF16, F32, BF16, BF16)
    return pl.pallas_call(
        functools.partial(_inproj_kernel, n_lat_tiles, bsz),
        grid=(bsz, t // tm),
        in_specs=[tok(d), full(ada), full(g),
                  pl.BlockSpec((tm, LANES), lambda b, i: (i, 0)),
                  pl.BlockSpec((tm, LANES), lambda b, i: (i, 0)),
                  full(wda), full(wml), full(wmg), full(wcv), full(wgt)],
        out_specs=[tok(w) for w in widths],
        out_shape=[jax.ShapeDtypeStruct((bsz, t, w), dt) for w, dt in zip(widths, dts)],
        compiler_params=_cparams(("parallel", "parallel")),
        name="inproj",
    )(x, ada, g, cos, sin, wda, wml, wmg, wcv, wgt)


def _attn_kernel(lam_init, q_ref, k_ref, v_ref, lam_ref, g_ref, *rest):
    o_ref, va_ref = rest[-2:]
    i = pl.program_id(2)
    lv = lam_ref[...]
    lam = (jnp.exp(jnp.sum(lv[0:1] * lv[1:2], axis=-1, keepdims=True))
           - jnp.exp(jnp.sum(lv[2:3] * lv[3:4], axis=-1, keepdims=True)) + lam_init)
    lane = lax.broadcasted_iota(jnp.int32, (1, LANES), 1)
    map0 = (lane % (LANES // 2)) < (DA_QK // 2)

    @pl.when(i == 0)
    def _():
        va_ref[:, 0:LANES] = v_ref[0]
        va_ref[:, LANES:] = jnp.ones((va_ref.shape[0], LANES), BF16)

    k = k_ref[0]
    va = va_ref[...]
    q = q_ref[0].astype(F32) * (DA_QK ** -0.5)
    q0 = jnp.where(map0, q, 0.0).astype(BF16)
    q1 = jnp.where(map0, 0.0, q).astype(BF16)
    s0 = _dot_nt(q0, k)
    s1 = _dot_nt(q1, k)
    e0 = jnp.exp((s0 - jnp.max(s0, axis=-1, keepdims=True)).astype(BF16))
    e1 = jnp.exp((s1 - jnp.max(s1, axis=-1, keepdims=True)).astype(BF16))
    o0 = _dot(e0, va)
    o1 = _dot(e1, va)
    o = o0[:, 0:LANES] / o0[:, LANES:] - lam * (o1[:, 0:LANES] / o1[:, LANES:])
    o_ref[0] = (_rms(o, g_ref[...]) * (1.0 - lam_init)).astype(o_ref.dtype)


def _attention(da, lam_p, subln_g, lam_init, tq, q_row0, n_q, kv_row0, kv_rows, prev=None):
    bsz, t, _ = da.shape
    q0 = q_row0 // tq
    kv0 = kv_row0 // kv_rows
    in_specs = [pl.BlockSpec((1, tq, LANES), lambda b, h, i: (b, q0 + i, h)),
                pl.BlockSpec((1, kv_rows, LANES), lambda b, h, i: (b, kv0, DA_HEADS + h)),
                pl.BlockSpec((1, kv_rows, LANES), lambda b, h, i: (b, kv0, 2 * DA_HEADS + h)),
                pl.BlockSpec(lam_p.shape, lambda b, h, i: (0, 0)),
                pl.BlockSpec((1, LANES), lambda b, h, i: (0, 0))]
    args = [da, da, da, lam_p, subln_g.reshape(1, LANES)]
    aliases = {}
    if prev is not None:
        in_specs.append(pl.BlockSpec(memory_space=pl.ANY))
        args.append(prev)
        aliases = {len(args) - 1: 0}
    return pl.pallas_call(
        functools.partial(_attn_kernel, lam_init),
        grid=(bsz, DA_HEADS, n_q),
        in_specs=in_specs,
        out_specs=pl.BlockSpec((1, tq, LANES), lambda b, h, i: (b, q0 + i, h)),
        out_shape=jax.ShapeDtypeStruct((bsz, t, DA_WIDTH), BF16),
        scratch_shapes=[pltpu.VMEM((kv_rows, 2 * LANES), BF16)],
        input_output_aliases=aliases,
        compiler_params=_cparams(("parallel", "parallel", "arbitrary")),
        name="diffattn",
    )(*args)


def _mlstm_kernel(ncl, ncc, ml_ref, mg_ref, cw_ref, gb_ref, ng_ref, o_ref,
                  qs_ref, ks_ref, hf_ref, hb_ref, c_ref, n_ref, m_ref):
    L = ML_CHUNK
    nc = ncl + ncc
    t_all = nc * L
    seq = ncl * L
    w = ML_WIDTH

    trow = lax.broadcasted_iota(jnp.int32, (t_all, 1), 0)
    first = (trow == 0) | (trow == seq)
    last = (trow == seq - 1) | (trow == t_all - 1)
    for j in range(2 * ML_HEADS):
        x = ml_ref[0, :, j * LANES:(j + 1) * LANES].astype(F32)
        xm = jnp.where(first, 0.0, pltpu.roll(x, 1, 0))
        xp = jnp.where(last, 0.0, pltpu.roll(x, t_all - 1, 0))
        cw = cw_ref[:, j * LANES:(j + 1) * LANES]
        y = xm * cw[0:1] + x * cw[1:2] + xp * cw[2:3]
        y = y * _sigmoid(y)
        if j < ML_HEADS:
            qs_ref[:, j * LANES:(j + 1) * LANES] = y.astype(BF16)
        else:
            jj = j - ML_HEADS
            ks_ref[:, jj * LANES:(jj + 1) * LANES] = (y * (ML_DH ** -0.5)).astype(BF16)

    c_ref[...] = jnp.zeros_like(c_ref)
    n_ref[...] = jnp.zeros_like(n_ref)
    m_ref[...] = jnp.zeros_like(m_ref)

    r_i = lax.broadcasted_iota(jnp.int32, (L, L), 0)
    c_i = lax.broadcasted_iota(jnp.int32, (L, L), 1)
    lower = r_i >= c_i
    upper = r_i <= c_i
    lower_f = lower.astype(F32)
    upper_f = upper.astype(F32)
    ones_blk = jnp.ones((L, LANES), BF16)

    def step(it, carry):
        dirs = []
        for dr in range(2):
            if dr == 0:
                ch = (it + ncl) % nc
                seen, tri_col, tri_row, end, h_ref = lower, lower_f, upper_f, L - 1, hf_ref
            else:
                ch = nc - 1 - it
                seen, tri_col, tri_row, end, h_ref = upper, upper_f, lower_f, 0, hb_ref
            r0 = pl.multiple_of(ch * L, L)
            g = mg_ref[0, pl.ds(r0, L), :] + gb_ref[...]
            gt = g.T
            bcol_all = jnp.dot(tri_col, _log_sigmoid(g), preferred_element_type=F32, precision=HIGHEST)
            brow_all = jnp.dot(_log_sigmoid(gt), tri_row, preferred_element_type=F32, precision=HIGHEST)
            dirs.append((seen, end, h_ref, r0, g, gt, bcol_all, brow_all))

        rec = []
        for dr in range(2):
            seen, end, h_ref, r0, g, gt, bcol_all, brow_all = dirs[dr]
            for h in range(ML_HEADS):
                icol = (2 * dr) * ML_HEADS + h
                fcol = (2 * dr + 1) * ML_HEADS + h
                idx = dr * ML_HEADS + h
                rec.append(dict(
                    seen=seen, end=end, h_ref=h_ref, r0=r0, idx=idx, cols=slice(h * LANES, (h + 1) * LANES),
                    q=qs_ref[pl.ds(r0, L), h * LANES:(h + 1) * LANES],
                    k=ks_ref[pl.ds(r0, L), h * LANES:(h + 1) * LANES],
                    v=ml_ref[0, pl.ds(r0, L), 2 * w + h * LANES:2 * w + (h + 1) * LANES],
                    b_col=bcol_all[:, fcol:fcol + 1], ig_col=g[:, icol:icol + 1],
                    b_row=brow_all[fcol:fcol + 1, :], ig_row=gt[icol:icol + 1, :],
                    m_prev=m_ref[idx][:, 0:1], cmat=c_ref[idx], nrow=n_ref[idx]))

        for c in rec:
            c["log_d"] = jnp.where(c["seen"], c["b_col"] - c["b_row"] + c["ig_row"], NEG_BIG)
        for c in rec:
            c["m_path"] = c["b_col"] + c["m_prev"]
            c["m_t"] = jnp.maximum(c["m_path"], jnp.max(c["log_d"], axis=-1, keepdims=True))
        for c in rec:
            c["m_tb"] = jnp.broadcast_to(c["m_t"], (L, L))
            c["dmat"] = jnp.exp(c["log_d"] - c["m_tb"])
        for c in rec:
            c["carry_b"] = jnp.broadcast_to(jnp.exp(c["m_path"] - c["m_t"]), (L, L))
        for c in rec:
            wmat = _dot_nt(c["q"], c["k"]) * c["dmat"]
            wv = _dot(wmat.astype(BF16), jnp.concatenate([c["v"], ones_blk], axis=1))
            qn = _dot_nt(c["q"], jnp.broadcast_to(c["nrow"], (L, LANES)).astype(BF16))
            num = wv[:, 0:LANES] + c["carry_b"] * _dot(c["q"], c["cmat"].astype(BF16))
            den = wv[:, LANES:] + c["carry_b"] * qn
            hval = num / jnp.maximum(jnp.abs(den), jnp.exp(-c["m_tb"]))
            c["h_ref"][pl.ds(c["r0"], L), c["cols"]] = hval
        for c in rec:
            b_end = c["b_col"][c["end"]:c["end"] + 1, :]
            log_w = b_end - c["b_col"] + c["ig_col"]
            m_new = jnp.maximum(b_end + c["m_prev"], jnp.max(log_w, axis=0, keepdims=True))
            c["wk"] = jnp.exp(log_w - m_new)
            c["decay"] = jnp.exp(b_end + c["m_prev"] - m_new)
            c["m_new"] = m_new
        for c in rec:
            kw = c["k"].astype(F32) * c["wk"]
            idx = c["idx"]
            c_ref[idx] = c["decay"] * c["cmat"] + _dot_tn(kw.astype(BF16), c["v"])
            n_ref[idx] = c["decay"] * c["nrow"] + jnp.sum(kw, axis=0, keepdims=True)
            m_ref[idx] = jnp.broadcast_to(c["m_new"], (1, LANES))
        return carry

    lax.fori_loop(0, nc, step, 0)

    for h in range(ML_HEADS):
        sl = slice(h * LANES, (h + 1) * LANES)
        hs = hf_ref[:, sl] + hb_ref[:, sl]
        y = _rms(hs, ng_ref[:, sl])
        og = ml_ref[0, :, 3 * w + h * LANES:3 * w + (h + 1) * LANES].astype(F32)
        o_ref[0, :, sl] = (y * _sigmoid(og)).astype(o_ref.dtype)


def _mlstm(ml, mg, conv_w, gate_b_row, norm_g, ncl, ncc):
    bsz, t, wtot = ml.shape
    nstate = 2 * ML_HEADS
    return pl.pallas_call(
        functools.partial(_mlstm_kernel, ncl, ncc),
        grid=(bsz,),
        in_specs=[pl.BlockSpec((1, t, wtot), lambda b: (b, 0, 0)),
                  pl.BlockSpec((1, t, LANES), lambda b: (b, 0, 0)),
                  pl.BlockSpec(conv_w.shape, lambda b: (0, 0)),
                  pl.BlockSpec((1, LANES), lambda b: (0, 0)),
                  pl.BlockSpec((1, ML_WIDTH), lambda b: (0, 0))],
        out_specs=pl.BlockSpec((1, t, ML_WIDTH), lambda b: (b, 0, 0)),
        out_shape=jax.ShapeDtypeStruct((bsz, t, ML_WIDTH), BF16),
        scratch_shapes=[pltpu.VMEM((t, ML_WIDTH), BF16), pltpu.VMEM((t, ML_WIDTH), BF16),
                        pltpu.VMEM((t, ML_WIDTH), F32), pltpu.VMEM((t, ML_WIDTH), F32),
                        pltpu.VMEM((nstate, ML_DH, ML_DH), F32),
                        pltpu.VMEM((nstate, 1, ML_DH), F32),
                        pltpu.VMEM((nstate, 1, LANES), F32)],
        compiler_params=_cparams(("parallel",)),
        name="mlstm",
    )(ml, mg, conv_w, gate_b_row, norm_g.reshape(1, ML_WIDTH))


def _merge_kernel(n_lat_tiles, n_batch, x_ref, ada_ref, g2_ref, yda_ref, yml_ref, cv_ref, pprev_ref, pnext_ref,
                  gt_ref, cw_ref, wda_ref, wml_ref, wcv_ref, wo_ref, x1_ref, h2_ref):
    b = pl.program_id(0)
    i = pl.program_id(1)
    n_tiles = pl.num_programs(1)
    d = x_ref.shape[-1]
    tm = x_ref.shape[1]
    row = jnp.where(i < n_lat_tiles, b, n_batch)

    bg = cv_ref[0, :, 0:CV_WIDTH].astype(F32)
    p = cv_ref[0, :, CV_WIDTH:].astype(F32)
    seq_start = (i == 0) | (i == n_lat_tiles)
    seq_end = (i == n_lat_tiles - 1) | (i == n_tiles - 1)
    prev_row = jnp.where(seq_start, 0.0, pprev_ref[0, HALO - 1:HALO, :].astype(F32))
    next_row = jnp.where(seq_end, 0.0, pnext_ref[0, 0:1, :].astype(F32))
    trow = lax.broadcasted_iota(jnp.int32, (tm, 1), 0)
    pm = jnp.where(trow == 0, prev_row, pltpu.roll(p, 1, 0))
    pp = jnp.where(trow == tm - 1, next_row, pltpu.roll(p, tm - 1, 0))
    cw = cw_ref[...]
    ycv = bg * (pm * cw[0:1] + p * cw[1:2] + pp * cw[2:3])

    m = (gt_ref[0, :, 0:d].astype(F32) * _dot(yda_ref[0], wda_ref[...])
         + gt_ref[0, :, d:2 * d].astype(F32) * _dot(yml_ref[0], wml_ref[...])
         + gt_ref[0, :, 2 * d:].astype(F32) * _dot(ycv.astype(BF16), wcv_ref[...]))
    out = _dot(m.astype(BF16), wo_ref[...])
    g1 = ada_ref[pl.ds(row, 1), 2 * d:3 * d]
    x1 = x_ref[0] + g1 * out
    x1_ref[0] = x1
    sh2 = ada_ref[pl.ds(row, 1), 3 * d:4 * d]
    sc2 = ada_ref[pl.ds(row, 1), 4 * d:5 * d]
    h2_ref[0] = (_rms(x1, g2_ref[...]) * (1.0 + sc2) + sh2).astype(BF16)


def _merge(x, ada, g2, yda, yml, cv, gt, cv_w, wda, wml, wcv, wo, n_lat_tiles, n_tiles):
    bsz, t, d = x.shape
    tm = TOK_TILE
    r = tm // HALO
    nrb = t // HALO
    full = lambda arr: pl.BlockSpec(arr.shape, lambda b, i: (0,) * arr.ndim, pipeline_mode=pl.Buffered(1))
    tok = lambda w: pl.BlockSpec((1, tm, w), lambda b, i: (b, i, 0))
    return pl.pallas_call(
        functools.partial(_merge_kernel, n_lat_tiles, bsz),
        grid=(bsz, n_tiles),
        in_specs=[tok(d), full(ada), full(g2), tok(DA_WIDTH), tok(ML_WIDTH), tok(2 * CV_WIDTH),
                  pl.BlockSpec((1, HALO, CV_WIDTH), lambda b, i: (b, jnp.maximum(i * r - 1, 0), 1)),
                  pl.BlockSpec((1, HALO, CV_WIDTH), lambda b, i: (b, jnp.minimum((i + 1) * r, nrb - 1), 1)),
                  tok(3 * d), full(cv_w), full(wda), full(wml), full(wcv), full(wo)],
        out_specs=[tok(d), tok(d)],
        out_shape=[jax.ShapeDtypeStruct((bsz, n_tiles * tm, d), F32),
                   jax.ShapeDtypeStruct((bsz, n_tiles * tm, d), BF16)],
        compiler_params=_cparams(("parallel", "parallel")),
        name="merge",
    )(x, ada, g2, yda, yml, cv, cv, cv, gt, cv_w, wda, wml, wcv, wo)


def _cmpx(v, i, j):
    a, b = v[i], v[j]
    v[i] = jnp.maximum(a, b)
    v[j] = jnp.minimum(a, b)


def _bitonic_merge_desc(v):
    n = len(v)
    j = n // 2
    while j >= 1:
        for i in range(n):
            l = i ^ j
            if l > i:
                _cmpx(v, i, l)
        j //= 2


def _bitonic_sort_desc(v):
    n = len(v)
    k = 2
    while k <= n:
        j = k // 2
        while j >= 1:
            for i in range(n):
                l = i ^ j
                if l > i:
                    if (i & k) == 0:
                        _cmpx(v, i, l)
                    else:
                        _cmpx(v, l, i)
            j //= 2
        k *= 2


def _merge_top(a, b):
    n = len(a)
    c = [jnp.maximum(a[r], b[n - 1 - r]) for r in range(n)]
    _bitonic_merge_desc(c)
    return c


def _top16_sorted(s):
    nk, tm = s.shape
    k = PEER_TOPK
    groups = nk // k
    assert groups == SUBLANES
    a = s.reshape(k, SUBLANES, tm)
    v = [a[r] for r in range(k)]
    _bitonic_sort_desc(v)
    sh = SUBLANES // 2
    while sh >= 1:
        v = _merge_top(v, [pltpu.roll(x, sh, 0) for x in v])
        sh //= 2
    return v


def _erf(x):
    return lax.erf(x)


def _peer_kernel(tiles_per_batch, n_lat_tiles, n_batch, n_i1, final, x_ref, h_ref, ada_ref, fg_ref, wq_ref, keys_ref,
                 u_ref, vt_ref, o_ref, s1_ref, s2_ref, f1_ref, cnt_ref, e2_ref, rk2_ref, t1_ref, t2_ref, cr_ref, rz_ref,
                 at0_ref, at1_ref, w0_ref, w1_ref, acc_ref, ht_ref):
    t = pl.program_id(0)
    s = pl.program_id(1)
    n_steps = pl.num_programs(1)
    n_chunks = (n_steps - 1) * PEER_SUBSTEPS
    d = x_ref.shape[-1]
    tm = x_ref.shape[0]
    nk = keys_ref.shape[2]
    dsub = keys_ref.shape[3]
    k = PEER_TOPK
    sort_w = PEER_SORT_W
    at_refs = (at0_ref, at1_ref)
    w_refs = (w0_ref, w1_ref)
    zero_b = jnp.zeros((), BF16)

    @pl.when(s == 0)
    def _():
        hb = h_ref[...]
        for half in range(tm // TOK_TILE):
            cs = slice(half * TOK_TILE, (half + 1) * TOK_TILE)
            ht_ref[:, cs] = hb[cs, :].astype(F32).T.astype(BF16)
        q = _dot(hb, wq_ref[...]).astype(BF16)
        for h in range(PEER_HEADS):
            for p in range(2):
                col = (h * 2 + p) * dsub
                (s1_ref if p == 0 else s2_ref)[h] = _dot_nt(keys_ref[h, p], q[:, col:col + dsub])
        neg = jnp.full((SUBLANES, sort_w), -jnp.inf, F32)
        for blk in range(tm // sort_w):
            ls = slice(blk * sort_w, (blk + 1) * sort_w)

            def sort_head(h, carry):
                for sref, tref in ((s1_ref, t1_ref), (s2_ref, t2_ref)):
                    v = _top16_sorted(sref[h, :, ls])
                    for r in range(k):
                        tref[blk, r, pl.ds(h, 1), :] = v[r][0:1, :]
                return carry

            lax.fori_loop(0, PEER_HEADS, sort_head, 0)
            t1 = [t1_ref[blk, r] for r in range(k)]
            t2 = [t2_ref[blk, r] for r in range(k)]
            rows = [[t1[r1] + t2[r2] for r2 in range(k // (r1 + 1))] for r1 in range(k)]
            best = rows[0]
            rest = [x for rw in rows[1:] for x in rw]
            while rest:
                grp, rest = rest[:k], rest[k:]
                grp = grp + [neg] * (k - len(grp))
                _bitonic_sort_desc(grp)
                best = _merge_top(best, grp)
            mx = best[0]
            z = jnp.exp(best[0] - mx)
            for r in range(1, k):
                z = z + jnp.exp(best[r] - mx)
            rz = 1.0 / z
            tau = best[k - 1]
            for r1 in range(k):
                cnt = jnp.where(rows[r1][0] >= tau, 1.0, 0.0)
                for r2 in range(1, k // (r1 + 1)):
                    cnt = cnt + jnp.where(rows[r1][r2] >= tau, 1.0, 0.0)
                cr_ref[blk, r1] = cnt
            rz_ref[blk] = rz

            def finish_head(h, carry):
                hs = pl.ds(h, 1)
                s1h = s1_ref[h, :, ls]
                s2h = s2_ref[h, :, ls]
                cn = jnp.zeros_like(s1h)
                rk = jnp.full_like(s2h, float(k))
                for r in reversed(range(k)):
                    cn = jnp.where(s1h == t1_ref[blk, r, hs, :], cr_ref[blk, r, hs, :], cn)
                    rk = jnp.where(s2h >= t2_ref[blk, r, hs, :], float(r), rk)
                cnt_ref[h, :, ls] = cn
                rk2_ref[h, :, ls] = rk.astype(BF16)
                f1_ref[h, :, ls] = jnp.exp(s1h - t1_ref[blk, 0, hs, :]) * rz_ref[blk, hs, :]
                e2_ref[h, :, ls] = jnp.exp(s2h - t2_ref[blk, 0, hs, :]).astype(BF16)
                return carry

            lax.fori_loop(0, PEER_HEADS, finish_head, 0)
        acc_ref[...] = jnp.zeros_like(acc_ref)

    ec = n_i1 * nk
    rb_rows = nk // 2
    n_grp = n_i1 // PEER_GROUP
    g_rows = ec // n_grp
    gpm = n_grp // PEER_MM_SLICES
    d_rows = d // PEER_MM_SLICES
    e_rows = ec // PEER_MM_SLICES

    def gate_group(grp, at_oth, w_oth, cb):
        g_off = pl.multiple_of(grp * g_rows, g_rows)
        i1s = [cb * n_i1 + grp * PEER_GROUP + jj for jj in range(PEER_GROUP)]
        cnrows = [[cnt_ref[h, pl.ds(i1, 1), :].astype(BF16) for i1 in i1s] for h in range(PEER_HEADS)]
        f1rows = [[f1_ref[h, pl.ds(i1, 1), :].astype(BF16) for i1 in i1s] for h in range(PEER_HEADS)]
        for tb in range(tm // LANES):
            ts = slice(tb * LANES, (tb + 1) * LANES)
            for rb in range(nk // rb_rows):
                rs = slice(rb * rb_rows, (rb + 1) * rb_rows)
                gs = [jnp.zeros((rb_rows, LANES), BF16) for _ in range(PEER_GROUP)]
                for h in range(PEER_HEADS):
                    rk2 = rk2_ref[h, rs, ts]
                    e2b = e2_ref[h, rs, ts]
                    for jj in range(PEER_GROUP):
                        cnr = cnrows[h][jj][:, ts]
                        f1r = f1rows[h][jj][:, ts]
                        gs[jj] = gs[jj] + jnp.where(rk2 < cnr, e2b * f1r, zero_b)
                for jj in range(PEER_GROUP):
                    r0 = pl.multiple_of(g_off + jj * nk + rb * rb_rows, rb_rows)
                    a = at_oth[pl.ds(r0, rb_rows), ts].astype(BF16)
                    act = (0.5 * a) * (1.0 + _erf(a * (2.0 ** -0.5)))
                    w_oth[pl.ds(r0, rb_rows), ts] = act * gs[jj]

    def substeps(do1, do2, do3):
        for sub in range(PEER_SUBSTEPS):
            at_cur, at_oth = at_refs[sub], at_refs[1 - sub]
            w_cur, w_oth = w_refs[sub], w_refs[1 - sub]
            cb = s * PEER_SUBSTEPS + sub - 1

            def mm_slice(ms, carry, sub=sub, at_cur=at_cur, at_oth=at_oth, w_cur=w_cur, w_oth=w_oth, cb=cb):
                if do1[sub]:
                    e_off = pl.multiple_of(ms * e_rows, e_rows)
                    u_off = pl.multiple_of((sub * ec + e_off) // 2, e_rows // 2)
                    u_rows = pltpu.bitcast(u_ref[pl.ds(u_off, e_rows // 2), :], BF16)
                    at_cur[pl.ds(e_off, e_rows), :] = _dot(u_rows, ht_ref[...])
                if do3[sub]:
                    d_off = pl.multiple_of(ms * d_rows, d_rows)
                    v_off = pl.multiple_of(d_off // 2, d_rows // 2)
                    v_rows = pltpu.bitcast(vt_ref[pl.ds(v_off, d_rows // 2), sub * ec:(sub + 1) * ec], BF16)
                    acc_ref[pl.ds(d_off, d_rows), :] += _dot(v_rows, w_cur[...])
                if do2[sub]:
                    for gi in range(gpm):
                        gate_group(ms * gpm + gi, at_oth, w_oth, cb)
                return carry

            lax.fori_loop(0, PEER_MM_SLICES, mm_slice, 0)

    first = s == 0
    last = s == n_steps - 1

    @pl.when(first)
    def _():
        substeps(do1=(True, True), do2=(False, True), do3=(False, False))

    @pl.when(jnp.logical_not(first | last))
    def _():
        substeps(do1=(True, True), do2=(True, True), do3=(True, True))

    @pl.when(last)
    def _():
        substeps(do1=(False, False), do2=(True, False), do3=(True, True))

    @pl.when(s == n_steps - 1)
    def _():
        for half in range(tm // TOK_TILE):
            g256 = t * (tm // TOK_TILE) + half
            row = jnp.where(g256 % tiles_per_batch < n_lat_tiles, g256 // tiles_per_batch, n_batch)
            g2 = ada_ref[pl.ds(row, 1), 5 * d:6 * d]
            rs = slice(half * TOK_TILE, (half + 1) * TOK_TILE)
            x2 = x_ref[rs, :] + g2 * acc_ref[:, rs].T
            if final:
                x2 = _rms(x2, fg_ref[...])
            o_ref[rs, :] = x2


def _peer(x1, h2, ada, final_g, wq, keys, u, vt, layer, tiles_per_batch, n_lat_tiles, n_batch, final):
    rows, d = x1.shape
    tm = PEER_TILE
    nk = keys.shape[2]
    n_i1 = PEER_CHUNK_I1
    ec = n_i1 * nk
    n_blocks = 2 * u.shape[1] // (PEER_SUBSTEPS * ec)
    n_sort = tm // PEER_SORT_W
    assert PEER_HEADS == SUBLANES
    full = lambda arr: pl.BlockSpec(arr.shape, lambda t, s: (0,) * arr.ndim, pipeline_mode=pl.Buffered(1))
    tok = pl.BlockSpec((tm, d), lambda t, s: (t, 0))
    return pl.pallas_call(
        functools.partial(_peer_kernel, tiles_per_batch, n_lat_tiles, n_batch, n_i1, final),
        grid=(rows // tm, n_blocks + 1),
        in_specs=[tok, tok, full(ada), full(final_g), full(wq), full(keys),
                  pl.BlockSpec((None, PEER_SUBSTEPS * ec // 2, d),
                               lambda t, s: (layer, jnp.minimum(s, n_blocks - 1), 0)),
                  pl.BlockSpec((None, d // 2, PEER_SUBSTEPS * ec), lambda t, s: (layer, 0, jnp.maximum(s - 1, 0)))],
        out_specs=tok,
        out_shape=jax.ShapeDtypeStruct((rows, d), F32),
        scratch_shapes=[pltpu.VMEM((PEER_HEADS, nk, tm), F32), pltpu.VMEM((PEER_HEADS, nk, tm), F32),
                        pltpu.VMEM((PEER_HEADS, nk, tm), F32), pltpu.VMEM((PEER_HEADS, nk, tm), F32),
                        pltpu.VMEM((PEER_HEADS, nk, tm), BF16), pltpu.VMEM((PEER_HEADS, nk, tm), BF16),
                        pltpu.VMEM((n_sort, PEER_TOPK, PEER_HEADS, PEER_SORT_W), F32),
                        pltpu.VMEM((n_sort, PEER_TOPK, PEER_HEADS, PEER_SORT_W), F32),
                        pltpu.VMEM((n_sort, PEER_TOPK, PEER_HEADS, PEER_SORT_W), F32),
                        pltpu.VMEM((n_sort, PEER_HEADS, PEER_SORT_W), F32),
                        pltpu.VMEM((ec, tm), F32), pltpu.VMEM((ec, tm), F32),
                        pltpu.VMEM((ec, tm), BF16), pltpu.VMEM((ec, tm), BF16),
                        pltpu.VMEM((d, tm), F32), pltpu.VMEM((d, tm), BF16)],
        compiler_params=_cparams(("parallel", "arbitrary")),
        name="peer",
    )(x1, h2, ada, final_g, wq, keys, u, vt)


def _rope_tables(seq, t):
    quarter = DA_QK // 4
    inv = ROPE_BASE ** (-np.arange(quarter, dtype=np.float32) / quarter)
    pos = np.arange(seq)
    ang_r = (pos // GRID_W).astype(np.float32)[:, None] * inv[None, :]
    ang_c = (pos % GRID_W).astype(np.float32)[:, None] * inv[None, :]
    half = np.concatenate([ang_r, ang_c, ang_r, ang_c], axis=1)
    cos = np.ones((t, LANES), np.float32)
    sin = np.zeros((t, LANES), np.float32)
    cos[:seq] = np.concatenate([np.cos(half), np.cos(half)], axis=1)
    sin[:seq] = np.concatenate([-np.sin(half), np.sin(half)], axis=1)
    return jnp.asarray(cos), jnp.asarray(sin)


def _pack_kernel(transpose, x_ref, o_ref):
    x = x_ref[0]
    if transpose:
        x = x.T
    o_ref[0] = pltpu.bitcast(x.astype(BF16), jnp.uint32)


def _pack_table(tab, transpose):
    depth, r, c = tab.shape
    rb = PACK_ROWS
    if transpose:
        in_spec = pl.BlockSpec((1, rb, c), lambda l, j: (l, j, 0))
        out_spec = pl.BlockSpec((1, c // 2, rb), lambda l, j: (l, 0, j))
        out_shape = (depth, c // 2, r)
    else:
        in_spec = pl.BlockSpec((1, rb, c), lambda l, j: (l, j, 0))
        out_spec = pl.BlockSpec((1, rb // 2, c), lambda l, j: (l, j, 0))
        out_shape = (depth, r // 2, c)
    return pl.pallas_call(
        functools.partial(_pack_kernel, transpose),
        grid=(depth, r // rb),
        in_specs=[in_spec],
        out_specs=out_spec,
        out_shape=jax.ShapeDtypeStruct(out_shape, jnp.uint32),
        compiler_params=_cparams(("parallel", "parallel")),
        name="pack_t" if transpose else "pack",
    )(tab)


def kernel(x, c, ctx, c_ctx, w_ada, b_ada, norm1_g, norm2_g, w_in, ml_gate_b, ml_conv_w, ml_norm_g, cv_conv_w,
           da_lam, da_subln_g, w_da, w_ml, w_cv, w_o, peer_wq, peer_keys, peer_u, peer_v, final_g):
    bsz, seq, d = x.shape
    n_ctx = ctx.shape[1]
    t = seq + n_ctx
    depth = w_ada.shape[0]
    tm = TOK_TILE
    assert seq % tm == 0 and n_ctx % tm == 0 and seq % ML_CHUNK == 0 and n_ctx % ML_CHUNK == 0
    assert (bsz * t) % PEER_TILE == 0 and (bsz * seq) % PEER_TILE == 0
    assert seq % ATTN_TILE == 0 and seq % n_ctx == 0
    n_lat_tiles = seq // tm
    n_tiles = t // tm

    rows = -(-(bsz + 1) // SUBLANES) * SUBLANES
    cc = jnp.zeros((rows, d), F32).at[:bsz].set(c).at[bsz].set(c_ctx)
    ada = _ada(cc, w_ada, b_ada)

    o = np.cumsum([0, DA_WIDTH, DA_WIDTH, DA_WIDTH, 2 * ML_WIDTH, ML_WIDTH, ML_WIDTH, 4 * ML_HEADS,
                   CV_WIDTH, CV_WIDTH, CV_WIDTH, 3 * d])
    quarter = DA_QK // 4
    wqk = w_in[:, :, o[0]:o[2]].reshape(depth, d, 2 * DA_HEADS, 2, 2, 2, quarter)
    wqk = wqk.transpose(0, 1, 2, 5, 3, 4, 6).reshape(depth, d, 2 * DA_WIDTH)
    wda_in = jnp.concatenate([wqk, w_in[:, :, o[2]:o[3]]], axis=-1).astype(BF16)
    wml_in = w_in[:, :, o[3]:o[6]].astype(BF16)
    wmg_in = jnp.pad(w_in[:, :, o[6]:o[7]], ((0, 0), (0, 0), (0, LANES - 4 * ML_HEADS))).astype(BF16)
    wcv_in = w_in[:, :, o[7]:o[10]].astype(BF16)
    wgt_in = w_in[:, :, o[10]:o[11]].astype(BF16)
    gate_b = jnp.pad(ml_gate_b.reshape(depth, 1, 4 * ML_HEADS), ((0, 0), (0, 0), (0, LANES - 4 * ML_HEADS)))
    lam_p = da_lam
    w_da_b, w_ml_b, w_cv_b, w_o_b = (a.astype(BF16) for a in (w_da, w_ml, w_cv, w_o))
    wq_b = peer_wq.astype(BF16)
    keys_b = peer_keys.astype(BF16)
    u_b = _pack_table(peer_u, transpose=False)
    vt_b = _pack_table(peer_v, transpose=True)
    cos, sin = _rope_tables(seq, t)

    xs = jnp.concatenate([x, ctx], axis=1)
    for l in range(depth):
        need_ctx = l < depth - 1
        lam_init = 0.8 - 0.6 * math.exp(-0.3 * l)
        da, ml, mg, cv, gt = _inproj(xs, ada[l], norm1_g[l].reshape(1, d), cos, sin,
                                     wda_in[l], wml_in[l], wmg_in[l], wcv_in[l], wgt_in[l], n_lat_tiles)
        yda = _attention(da, lam_p[l], da_subln_g[l], lam_init, ATTN_TILE, 0, seq // ATTN_TILE, 0, t)
        if need_ctx:
            yda = _attention(da, lam_p[l], da_subln_g[l], lam_init, n_ctx, seq, 1, seq, n_ctx, prev=yda)
        yml = _mlstm(ml, mg, ml_conv_w[l], gate_b[l], ml_norm_g[l], seq // ML_CHUNK, n_ctx // ML_CHUNK)
        n_out_tiles = n_tiles if need_ctx else n_lat_tiles
        x1, h2 = _merge(xs, ada[l], norm2_g[l].reshape(1, d), yda, yml, cv, gt, cv_conv_w[l],
                        w_da_b[l], w_ml_b[l], w_cv_b[l], w_o_b[l], n_lat_tiles, n_out_tiles)
        t_out = n_out_tiles * tm
        xs = _peer(x1.reshape(bsz * t_out, d), h2.reshape(bsz * t_out, d), ada[l], final_g.reshape(1, d),
                   wq_b[l], keys_b[l], u_b, vt_b, l, n_out_tiles, n_lat_tiles, bsz, final=not need_ctx)
        xs = xs.reshape(bsz, t_out, d)
    return xs
```

```python
import functools
import math

import numpy as np
import jax
import jax.numpy as jnp
from jax import lax
from jax.experimental import pallas as pl
from jax.experimental.pallas import tpu as pltpu

EPS = 1e-6
GRID_W = 64
DA_HEADS = 4
DA_QK = 64
DA_V = 2 * DA_QK
DA_WIDTH = DA_HEADS * DA_V
ROPE_BASE = 10000.0
ML_HEADS = 4
ML_DH = 128
ML_WIDTH = ML_HEADS * ML_DH
ML_CHUNK = 128
CV_WIDTH = 512
PEER_HEADS = 8
PEER_TOPK = 16
PEER_GROUP = 2

LANES = 128
SUBLANES = 8
ATTN_TILE = 1024
TOK_TILE = 256
HALO = 16
PEER_TILE = 512
PEER_CHUNK_I1 = 8
PEER_SUBSTEPS = 2
PACK_ROWS = 512
PEER_MM_SLICES = 2
PEER_SORT_W = 2 * LANES
VMEM_LIMIT = 56 * 1024 * 1024

NEG_BIG = -1e30
BF16 = jnp.bfloat16
F32 = jnp.float32
HIGHEST = lax.Precision.HIGHEST


def _cparams(sem):
    return pltpu.CompilerParams(dimension_semantics=sem, vmem_limit_bytes=VMEM_LIMIT)


def _dot(a, b):
    return jnp.dot(a, b, preferred_element_type=F32)


def _dot_nt(a, b):
    return lax.dot_general(a, b, (((1,), (1,)), ((), ())), preferred_element_type=F32)


def _dot_tn(a, b):
    return lax.dot_general(a, b, (((0,), (0,)), ((), ())), preferred_element_type=F32)


def _rms(x, g):
    return x * lax.rsqrt(jnp.mean(x * x, axis=-1, keepdims=True) + EPS) * g


def _sigmoid(x):
    return 1.0 / (1.0 + jnp.exp(-x))


def _log_sigmoid(x):
    return jnp.minimum(x, 0.0) - jnp.log1p(jnp.exp(-jnp.abs(x)))


def _ada_kernel(c_ref, w_ref, b_ref, o_ref):
    c = c_ref[...]
    s = c * _sigmoid(c)
    o_ref[0] = jnp.dot(s, w_ref[0], preferred_element_type=F32, precision=HIGHEST) + b_ref[0]


def _ada(cc, w_ada, b_ada):
    depth, d, d6 = w_ada.shape
    rows = cc.shape[0]
    nb = d6 // d
    return pl.pallas_call(
        _ada_kernel,
        grid=(depth, nb),
        in_specs=[pl.BlockSpec((rows, d), lambda l, j: (0, 0)),
                  pl.BlockSpec((1, d, d), lambda l, j: (l, 0, j)),
                  pl.BlockSpec((1, 1, d), lambda l, j: (l, 0, j))],
        out_specs=pl.BlockSpec((1, rows, d), lambda l, j: (l, 0, j)),
        out_shape=jax.ShapeDtypeStruct((depth, rows, d6), F32),
        compiler_params=_cparams(("parallel", "parallel")),
        name="ada",
    )(cc, w_ada, b_ada.reshape(depth, 1, d6))


def _inproj_kernel(n_lat_tiles, n_batch, xl_ref, xc_ref, ada_ref, g_ref, cos_ref, sin_ref,
                   wda_ref, wml_ref, wmg_ref, wcv_ref, wgt_ref,
                   da_ref, ml_ref, mg_ref, cv_ref, gt_ref):
    b = pl.program_id(0)
    i = pl.program_id(1)
    d = xl_ref.shape[-1]
    is_lat = i < n_lat_tiles
    row = jnp.where(is_lat, b, n_batch)
    shift = ada_ref[pl.ds(row, 1), 0:d]
    scale = ada_ref[pl.ds(row, 1), d:2 * d]
    x = jnp.where(is_lat, xl_ref[0], xc_ref[0])
    h = _rms(x, g_ref[...]) * (1.0 + scale) + shift
    hb = h.astype(BF16)

    zda = _dot(hb, wda_ref[...])
    cos = cos_ref[...]
    sin = sin_ref[...]
    for j in range(2 * DA_HEADS):
        blk = zda[:, j * LANES:(j + 1) * LANES]
        rot = blk * cos + pltpu.roll(blk, LANES // 2, 1) * sin
        da_ref[0, :, j * LANES:(j + 1) * LANES] = rot.astype(BF16)
    da_ref[0, :, 2 * DA_WIDTH:] = zda[:, 2 * DA_WIDTH:].astype(BF16)

    ml_ref[0] = _dot(hb, wml_ref[...]).astype(BF16)
    mg_ref[0] = _dot(hb, wmg_ref[...])

    zcv = _dot(hb, wcv_ref[...])
    cv_ref[0, :, 0:CV_WIDTH] = zcv[:, 0:CV_WIDTH].astype(BF16)
    cv_ref[0, :, CV_WIDTH:] = (zcv[:, CV_WIDTH:2 * CV_WIDTH] * zcv[:, 2 * CV_WIDTH:]).astype(BF16)

    for j in range(3):
        zg = _dot(hb, wgt_ref[:, j * d:(j + 1) * d])
        gt_ref[0, :, j * d:(j + 1) * d] = _sigmoid(zg).astype(BF16)


def _stream_specs(tm, d, n_lat_tiles, n_tiles, xc_tile0):
    n_ctx_tiles = n_tiles - n_lat_tiles
    lat = pl.BlockSpec((1, tm, d), lambda b, i: (b, jnp.minimum(i, n_lat_tiles - 1), 0))
    ctx = pl.BlockSpec((1, tm, d), lambda b, i: (b, xc_tile0 + jnp.clip(i - n_lat_tiles, 0, n_ctx_tiles - 1), 0))
    return lat, ctx


def _inproj(xl, xc, xc_tile0, t, ada, g, cos, sin, wda, wml, wmg, wcv, wgt, n_lat_tiles):
    bsz, _, d = xl.shape
    tm = TOK_TILE
    full = lambda arr: pl.BlockSpec(arr.shape, lambda b, i: (0,) * arr.ndim, pipeline_mode=pl.Buffered(1))
    tok = lambda w: pl.BlockSpec((1, tm, w), lambda b, i: (b, i, 0))
    lat_spec, ctx_spec = _stream_specs(tm, d, n_lat_tiles, t // tm, xc_tile0)
    widths =(wda.shape[1], wml.shape[1], wmg.shape[1], 2 * CV_WIDTH, wgt.shape[1])
    dts = (BF16, BF16, F32, BF16, BF16)
    return pl.pallas_call(
        functools.partial(_inproj_kernel, n_lat_tiles, bsz),
        grid=(bsz, t // tm),
        in_specs=[lat_spec, ctx_spec, full(ada), full(g),
                  pl.BlockSpec((tm, LANES), lambda b, i: (i, 0)),
                  pl.BlockSpec((tm, LANES), lambda b, i: (i, 0)),
                  full(wda), full(wml), full(wmg), full(wcv), full(wgt)],
        out_specs=[tok(w) for w in widths],
        out_shape=[jax.ShapeDtypeStruct((bsz, t, w), dt) for w, dt in zip(widths, dts)],
        compiler_params=_cparams(("parallel", "parallel")),
        name="inproj",
    )(xl, xc, ada, g, cos, sin, wda, wml, wmg, wcv, wgt)


def _attn_kernel(lam_init, q_ref, k_ref, v_ref, lam_ref, g_ref, *rest):
    o_ref, va_ref = rest[-2:]
    i = pl.program_id(2)
    lv = lam_ref[...]
    lam = (jnp.exp(jnp.sum(lv[0:1] * lv[1:2], axis=-1, keepdims=True))
           - jnp.exp(jnp.sum(lv[2:3] * lv[3:4], axis=-1, keepdims=True)) + lam_init)
    lane = lax.broadcasted_iota(jnp.int32, (1, LANES), 1)
    map0 = (lane % (LANES // 2)) < (DA_QK // 2)

    @pl.when(i == 0)
    def _():
        va_ref[:, 0:LANES] = v_ref[0]
        va_ref[:, LANES:] = jnp.ones((va_ref.shape[0], LANES), BF16)

    k = k_ref[0]
    va = va_ref[...]
    q = q_ref[0].astype(F32) * (DA_QK ** -0.5)
    q0 = jnp.where(map0, q, 0.0).astype(BF16)
    q1 = jnp.where(map0, 0.0, q).astype(BF16)
    s0 = _dot_nt(q0, k)
    s1 = _dot_nt(q1, k)
    e0 = jnp.exp((s0 - jnp.max(s0, axis=-1, keepdims=True)).astype(BF16))
    e1 = jnp.exp((s1 - jnp.max(s1, axis=-1, keepdims=True)).astype(BF16))
    o0 = _dot(e0, va)
    o1 = _dot(e1, va)
    o = o0[:, 0:LANES] / o0[:, LANES:] - lam * (o1[:, 0:LANES] / o1[:, LANES:])
    o_ref[0] = (_rms(o, g_ref[...]) * (1.0 - lam_init)).astype(o_ref.dtype)


def _attention(da, lam_p, subln_g, lam_init, tq, q_row0, n_q, kv_row0, kv_rows, prev=None):
    bsz, t, _ = da.shape
    q0 = q_row0 // tq
    kv0 = kv_row0 // kv_rows
    in_specs = [pl.BlockSpec((1, tq, LANES), lambda b, h, i: (b, q0 + i, h)),
                pl.BlockSpec((1, kv_rows, LANES), lambda b, h, i: (b, kv0, DA_HEADS + h)),
                pl.BlockSpec((1, kv_rows, LANES), lambda b, h, i: (b, kv0, 2 * DA_HEADS + h)),
                pl.BlockSpec(lam_p.shape, lambda b, h, i: (0, 0)),
                pl.BlockSpec((1, LANES), lambda b, h, i: (0, 0))]
    args = [da, da, da, lam_p, subln_g.reshape(1, LANES)]
    aliases = {}
    if prev is not None:
        in_specs.append(pl.BlockSpec(memory_space=pl.ANY))
        args.append(prev)
        aliases = {len(args) - 1: 0}
    return pl.pallas_call(
        functools.partial(_attn_kernel, lam_init),
        grid=(bsz, DA_HEADS, n_q),
        in_specs=in_specs,
        out_specs=pl.BlockSpec((1, tq, LANES), lambda b, h, i: (b, q0 + i, h)),
        out_shape=jax.ShapeDtypeStruct((bsz, t, DA_WIDTH), BF16),
        scratch_shapes=[pltpu.VMEM((kv_rows, 2 * LANES), BF16)],
        input_output_aliases=aliases,
        compiler_params=_cparams(("parallel", "parallel", "arbitrary")),
        name="diffattn",
    )(*args)


def _mlstm_kernel(ncl, ncc, ml_ref, mg_ref, cw_ref, gb_ref, ng_ref, o_ref,
                  qs_ref, ks_ref, hf_ref, hb_ref, c_ref, n_ref, m_ref):
    L = ML_CHUNK
    nc = ncl + ncc
    t_all = nc * L
    seq = ncl * L
    w = ML_WIDTH

    trow = lax.broadcasted_iota(jnp.int32, (t_all, 1), 0)
    first = (trow == 0) | (trow == seq)
    last = (trow == seq - 1) | (trow == t_all - 1)
    for j in range(2 * ML_HEADS):
        x = ml_ref[0, :, j * LANES:(j + 1) * LANES].astype(F32)
        xm = jnp.where(first, 0.0, pltpu.roll(x, 1, 0))
        xp = jnp.where(last, 0.0, pltpu.roll(x, t_all - 1, 0))
        cw = cw_ref[:, j * LANES:(j + 1) * LANES]
        y = xm * cw[0:1] + x * cw[1:2] + xp * cw[2:3]
        y = y * _sigmoid(y)
        if j < ML_HEADS:
            qs_ref[:, j * LANES:(j + 1) * LANES] = y.astype(BF16)
        else:
            jj = j - ML_HEADS
            ks_ref[:, jj * LANES:(jj + 1) * LANES] = (y * (ML_DH ** -0.5)).astype(BF16)

    c_ref[...] = jnp.zeros_like(c_ref)
    n_ref[...] = jnp.zeros_like(n_ref)
    m_ref[...] = jnp.zeros_like(m_ref)

    r_i = lax.broadcasted_iota(jnp.int32, (L, L), 0)
    c_i = lax.broadcasted_iota(jnp.int32, (L, L), 1)
    lower = r_i >= c_i
    upper = r_i <= c_i
    lower_f = lower.astype(F32)
    upper_f = upper.astype(F32)
    ones_blk = jnp.ones((L, LANES), BF16)

    def step(it, carry):
        dirs = []
        for dr in range(2):
            if dr == 0:
                ch = (it + ncl) % nc
                seen, tri_col, tri_row, end, h_ref = lower, lower_f, upper_f, L - 1, hf_ref
            else:
                ch = nc - 1 - it
                seen, tri_col, tri_row, end, h_ref = upper, upper_f, lower_f, 0, hb_ref
            r0 = pl.multiple_of(ch * L, L)
            g = mg_ref[0, pl.ds(r0, L), :] + gb_ref[...]
            gt = g.T
            bcol_all = jnp.dot(tri_col, _log_sigmoid(g), preferred_element_type=F32, precision=HIGHEST)
            brow_all = jnp.dot(_log_sigmoid(gt), tri_row, preferred_element_type=F32, precision=HIGHEST)
            dirs.append((seen, end, h_ref, r0, g, gt, bcol_all, brow_all))

        rec = []
        for dr in range(2):
            seen, end, h_ref, r0, g, gt, bcol_all, brow_all = dirs[dr]
            for h in range(ML_HEADS):
                icol = (2 * dr) * ML_HEADS + h
                fcol = (2 * dr + 1) * ML_HEADS + h
                idx = dr * ML_HEADS + h
                rec.append(dict(
                    seen=seen, end=end, h_ref=h_ref, r0=r0, idx=idx, cols=slice(h * LANES, (h + 1) * LANES),
                    q=qs_ref[pl.ds(r0, L), h * LANES:(h + 1) * LANES],
                    k=ks_ref[pl.ds(r0, L), h * LANES:(h + 1) * LANES],
                    v=ml_ref[0, pl.ds(r0, L), 2 * w + h * LANES:2 * w + (h + 1) * LANES],
                    b_col=bcol_all[:, fcol:fcol + 1], ig_col=g[:, icol:icol + 1],
                    b_row=brow_all[fcol:fcol + 1, :], ig_row=gt[icol:icol + 1, :],
                    m_prev=m_ref[idx][:, 0:1], cmat=c_ref[idx], nrow=n_ref[idx]))

        for c in rec:
            c["b_cb"] = jnp.broadcast_to(c["b_col"], (L, L))
            c["log_d"] = jnp.where(c["seen"], c["b_cb"] - c["b_row"] + c["ig_row"], NEG_BIG)
        for c in rec:
            c["m_path"] = c["b_col"] + c["m_prev"]
            c["m_t"] = jnp.maximum(c["m_path"], jnp.max(c["log_d"], axis=-1, keepdims=True))
        for c in rec:
            c["m_tb"] = jnp.broadcast_to(c["m_t"], (L, L))
            c["dmat"] = jnp.exp(c["log_d"] - c["m_tb"])
            c["carry_b"] = jnp.exp((c["b_cb"] + c["m_prev"]) - c["m_tb"])
        for c in rec:
            wmat = _dot_nt(c["q"], c["k"]) * c["dmat"]
            wv = _dot(wmat.astype(BF16), jnp.concatenate([c["v"], ones_blk], axis=1))
            qn = _dot_nt(c["q"], jnp.broadcast_to(c["nrow"], (L, LANES)).astype(BF16))
            num = wv[:, 0:LANES] + c["carry_b"] * _dot(c["q"], c["cmat"].astype(BF16))
            den = wv[:, LANES:] + c["carry_b"] * qn
            hval = num / jnp.maximum(jnp.abs(den), jnp.exp(-c["m_tb"]))
            c["h_ref"][pl.ds(c["r0"], L), c["cols"]] = hval
        for c in rec:
            b_end = c["b_col"][c["end"]:c["end"] + 1, :]
            log_w = b_end - c["b_col"] + c["ig_col"]
            m_new = jnp.maximum(b_end + c["m_prev"], jnp.max(log_w, axis=0, keepdims=True))
            c["wk"] = jnp.exp(log_w - m_new)
            c["decay"] = jnp.exp(b_end + c["m_prev"] - m_new)
            c["m_new"] = m_new
        for c in rec:
            kw = c["k"].astype(F32) * c["wk"]
            idx = c["idx"]
            c_ref[idx] = c["decay"] * c["cmat"] + _dot_tn(kw.astype(BF16), c["v"])
            n_ref[idx] = c["decay"] * c["nrow"] + jnp.sum(kw, axis=0, keepdims=True)
            m_ref[idx] = jnp.broadcast_to(c["m_new"], (1, LANES))
        return carry

    lax.fori_loop(0, nc, step, 0)

    for h in range(ML_HEADS):
        sl = slice(h * LANES, (h + 1) * LANES)
        hs = hf_ref[:, sl] + hb_ref[:, sl]
        y = _rms(hs, ng_ref[:, sl])
        og = ml_ref[0, :, 3 * w + h * LANES:3 * w + (h + 1) * LANES].astype(F32)
        o_ref[0, :, sl] = (y * _sigmoid(og)).astype(o_ref.dtype)


def _mlstm(ml, mg, conv_w, gate_b_row, norm_g, ncl, ncc):
    bsz, t, wtot = ml.shape
    nstate = 2 * ML_HEADS
    return pl.pallas_call(
        functools.partial(_mlstm_kernel, ncl, ncc),
        grid=(bsz,),
        in_specs=[pl.BlockSpec((1, t, wtot), lambda b: (b, 0, 0)),
                  pl.BlockSpec((1, t, LANES), lambda b: (b, 0, 0)),
                  pl.BlockSpec(conv_w.shape, lambda b: (0, 0)),
                  pl.BlockSpec((1, LANES), lambda b: (0, 0)),
                  pl.BlockSpec((1, ML_WIDTH), lambda b: (0, 0))],
        out_specs=pl.BlockSpec((1, t, ML_WIDTH), lambda b: (b, 0, 0)),
        out_shape=jax.ShapeDtypeStruct((bsz, t, ML_WIDTH), BF16),
        scratch_shapes=[pltpu.VMEM((t, ML_WIDTH), BF16), pltpu.VMEM((t, ML_WIDTH), BF16),
                        pltpu.VMEM((t, ML_WIDTH), F32), pltpu.VMEM((t, ML_WIDTH), F32),
                        pltpu.VMEM((nstate, ML_DH, ML_DH), F32),
                        pltpu.VMEM((nstate, 1, ML_DH), F32),
                        pltpu.VMEM((nstate, 1, LANES), F32)],
        compiler_params=_cparams(("parallel",)),
        name="mlstm",
    )(ml, mg, conv_w, gate_b_row, norm_g.reshape(1, ML_WIDTH))


def _merge_kernel(n_lat_tiles, n_batch, xl_ref, xc_ref, ada_ref, g2_ref, yda_ref, yml_ref, cv_ref, pprev_ref, pnext_ref,
                  gt_ref, cw_ref, wda_ref, wml_ref, wcv_ref, wo_ref, x1_ref, h2_ref):
    b = pl.program_id(0)
    i = pl.program_id(1)
    n_tiles = pl.num_programs(1)
    d = xl_ref.shape[-1]
    tm = xl_ref.shape[1]
    is_lat = i < n_lat_tiles
    row = jnp.where(is_lat, b, n_batch)

    bg = cv_ref[0, :, 0:CV_WIDTH].astype(F32)
    p = cv_ref[0, :, CV_WIDTH:].astype(F32)
    seq_start = (i == 0) | (i == n_lat_tiles)
    seq_end = (i == n_lat_tiles - 1) | (i == n_tiles - 1)
    prev_row = jnp.where(seq_start, 0.0, pprev_ref[0, HALO - 1:HALO, :].astype(F32))
    next_row = jnp.where(seq_end, 0.0, pnext_ref[0, 0:1, :].astype(F32))
    trow = lax.broadcasted_iota(jnp.int32, (tm, 1), 0)
    pm = jnp.where(trow == 0, prev_row, pltpu.roll(p, 1, 0))
    pp = jnp.where(trow == tm - 1, next_row, pltpu.roll(p, tm - 1, 0))
    cw = cw_ref[...]
    ycv = bg * (pm * cw[0:1] + p * cw[1:2] + pp * cw[2:3])

    m = (gt_ref[0, :, 0:d].astype(F32) * _dot(yda_ref[0], wda_ref[...])
         + gt_ref[0, :, d:2 * d].astype(F32) * _dot(yml_ref[0], wml_ref[...])
         + gt_ref[0, :, 2 * d:].astype(F32) * _dot(ycv.astype(BF16), wcv_ref[...]))
    out = _dot(m.astype(BF16), wo_ref[...])
    g1 = ada_ref[pl.ds(row, 1), 2 * d:3 * d]
    x1 = jnp.where(is_lat, xl_ref[0], xc_ref[0]) + g1 * out
    x1_ref[0] = x1
    sh2 = ada_ref[pl.ds(row, 1), 3 * d:4 * d]
    sc2 = ada_ref[pl.ds(row, 1), 4 * d:5 * d]
    h2_ref[0] = (_rms(x1, g2_ref[...]) * (1.0 + sc2) + sh2).astype(BF16)


def _merge(xl, xc, xc_tile0, t, ada, g2, yda, yml, cv, gt, cv_w, wda, wml, wcv, wo, n_lat_tiles, n_tiles):
    bsz, _, d = xl.shape
    tm = TOK_TILE
    lat_spec, ctx_spec = _stream_specs(tm, d, n_lat_tiles, t // tm, xc_tile0)
    r = tm // HALO
    nrb = t // HALO
    full = lambda arr: pl.BlockSpec(arr.shape, lambda b, i: (0,) * arr.ndim, pipeline_mode=pl.Buffered(1))
    tok = lambda w: pl.BlockSpec((1, tm, w), lambda b, i: (b, i, 0))
    return pl.pallas_call(
        functools.partial(_merge_kernel, n_lat_tiles, bsz),
        grid=(bsz, n_tiles),
        in_specs=[lat_spec, ctx_spec, full(ada), full(g2), tok(DA_WIDTH), tok(ML_WIDTH), tok(2 * CV_WIDTH),
                  pl.BlockSpec((1, HALO, CV_WIDTH), lambda b, i: (b, jnp.maximum(i * r - 1, 0), 1)),
                  pl.BlockSpec((1, HALO, CV_WIDTH), lambda b, i: (b, jnp.minimum((i + 1) * r, nrb - 1), 1)),
                  tok(3 * d), full(cv_w), full(wda), full(wml), full(wcv), full(wo)],
        out_specs=[tok(d), tok(d)],
        out_shape=[jax.ShapeDtypeStruct((bsz, n_tiles * tm, d), F32),
                   jax.ShapeDtypeStruct((bsz, n_tiles * tm, d), BF16)],
        compiler_params=_cparams(("parallel", "parallel")),
        name="merge",
    )(xl, xc, ada, g2, yda, yml, cv, cv, cv, gt, cv_w, wda, wml, wcv, wo)


def _cmpx(v, i, j):
    a, b = v[i], v[j]
    v[i] = jnp.maximum(a, b)
    v[j] = jnp.minimum(a, b)


def _bitonic_merge_desc(v):
    n = len(v)
    j = n // 2
    while j >= 1:
        for i in range(n):
            l = i ^ j
            if l > i:
                _cmpx(v, i, l)
        j //= 2


def _bitonic_sort_desc(v):
    n = len(v)
    k = 2
    while k <= n:
        j = k // 2
        while j >= 1:
            for i in range(n):
                l = i ^ j
                if l > i:
                    if (i & k) == 0:
                        _cmpx(v, i, l)
                    else:
                        _cmpx(v, l, i)
            j //= 2
        k *= 2


def _merge_top(a, b):
    n = len(a)
    c = [jnp.maximum(a[r], b[n - 1 - r]) for r in range(n)]
    _bitonic_merge_desc(c)
    return c


def _top16_sorted(s):
    nk, tm = s.shape
    k = PEER_TOPK
    groups = nk // k
    assert groups == SUBLANES
    a = s.reshape(k, SUBLANES, tm)
    v = [a[r] for r in range(k)]
    _bitonic_sort_desc(v)
    sh = SUBLANES // 2
    while sh >= 1:
        v = _merge_top(v, [pltpu.roll(x, sh, 0) for x in v])
        sh //= 2
    return v


def _erf(x):
    return lax.erf(x)


def _peer_kernel(tiles_per_batch, n_lat_tiles, n_batch, n_i1, final, x_ref, h_ref, ada_ref, fg_ref, wq_ref, keys_ref,
                 u_ref, vt_ref, o_ref, s1_ref, s2_ref, f1_ref, cnt_ref, e2_ref, rk2_ref, t1_ref, t2_ref, cr_ref, rz_ref,
                 at0_ref, at1_ref, w0_ref, w1_ref, acc_ref, ht_ref):
    t = pl.program_id(0)
    s = pl.program_id(1)
    n_steps = pl.num_programs(1)
    n_chunks = (n_steps - 1) * PEER_SUBSTEPS
    d = x_ref.shape[-1]
    tm = x_ref.shape[0]
    nk = keys_ref.shape[2]
    dsub = keys_ref.shape[3]
    k = PEER_TOPK
    sort_w = PEER_SORT_W
    at_refs = (at0_ref, at1_ref)
    w_refs = (w0_ref, w1_ref)
    zero_b = jnp.zeros((), BF16)

    @pl.when(s == 0)
    def _():
        hb = h_ref[...]
        for half in range(tm // TOK_TILE):
            cs = slice(half * TOK_TILE, (half + 1) * TOK_TILE)
            ht_ref[:, cs] = hb[cs, :].astype(F32).T.astype(BF16)
        q = _dot(hb, wq_ref[...]).astype(BF16)
        for h in range(PEER_HEADS):
            for p in range(2):
                col = (h * 2 + p) * dsub
                (s1_ref if p == 0 else s2_ref)[h] = _dot_nt(keys_ref[h, p], q[:, col:col + dsub])
        neg = jnp.full((SUBLANES, sort_w), -jnp.inf, F32)
        for blk in range(tm // sort_w):
            ls = slice(blk * sort_w, (blk + 1) * sort_w)

            def sort_head(h, carry):
                for sref, tref in ((s1_ref, t1_ref), (s2_ref, t2_ref)):
                    v = _top16_sorted(sref[h, :, ls])
                    for r in range(k):
                        tref[blk, r, pl.ds(h, 1), :] = v[r][0:1, :]
                return carry

            lax.fori_loop(0, PEER_HEADS, sort_head, 0)
            t1 = [t1_ref[blk, r] for r in range(k)]
            t2 = [t2_ref[blk, r] for r in range(k)]
            rows = [[t1[r1] + t2[r2] for r2 in range(k // (r1 + 1))] for r1 in range(k)]
            best = rows[0]
            rest = [x for rw in rows[1:] for x in rw]
            while rest:
                grp, rest = rest[:k], rest[k:]
                grp = grp + [neg] * (k - len(grp))
                _bitonic_sort_desc(grp)
                best = _merge_top(best, grp)
            mx = best[0]
            z = jnp.exp(best[0] - mx)
            for r in range(1, k):
                z = z + jnp.exp(best[r] - mx)
            rz = 1.0 / z
            tau = best[k - 1]
            for r1 in range(k):
                cnt = jnp.where(rows[r1][0] >= tau, 1.0, 0.0)
                for r2 in range(1, k // (r1 + 1)):
                    cnt = cnt + jnp.where(rows[r1][r2] >= tau, 1.0, 0.0)
                cr_ref[blk, r1] = cnt
            rz_ref[blk] = rz

            def finish_head(h, carry):
                hs = pl.ds(h, 1)
                s1h = s1_ref[h, :, ls]
                s2h = s2_ref[h, :, ls]
                cn = jnp.zeros_like(s1h)
                rk = jnp.full_like(s2h, float(k))
                for r in reversed(range(k)):
                    cn = jnp.where(s1h == t1_ref[blk, r, hs, :], cr_ref[blk, r, hs, :], cn)
                    rk = jnp.where(s2h >= t2_ref[blk, r, hs, :], float(r), rk)
                cnt_ref[h, :, ls] = cn
                rk2_ref[h, :, ls] = rk.astype(BF16)
                f1_ref[h, :, ls] = jnp.exp(s1h - t1_ref[blk, 0, hs, :]) * rz_ref[blk, hs, :]
                e2_ref[h, :, ls] = jnp.exp(s2h - t2_ref[blk, 0, hs, :]).astype(BF16)
                return carry

            lax.fori_loop(0, PEER_HEADS, finish_head, 0)
        acc_ref[...] = jnp.zeros_like(acc_ref)

    ec = n_i1 * nk
    rb_rows = nk // 2
    n_grp = n_i1 // PEER_GROUP
    g_rows = ec // n_grp
    gpm = n_grp // PEER_MM_SLICES
    d_rows = d // PEER_MM_SLICES
    e_rows = ec // PEER_MM_SLICES

    def gate_group(grp, at_oth, w_oth, cb):
        g_off = pl.multiple_of(grp * g_rows, g_rows)
        i1s = [cb * n_i1 + grp * PEER_GROUP + jj for jj in range(PEER_GROUP)]
        cnrows = [[cnt_ref[h, pl.ds(i1, 1), :].astype(BF16) for i1 in i1s] for h in range(PEER_HEADS)]
        f1rows = [[f1_ref[h, pl.ds(i1, 1), :].astype(BF16) for i1 in i1s] for h in range(PEER_HEADS)]
        for tb in range(tm // LANES):
            ts = slice(tb * LANES, (tb + 1) * LANES)
            for rb in range(nk // rb_rows):
                rs = slice(rb * rb_rows, (rb + 1) * rb_rows)
                gs = [jnp.zeros((rb_rows, LANES), BF16) for _ in range(PEER_GROUP)]
                for h in range(PEER_HEADS):
                    rk2 = rk2_ref[h, rs, ts]
                    e2b = e2_ref[h, rs, ts]
                    for jj in range(PEER_GROUP):
                        cnr = cnrows[h][jj][:, ts]
                        f1r = f1rows[h][jj][:, ts]
                        gs[jj] = gs[jj] + jnp.where(rk2 < cnr, e2b * f1r, zero_b)
                for jj in range(PEER_GROUP):
                    r0 = pl.multiple_of(g_off + jj * nk + rb * rb_rows, rb_rows)
                    a = at_oth[pl.ds(r0, rb_rows), ts].astype(BF16)
                    act = (0.5 * a) * (1.0 + _erf(a * (2.0 ** -0.5)))
                    w_oth[pl.ds(r0, rb_rows), ts] = act * gs[jj]

    def substeps(do1, do2, do3):
        for sub in range(PEER_SUBSTEPS):
            at_cur, at_oth = at_refs[sub], at_refs[1 - sub]
            w_cur, w_oth = w_refs[sub], w_refs[1 - sub]
            cb = s * PEER_SUBSTEPS + sub - 1

            def mm_slice(ms, carry, sub=sub, at_cur=at_cur, at_oth=at_oth, w_cur=w_cur, w_oth=w_oth, cb=cb):
                if do1[sub]:
                    e_off = pl.multiple_of(ms * e_rows, e_rows)
                    u_off = pl.multiple_of((sub * ec + e_off) // 2, e_rows // 2)
                    u_rows = pltpu.bitcast(u_ref[pl.ds(u_off, e_rows // 2), :], BF16)
                    at_cur[pl.ds(e_off, e_rows), :] = _dot(u_rows, ht_ref[...])
                if do3[sub]:
                    d_off = pl.multiple_of(ms * d_rows, d_rows)
                    v_off = pl.multiple_of(d_off // 2, d_rows // 2)
                    v_rows = pltpu.bitcast(vt_ref[pl.ds(v_off, d_rows // 2), sub * ec:(sub + 1) * ec], BF16)
                    acc_ref[pl.ds(d_off, d_rows), :] += _dot(v_rows, w_cur[...])
                if do2[sub]:
                    for gi in range(gpm):
                        gate_group(ms * gpm + gi, at_oth, w_oth, cb)
                return carry

            lax.fori_loop(0, PEER_MM_SLICES, mm_slice, 0)

    first = s == 0
    last = s == n_steps - 1

    @pl.when(first)
    def _():
        substeps(do1=(True, True), do2=(False, True), do3=(False, False))

    @pl.when(jnp.logical_not(first | last))
    def _():
        substeps(do1=(True, True), do2=(True, True), do3=(True, True))

    @pl.when(last)
    def _():
        substeps(do1=(False, False), do2=(True, False), do3=(True, True))

    @pl.when(s == n_steps - 1)
    def _():
        for half in range(tm // TOK_TILE):
            g256 = t * (tm // TOK_TILE) + half
            row = jnp.where(g256 % tiles_per_batch < n_lat_tiles, g256 // tiles_per_batch, n_batch)
            g2 = ada_ref[pl.ds(row, 1), 5 * d:6 * d]
            rs = slice(half * TOK_TILE, (half + 1) * TOK_TILE)
            x2 = x_ref[rs, :] + g2 * acc_ref[:, rs].T
            if final:
                x2 = _rms(x2, fg_ref[...])
            o_ref[rs, :] = x2


def _peer(x1, h2, ada, final_g, wq, keys, u, vt, layer, tiles_per_batch, n_lat_tiles, n_batch, final):
    rows, d = x1.shape
    tm = PEER_TILE
    nk = keys.shape[2]
    n_i1 = PEER_CHUNK_I1
    ec = n_i1 * nk
    n_blocks = 2 * u.shape[1] // (PEER_SUBSTEPS * ec)
    n_sort = tm // PEER_SORT_W
    assert PEER_HEADS == SUBLANES
    full = lambda arr: pl.BlockSpec(arr.shape, lambda t, s: (0,) * arr.ndim, pipeline_mode=pl.Buffered(1))
    tok = pl.BlockSpec((tm, d), lambda t, s: (t, 0))
    return pl.pallas_call(
        functools.partial(_peer_kernel, tiles_per_batch, n_lat_tiles, n_batch, n_i1, final),
        grid=(rows // tm, n_blocks + 1),
        in_specs=[tok, tok, full(ada), full(final_g), full(wq), full(keys),
                  pl.BlockSpec((None, PEER_SUBSTEPS * ec // 2, d),
                               lambda t, s: (layer, jnp.minimum(s, n_blocks - 1), 0)),
                  pl.BlockSpec((None, d // 2, PEER_SUBSTEPS * ec), lambda t, s: (layer, 0, jnp.maximum(s - 1, 0)))],
        out_specs=tok,
        out_shape=jax.ShapeDtypeStruct((rows, d), F32),
        scratch_shapes=[pltpu.VMEM((PEER_HEADS, nk, tm), F32), pltpu.VMEM((PEER_HEADS, nk, tm), F32),
                        pltpu.VMEM((PEER_HEADS, nk, tm), F32), pltpu.VMEM((PEER_HEADS, nk, tm), F32),
                        pltpu.VMEM((PEER_HEADS, nk, tm), BF16), pltpu.VMEM((PEER_HEADS, nk, tm), BF16),
                        pltpu.VMEM((n_sort, PEER_TOPK, PEER_HEADS, PEER_SORT_W), F32),
                        pltpu.VMEM((n_sort, PEER_TOPK, PEER_HEADS, PEER_SORT_W), F32),
                        pltpu.VMEM((n_sort, PEER_TOPK, PEER_HEADS, PEER_SORT_W), F32),
                        pltpu.VMEM((n_sort, PEER_HEADS, PEER_SORT_W), F32),
                        pltpu.VMEM((ec, tm), F32), pltpu.VMEM((ec, tm), F32),
                        pltpu.VMEM((ec, tm), BF16), pltpu.VMEM((ec, tm), BF16),
                        pltpu.VMEM((d, tm), F32), pltpu.VMEM((d, tm), BF16)],
        compiler_params=_cparams(("parallel", "arbitrary")),
        name="peer",
    )(x1, h2, ada, final_g, wq, keys, u, vt)


def _rope_tables(seq, t):
    quarter = DA_QK // 4
    inv = ROPE_BASE ** (-np.arange(quarter, dtype=np.float32) / quarter)
    pos = np.arange(seq)
    ang_r = (pos // GRID_W).astype(np.float32)[:, None] * inv[None, :]
    ang_c = (pos % GRID_W).astype(np.float32)[:, None] * inv[None, :]
    half = np.concatenate([ang_r, ang_c, ang_r, ang_c], axis=1)
    cos = np.ones((t, LANES), np.float32)
    sin = np.zeros((t, LANES), np.float32)
    cos[:seq] = np.concatenate([np.cos(half), np.cos(half)], axis=1)
    sin[:seq] = np.concatenate([-np.sin(half), np.sin(half)], axis=1)
    return jnp.asarray(cos), jnp.asarray(sin)


def _pack_kernel(transpose, x_ref, o_ref):
    x = x_ref[0]
    if transpose:
        x = x.T
    o_ref[0] = pltpu.bitcast(x.astype(BF16), jnp.uint32)


def _pack_table(tab, transpose):
    depth, r, c = tab.shape
    rb = PACK_ROWS
    if transpose:
        in_spec = pl.BlockSpec((1, rb, c), lambda l, j: (l, j, 0))
        out_spec = pl.BlockSpec((1, c // 2, rb), lambda l, j: (l, 0, j))
        out_shape = (depth, c // 2, r)
    else:
        in_spec = pl.BlockSpec((1, rb, c), lambda l, j: (l, j, 0))
        out_spec = pl.BlockSpec((1, rb // 2, c), lambda l, j: (l, j, 0))
        out_shape = (depth, r // 2, c)
    return pl.pallas_call(
        functools.partial(_pack_kernel, transpose),
        grid=(depth, r // rb),
        in_specs=[in_spec],
        out_specs=out_spec,
        out_shape=jax.ShapeDtypeStruct(out_shape, jnp.uint32),
        compiler_params=_cparams(("parallel", "parallel")),
        name="pack_t" if transpose else "pack",
    )(tab)


def kernel(x, c, ctx, c_ctx, w_ada, b_ada, norm1_g, norm2_g, w_in, ml_gate_b, ml_conv_w, ml_norm_g, cv_conv_w,
           da_lam, da_subln_g, w_da, w_ml, w_cv, w_o, peer_wq, peer_keys, peer_u, peer_v, final_g):
    bsz, seq, d = x.shape
    n_ctx = ctx.shape[1]
    t = seq + n_ctx
    depth = w_ada.shape[0]
    tm = TOK_TILE
    assert seq % tm == 0 and n_ctx % tm == 0 and seq % ML_CHUNK == 0 and n_ctx % ML_CHUNK == 0
    assert (bsz * t) % PEER_TILE == 0 and (bsz * seq) % PEER_TILE == 0
    assert seq % ATTN_TILE == 0 and seq % n_ctx == 0
    n_lat_tiles = seq // tm
    n_tiles = t // tm

    rows = -(-(bsz + 1) // SUBLANES) * SUBLANES
    cc = jnp.zeros((rows, d), F32).at[:bsz].set(c).at[bsz].set(c_ctx)
    ada = _ada(cc, w_ada, b_ada)

    o = np.cumsum([0, DA_WIDTH, DA_WIDTH, DA_WIDTH, 2 * ML_WIDTH, ML_WIDTH, ML_WIDTH, 4 * ML_HEADS,
                   CV_WIDTH, CV_WIDTH, CV_WIDTH, 3 * d])
    quarter = DA_QK // 4
    wqk = w_in[:, :, o[0]:o[2]].reshape(depth, d, 2 * DA_HEADS, 2, 2, 2, quarter)
    wqk = wqk.transpose(0, 1, 2, 5, 3, 4, 6).reshape(depth, d, 2 * DA_WIDTH)
    wda_in = jnp.concatenate([wqk, w_in[:, :, o[2]:o[3]]], axis=-1).astype(BF16)
    wml_in = w_in[:, :, o[3]:o[6]].astype(BF16)
    wmg_in = jnp.pad(w_in[:, :, o[6]:o[7]], ((0, 0), (0, 0), (0, LANES - 4 * ML_HEADS))).astype(BF16)
    wcv_in = w_in[:, :, o[7]:o[10]].astype(BF16)
    wgt_in = w_in[:, :, o[10]:o[11]].astype(BF16)
    gate_b = jnp.pad(ml_gate_b.reshape(depth, 1, 4 * ML_HEADS), ((0, 0), (0, 0), (0, LANES - 4 * ML_HEADS)))
    lam_p = da_lam
    w_da_b, w_ml_b, w_cv_b, w_o_b = (a.astype(BF16) for a in (w_da, w_ml, w_cv, w_o))
    wq_b = peer_wq.astype(BF16)
    keys_b = peer_keys.astype(BF16)
    u_b = _pack_table(peer_u, transpose=False)
    vt_b = _pack_table(peer_v, transpose=True)
    cos, sin = _rope_tables(seq, t)

    xl, xc, xc_tile0 = x, ctx, 0
    for l in range(depth):
        need_ctx = l < depth - 1
        lam_init = 0.8 - 0.6 * math.exp(-0.3 * l)
        da, ml, mg, cv, gt = _inproj(xl, xc, xc_tile0, t, ada[l], norm1_g[l].reshape(1, d), cos, sin,
                                     wda_in[l], wml_in[l], wmg_in[l], wcv_in[l], wgt_in[l], n_lat_tiles)
        yda = _attention(da, lam_p[l], da_subln_g[l], lam_init, ATTN_TILE, 0, seq // ATTN_TILE, 0, t)
        if need_ctx:
            yda = _attention(da, lam_p[l], da_subln_g[l], lam_init, n_ctx, seq, 1, seq, n_ctx, prev=yda)
        yml = _mlstm(ml, mg, ml_conv_w[l], gate_b[l], ml_norm_g[l], seq // ML_CHUNK, n_ctx // ML_CHUNK)
        n_out_tiles = n_tiles if need_ctx else n_lat_tiles
        x1, h2 = _merge(xl, xc, xc_tile0, t, ada[l], norm2_g[l].reshape(1, d), yda, yml, cv, gt, cv_conv_w[l],
                        w_da_b[l], w_ml_b[l], w_cv_b[l], w_o_b[l], n_lat_tiles, n_out_tiles)
        t_out = n_out_tiles * tm
        xs = _peer(x1.reshape(bsz * t_out, d), h2.reshape(bsz * t_out, d), ada[l], final_g.reshape(1, d),
                   wq_b[l], keys_b[l], u_b, vt_b, l, n_out_tiles, n_lat_tiles, bsz, final=not need_ctx)
        xs = xs.reshape(bsz, t_out, d)
        xl, xc, xc_tile0 = xs, xs, n_lat_tiles
    return xs
```

```python
import functools
import math

import numpy as np
import jax
import jax.numpy as jnp
from jax import lax
from jax.experimental import pallas as pl
from jax.experimental.pallas import tpu as pltpu

EPS = 1e-6
GRID_W = 64
DA_HEADS = 4
DA_QK = 64
DA_V = 2 * DA_QK
DA_WIDTH = DA_HEADS * DA_V
ROPE_BASE = 10000.0
ML_HEADS = 4
ML_DH = 128
ML_WIDTH = ML_HEADS * ML_DH
ML_CHUNK = 128
CV_WIDTH = 512
PEER_HEADS = 8
PEER_TOPK = 16
PEER_GROUP = 2

LANES = 128
SUBLANES = 8
ATTN_TILE = 1024
TOK_TILE = 256
HALO = 16
PEER_TILE = 512
PEER_CHUNK_I1 = 8
PEER_SUBSTEPS = 2
PACK_ROWS = 1024
PEER_MM_SLICES = 2
PEER_SORT_W = 2 * LANES
VMEM_LIMIT = 56 * 1024 * 1024

NEG_BIG = -1e30
BF16 = jnp.bfloat16
F32 = jnp.float32
HIGHEST = lax.Precision.HIGHEST


def _cparams(sem):
    return pltpu.CompilerParams(dimension_semantics=sem, vmem_limit_bytes=VMEM_LIMIT)


def _dot(a, b):
    return jnp.dot(a, b, preferred_element_type=F32)


def _dot_nt(a, b):
    return lax.dot_general(a, b, (((1,), (1,)), ((), ())), preferred_element_type=F32)


def _dot_tn(a, b):
    return lax.dot_general(a, b, (((0,), (0,)), ((), ())), preferred_element_type=F32)


def _rms(x, g):
    return x * lax.rsqrt(jnp.mean(x * x, axis=-1, keepdims=True) + EPS) * g


def _sigmoid(x):
    return 1.0 / (1.0 + jnp.exp(-x))


def _log_sigmoid(x):
    return jnp.minimum(x, 0.0) - jnp.log1p(jnp.exp(-jnp.abs(x)))


def _ada_kernel(c_ref, w_ref, b_ref, o_ref):
    c = c_ref[...]
    s = c * _sigmoid(c)
    o_ref[0] = jnp.dot(s, w_ref[0], preferred_element_type=F32, precision=HIGHEST) + b_ref[0]


def _ada(cc, w_ada, b_ada):
    depth, d, d6 = w_ada.shape
    rows = cc.shape[0]
    nb = d6 // d
    return pl.pallas_call(
        _ada_kernel,
        grid=(depth, nb),
        in_specs=[pl.BlockSpec((rows, d), lambda l, j: (0, 0)),
                  pl.BlockSpec((1, d, d), lambda l, j: (l, 0, j)),
                  pl.BlockSpec((1, 1, d), lambda l, j: (l, 0, j))],
        out_specs=pl.BlockSpec((1, rows, d), lambda l, j: (l, 0, j)),
        out_shape=jax.ShapeDtypeStruct((depth, rows, d6), F32),
        compiler_params=_cparams(("parallel", "parallel")),
        name="ada",
    )(cc, w_ada, b_ada.reshape(depth, 1, d6))


def _inproj_kernel(n_lat_tiles, n_batch, xl_ref, xc_ref, ada_ref, g_ref, cos_ref, sin_ref,
                   wda_ref, wml_ref, wmg_ref, wcv_ref, wgt_ref,
                   da_ref, ml_ref, mg_ref, cv_ref, gt_ref):
    b = pl.program_id(0)
    i = pl.program_id(1)
    d = xl_ref.shape[-1]
    is_lat = i < n_lat_tiles
    row = jnp.where(is_lat, b, n_batch)
    shift = ada_ref[pl.ds(row, 1), 0:d]
    scale = ada_ref[pl.ds(row, 1), d:2 * d]
    x = jnp.where(is_lat, xl_ref[0], xc_ref[0])
    h = _rms(x, g_ref[...]) * (1.0 + scale) + shift
    hb = h.astype(BF16)

    zda = _dot(hb, wda_ref[...])
    cos = cos_ref[...]
    sin = sin_ref[...]
    for j in range(2 * DA_HEADS):
        blk = zda[:, j * LANES:(j + 1) * LANES]
        rot = blk * cos + pltpu.roll(blk, LANES // 2, 1) * sin
        da_ref[0, :, j * LANES:(j + 1) * LANES] = rot.astype(BF16)
    da_ref[0, :, 2 * DA_WIDTH:] = zda[:, 2 * DA_WIDTH:].astype(BF16)

    ml_ref[0] = _dot(hb, wml_ref[...]).astype(BF16)
    mg_ref[0] = _dot(hb, wmg_ref[...])

    zcv = _dot(hb, wcv_ref[...])
    cv_ref[0, :, 0:CV_WIDTH] = zcv[:, 0:CV_WIDTH].astype(BF16)
    cv_ref[0, :, CV_WIDTH:] = (zcv[:, CV_WIDTH:2 * CV_WIDTH] * zcv[:, 2 * CV_WIDTH:]).astype(BF16)

    for j in range(3):
        zg = _dot(hb, wgt_ref[:, j * d:(j + 1) * d])
        gt_ref[0, :, j * d:(j + 1) * d] = _sigmoid(zg).astype(BF16)


def _stream_specs(tm, d, n_lat_tiles, n_tiles, xc_tile0):
    n_ctx_tiles = n_tiles - n_lat_tiles
    lat = pl.BlockSpec((1, tm, d), lambda b, i: (b, jnp.minimum(i, n_lat_tiles - 1), 0))
    ctx = pl.BlockSpec((1, tm, d), lambda b, i: (b, xc_tile0 + jnp.clip(i - n_lat_tiles, 0, n_ctx_tiles - 1), 0))
    return lat, ctx


def _inproj(xl, xc, xc_tile0, t, ada, g, cos, sin, wda, wml, wmg, wcv, wgt, n_lat_tiles):
    bsz, _, d = xl.shape
    tm = TOK_TILE
    full = lambda arr: pl.BlockSpec(arr.shape, lambda b, i: (0,) * arr.ndim, pipeline_mode=pl.Buffered(1))
    tok = lambda w: pl.BlockSpec((1, tm, w), lambda b, i: (b, i, 0))
    lat_spec, ctx_spec = _stream_specs(tm, d, n_lat_tiles, t // tm, xc_tile0)
    widths =(wda.shape[1], wml.shape[1], wmg.shape[1], 2 * CV_WIDTH, wgt.shape[1])
    dts = (BF16, BF16, F32, BF16, BF16)
    return pl.pallas_call(
        functools.partial(_inproj_kernel, n_lat_tiles, bsz),
        grid=(bsz, t // tm),
        in_specs=[lat_spec, ctx_spec, full(ada), full(g),
                  pl.BlockSpec((tm, LANES), lambda b, i: (i, 0)),
                  pl.BlockSpec((tm, LANES), lambda b, i: (i, 0)),
                  full(wda), full(wml), full(wmg), full(wcv), full(wgt)],
        out_specs=[tok(w) for w in widths],
        out_shape=[jax.ShapeDtypeStruct((bsz, t, w), dt) for w, dt in zip(widths, dts)],
        compiler_params=_cparams(("parallel", "parallel")),
        name="inproj",
    )(xl, xc, ada, g, cos, sin, wda, wml, wmg, wcv, wgt)


def _attn_kernel(lam_init, q_ref, k_ref, v_ref, lam_ref, g_ref, o_ref, va_ref):
    i = pl.program_id(2)
    lv = lam_ref[...]
    lam = (jnp.exp(jnp.sum(lv[0:1] * lv[1:2], axis=-1, keepdims=True))
           - jnp.exp(jnp.sum(lv[2:3] * lv[3:4], axis=-1, keepdims=True)) + lam_init)
    lane = lax.broadcasted_iota(jnp.int32, (1, LANES), 1)
    map0 = (lane % (LANES // 2)) < (DA_QK // 2)

    @pl.when(i == 0)
    def _():
        va_ref[:, 0:LANES] = v_ref[0]
        va_ref[:, LANES:] = jnp.ones((va_ref.shape[0], LANES), BF16)

    k = k_ref[0]
    va = va_ref[...]
    q = q_ref[0].astype(F32) * (DA_QK ** -0.5)
    q0 = jnp.where(map0, q, 0.0).astype(BF16)
    q1 = jnp.where(map0, 0.0, q).astype(BF16)
    s0 = _dot_nt(q0, k)
    s1 = _dot_nt(q1, k)
    e0 = jnp.exp((s0 - jnp.max(s0, axis=-1, keepdims=True)).astype(BF16))
    e1 = jnp.exp((s1 - jnp.max(s1, axis=-1, keepdims=True)).astype(BF16))
    o0 = _dot(e0, va)
    o1 = _dot(e1, va)
    o = o0[:, 0:LANES] / o0[:, LANES:] - lam * (o1[:, 0:LANES] / o1[:, LANES:])
    o_ref[0] = (_rms(o, g_ref[...]) * (1.0 - lam_init)).astype(o_ref.dtype)


def _attention(da, lam_p, subln_g, lam_init, tq, q_row0, n_q, kv_row0, kv_rows):
    bsz = da.shape[0]
    q0 = q_row0 // tq
    kv0 = kv_row0 // kv_rows
    return pl.pallas_call(
        functools.partial(_attn_kernel, lam_init),
        grid=(bsz, DA_HEADS, n_q),
        in_specs=[pl.BlockSpec((1, tq, LANES), lambda b, h, i: (b, q0 + i, h)),
                  pl.BlockSpec((1, kv_rows, LANES), lambda b, h, i: (b, kv0, DA_HEADS + h)),
                  pl.BlockSpec((1, kv_rows, LANES), lambda b, h, i: (b, kv0, 2 * DA_HEADS + h)),
                  pl.BlockSpec(lam_p.shape, lambda b, h, i: (0, 0)),
                  pl.BlockSpec((1, LANES), lambda b, h, i: (0, 0))],
        out_specs=pl.BlockSpec((1, tq, LANES), lambda b, h, i: (b, i, h)),
        out_shape=jax.ShapeDtypeStruct((bsz, n_q * tq, DA_WIDTH), BF16),
        scratch_shapes=[pltpu.VMEM((kv_rows, 2 * LANES), BF16)],
        compiler_params=_cparams(("parallel", "parallel", "arbitrary")),
        name="diffattn",
    )(da, da, da, lam_p, subln_g.reshape(1, LANES))


def _mlstm_kernel(ncl, ncc, ml_ref, mg_ref, cw_ref, gb_ref, ng_ref, o_ref,
                  qs_ref, ks_ref, hf_ref, hb_ref, c_ref, n_ref, m_ref):
    L = ML_CHUNK
    nc = ncl + ncc
    t_all = nc * L
    seq = ncl * L
    w = ML_WIDTH

    trow = lax.broadcasted_iota(jnp.int32, (t_all, 1), 0)
    first = (trow == 0) | (trow == seq)
    last = (trow == seq - 1) | (trow == t_all - 1)
    for j in range(2 * ML_HEADS):
        x = ml_ref[0, :, j * LANES:(j + 1) * LANES].astype(F32)
        xm = jnp.where(first, 0.0, pltpu.roll(x, 1, 0))
        xp = jnp.where(last, 0.0, pltpu.roll(x, t_all - 1, 0))
        cw = cw_ref[:, j * LANES:(j + 1) * LANES]
        y = xm * cw[0:1] + x * cw[1:2] + xp * cw[2:3]
        y = y * _sigmoid(y)
        if j < ML_HEADS:
            qs_ref[:, j * LANES:(j + 1) * LANES] = y.astype(BF16)
        else:
            jj = j - ML_HEADS
            ks_ref[:, jj * LANES:(jj + 1) * LANES] = (y * (ML_DH ** -0.5)).astype(BF16)

    c_ref[...] = jnp.zeros_like(c_ref)
    n_ref[...] = jnp.zeros_like(n_ref)
    m_ref[...] = jnp.zeros_like(m_ref)

    r_i = lax.broadcasted_iota(jnp.int32, (L, L), 0)
    c_i = lax.broadcasted_iota(jnp.int32, (L, L), 1)
    lower = r_i >= c_i
    upper = r_i <= c_i
    lower_f = lower.astype(F32)
    upper_f = upper.astype(F32)
    ones_blk = jnp.ones((L, LANES), BF16)

    def step(it, carry):
        dirs = []
        for dr in range(2):
            if dr == 0:
                ch = (it + ncl) % nc
                seen, tri_col, tri_row, end, h_ref = lower, lower_f, upper_f, L - 1, hf_ref
            else:
                ch = nc - 1 - it
                seen, tri_col, tri_row, end, h_ref = upper, upper_f, lower_f, 0, hb_ref
            r0 = pl.multiple_of(ch * L, L)
            g = mg_ref[0, pl.ds(r0, L), :] + gb_ref[...]
            gt = g.T
            bcol_all = jnp.dot(tri_col, _log_sigmoid(g), preferred_element_type=F32, precision=HIGHEST)
            brow_all = jnp.dot(_log_sigmoid(gt), tri_row, preferred_element_type=F32, precision=HIGHEST)
            dirs.append((seen, end, h_ref, r0, g, gt, bcol_all, brow_all))

        rec = []
        for dr in range(2):
            seen, end, h_ref, r0, g, gt, bcol_all, brow_all = dirs[dr]
            for h in range(ML_HEADS):
                icol = (2 * dr) * ML_HEADS + h
                fcol = (2 * dr + 1) * ML_HEADS + h
                idx = dr * ML_HEADS + h
                rec.append(dict(
                    seen=seen, end=end, h_ref=h_ref, r0=r0, idx=idx, cols=slice(h * LANES, (h + 1) * LANES),
                    q=qs_ref[pl.ds(r0, L), h * LANES:(h + 1) * LANES],
                    k=ks_ref[pl.ds(r0, L), h * LANES:(h + 1) * LANES],
                    v=ml_ref[0, pl.ds(r0, L), 2 * w + h * LANES:2 * w + (h + 1) * LANES],
                    b_col=bcol_all[:, fcol:fcol + 1], ig_col=g[:, icol:icol + 1],
                    b_row=brow_all[fcol:fcol + 1, :], ig_row=gt[icol:icol + 1, :],
                    m_prev=m_ref[idx][:, 0:1], cmat=c_ref[idx], nrow=n_ref[idx]))

        for c in rec:
            c["b_cb"] = jnp.broadcast_to(c["b_col"], (L, L))
            c["log_d"] = jnp.where(c["seen"], c["b_cb"] - c["b_row"] + c["ig_row"], NEG_BIG)
        for c in rec:
            c["m_path"] = c["b_col"] + c["m_prev"]
            c["m_t"] = jnp.maximum(c["m_path"], jnp.max(c["log_d"], axis=-1, keepdims=True))
        for c in rec:
            c["m_tb"] = jnp.broadcast_to(c["m_t"], (L, L))
            c["dmat"] = jnp.exp(c["log_d"] - c["m_tb"])
            c["carry_b"] = jnp.exp((c["b_cb"] + c["m_prev"]) - c["m_tb"])
        for c in rec:
            wmat = _dot_nt(c["q"], c["k"]) * c["dmat"]
            wv = _dot(wmat.astype(BF16), jnp.concatenate([c["v"], ones_blk], axis=1))
            qn = _dot_nt(c["q"], jnp.broadcast_to(c["nrow"], (L, LANES)).astype(BF16))
            num = wv[:, 0:LANES] + c["carry_b"] * _dot(c["q"], c["cmat"].astype(BF16))
            den = wv[:, LANES:] + c["carry_b"] * qn
            hval = num / jnp.maximum(jnp.abs(den), jnp.exp(-c["m_tb"]))
            c["h_ref"][pl.ds(c["r0"], L), c["cols"]] = hval
        for c in rec:
            b_end = c["b_col"][c["end"]:c["end"] + 1, :]
            log_w = b_end - c["b_col"] + c["ig_col"]
            m_new = jnp.maximum(b_end + c["m_prev"], jnp.max(log_w, axis=0, keepdims=True))
            c["wk"] = jnp.exp(log_w - m_new)
            c["decay"] = jnp.exp(b_end + c["m_prev"] - m_new)
            c["m_new"] = m_new
        for c in rec:
            kw = c["k"].astype(F32) * c["wk"]
            idx = c["idx"]
            c_ref[idx] = c["decay"] * c["cmat"] + _dot_tn(kw.astype(BF16), c["v"])
            n_ref[idx] = c["decay"] * c["nrow"] + jnp.sum(kw, axis=0, keepdims=True)
            m_ref[idx] = jnp.broadcast_to(c["m_new"], (1, LANES))
        return carry

    lax.fori_loop(0, nc, step, 0)

    for h in range(ML_HEADS):
        sl = slice(h * LANES, (h + 1) * LANES)
        hs = hf_ref[:, sl] + hb_ref[:, sl]
        y = _rms(hs, ng_ref[:, sl])
        og = ml_ref[0, :, 3 * w + h * LANES:3 * w + (h + 1) * LANES].astype(F32)
        o_ref[0, :, sl] = (y * _sigmoid(og)).astype(o_ref.dtype)


def _mlstm(ml, mg, conv_w, gate_b_row, norm_g, ncl, ncc):
    bsz, t, wtot = ml.shape
    nstate = 2 * ML_HEADS
    return pl.pallas_call(
        functools.partial(_mlstm_kernel, ncl, ncc),
        grid=(bsz,),
        in_specs=[pl.BlockSpec((1, t, wtot), lambda b: (b, 0, 0)),
                  pl.BlockSpec((1, t, LANES), lambda b: (b, 0, 0)),
                  pl.BlockSpec(conv_w.shape, lambda b: (0, 0)),
                  pl.BlockSpec((1, LANES), lambda b: (0, 0)),
                  pl.BlockSpec((1, ML_WIDTH), lambda b: (0, 0))],
        out_specs=pl.BlockSpec((1, t, ML_WIDTH), lambda b: (b, 0, 0)),
        out_shape=jax.ShapeDtypeStruct((bsz, t, ML_WIDTH), BF16),
        scratch_shapes=[pltpu.VMEM((t, ML_WIDTH), BF16), pltpu.VMEM((t, ML_WIDTH), BF16),
                        pltpu.VMEM((t, ML_WIDTH), F32), pltpu.VMEM((t, ML_WIDTH), F32),
                        pltpu.VMEM((nstate, ML_DH, ML_DH), F32),
                        pltpu.VMEM((nstate, 1, ML_DH), F32),
                        pltpu.VMEM((nstate, 1, LANES), F32)],
        compiler_params=_cparams(("parallel",)),
        name="mlstm",
    )(ml, mg, conv_w, gate_b_row, norm_g.reshape(1, ML_WIDTH))


def _merge_kernel(n_lat_tiles, n_batch, xl_ref, xc_ref, ada_ref, g2_ref, ydal_ref, ydac_ref, yml_ref, cv_ref, pprev_ref, pnext_ref,
                  gt_ref, cw_ref, wda_ref, wml_ref, wcv_ref, wo_ref, x1_ref, h2_ref):
    b = pl.program_id(0)
    i = pl.program_id(1)
    n_tiles = pl.num_programs(1)
    d = xl_ref.shape[-1]
    tm = xl_ref.shape[1]
    is_lat = i < n_lat_tiles
    row = jnp.where(is_lat, b, n_batch)

    bg = cv_ref[0, :, 0:CV_WIDTH].astype(F32)
    p = cv_ref[0, :, CV_WIDTH:].astype(F32)
    seq_start = (i == 0) | (i == n_lat_tiles)
    seq_end = (i == n_lat_tiles - 1) | (i == n_tiles - 1)
    prev_row = jnp.where(seq_start, 0.0, pprev_ref[0, HALO - 1:HALO, :].astype(F32))
    next_row = jnp.where(seq_end, 0.0, pnext_ref[0, 0:1, :].astype(F32))
    trow = lax.broadcasted_iota(jnp.int32, (tm, 1), 0)
    pm = jnp.where(trow == 0, prev_row, pltpu.roll(p, 1, 0))
    pp = jnp.where(trow == tm - 1, next_row, pltpu.roll(p, tm - 1, 0))
    cw = cw_ref[...]
    ycv = bg * (pm * cw[0:1] + p * cw[1:2] + pp * cw[2:3])

    yda = jnp.where(is_lat, ydal_ref[0], ydac_ref[0])
    m = (gt_ref[0, :, 0:d].astype(F32) * _dot(yda, wda_ref[...])
         + gt_ref[0, :, d:2 * d].astype(F32) * _dot(yml_ref[0], wml_ref[...])
         + gt_ref[0, :, 2 * d:].astype(F32) * _dot(ycv.astype(BF16), wcv_ref[...]))
    out = _dot(m.astype(BF16), wo_ref[...])
    g1 = ada_ref[pl.ds(row, 1), 2 * d:3 * d]
    x1 = jnp.where(is_lat, xl_ref[0], xc_ref[0]) + g1 * out
    x1_ref[0] = x1
    sh2 = ada_ref[pl.ds(row, 1), 3 * d:4 * d]
    sc2 = ada_ref[pl.ds(row, 1), 4 * d:5 * d]
    h2_ref[0] = (_rms(x1, g2_ref[...]) * (1.0 + sc2) + sh2).astype(BF16)


def _merge(xl, xc, xc_tile0, t, ada, g2, yda, yda_c, yml, cv, gt, cv_w, wda, wml, wcv, wo, n_lat_tiles, n_tiles):
    bsz, _, d = xl.shape
    tm = TOK_TILE
    lat_spec, ctx_spec = _stream_specs(tm, d, n_lat_tiles, t // tm, xc_tile0)
    yl_spec, yc_spec = _stream_specs(tm, DA_WIDTH, n_lat_tiles, t // tm, 0)
    r = tm // HALO
    nrb = t // HALO
    full = lambda arr: pl.BlockSpec(arr.shape, lambda b, i: (0,) * arr.ndim, pipeline_mode=pl.Buffered(1))
    tok = lambda w: pl.BlockSpec((1, tm, w), lambda b, i: (b, i, 0))
    return pl.pallas_call(
        functools.partial(_merge_kernel, n_lat_tiles, bsz),
        grid=(bsz, n_tiles),
        in_specs=[lat_spec, ctx_spec, full(ada), full(g2), yl_spec, yc_spec, tok(ML_WIDTH), tok(2 * CV_WIDTH),
                  pl.BlockSpec((1, HALO, CV_WIDTH), lambda b, i: (b, jnp.maximum(i * r - 1, 0), 1)),
                  pl.BlockSpec((1, HALO, CV_WIDTH), lambda b, i: (b, jnp.minimum((i + 1) * r, nrb - 1), 1)),
                  tok(3 * d), full(cv_w), full(wda), full(wml), full(wcv), full(wo)],
        out_specs=[tok(d), tok(d)],
        out_shape=[jax.ShapeDtypeStruct((bsz, n_tiles * tm, d), F32),
                   jax.ShapeDtypeStruct((bsz, n_tiles * tm, d), BF16)],
        compiler_params=_cparams(("parallel", "parallel")),
        name="merge",
    )(xl, xc, ada, g2, yda, yda_c, yml, cv, cv, cv, gt, cv_w, wda, wml, wcv, wo)


def _cmpx(v, i, j):
    a, b = v[i], v[j]
    v[i] = jnp.maximum(a, b)
    v[j] = jnp.minimum(a, b)


def _bitonic_merge_desc(v):
    n = len(v)
    j = n // 2
    while j >= 1:
        for i in range(n):
            l = i ^ j
            if l > i:
                _cmpx(v, i, l)
        j //= 2


def _bitonic_sort_desc(v):
    n = len(v)
    k = 2
    while k <= n:
        j = k // 2
        while j >= 1:
            for i in range(n):
                l = i ^ j
                if l > i:
                    if (i & k) == 0:
                        _cmpx(v, i, l)
                    else:
                        _cmpx(v, l, i)
            j //= 2
        k *= 2


def _merge_top(a, b):
    n = len(a)
    c = [jnp.maximum(a[r], b[n - 1 - r]) for r in range(n)]
    _bitonic_merge_desc(c)
    return c


def _top16_sorted(s):
    nk, tm = s.shape
    k = PEER_TOPK
    groups = nk // k
    assert groups == SUBLANES
    a = s.reshape(k, SUBLANES, tm)
    v = [a[r] for r in range(k)]
    _bitonic_sort_desc(v)
    sh = SUBLANES // 2
    while sh >= 1:
        v = _merge_top(v, [pltpu.roll(x, sh, 0) for x in v])
        sh //= 2
    return v


def _erf(x):
    return lax.erf(x)


def _peer_kernel(tiles_per_batch, n_lat_tiles, n_batch, n_i1, final, x_ref, h_ref, ada_ref, fg_ref, wq_ref, keys_ref,
                 u_ref, vt_ref, o_ref, s1_ref, s2_ref, f1_ref, cnt_ref, e2_ref, rk2_ref, t1_ref, t2_ref, cr_ref, rz_ref,
                 at0_ref, at1_ref, w0_ref, w1_ref, acc_ref, ht_ref):
    t = pl.program_id(0)
    s = pl.program_id(1)
    n_steps = pl.num_programs(1)
    n_chunks = (n_steps - 1) * PEER_SUBSTEPS
    d = x_ref.shape[-1]
    tm = x_ref.shape[0]
    nk = keys_ref.shape[2]
    dsub = keys_ref.shape[3]
    k = PEER_TOPK
    sort_w = PEER_SORT_W
    at_refs = (at0_ref, at1_ref)
    w_refs = (w0_ref, w1_ref)
    zero_b = jnp.zeros((), BF16)

    @pl.when(s == 0)
    def _():
        hb = h_ref[...]
        for half in range(tm // TOK_TILE):
            cs = slice(half * TOK_TILE, (half + 1) * TOK_TILE)
            ht_ref[:, cs] = hb[cs, :].astype(F32).T.astype(BF16)
        q = _dot(hb, wq_ref[...]).astype(BF16)
        for h in range(PEER_HEADS):
            for p in range(2):
                col = (h * 2 + p) * dsub
                (s1_ref if p == 0 else s2_ref)[h] = _dot_nt(keys_ref[h, p], q[:, col:col + dsub])
        neg = jnp.full((SUBLANES, sort_w), -jnp.inf, F32)
        for blk in range(tm // sort_w):
            ls = slice(blk * sort_w, (blk + 1) * sort_w)

            def sort_head(h, carry):
                for sref, tref in ((s1_ref, t1_ref), (s2_ref, t2_ref)):
                    v = _top16_sorted(sref[h, :, ls])
                    for r in range(k):
                        tref[blk, r, pl.ds(h, 1), :] = v[r][0:1, :]
                return carry

            lax.fori_loop(0, PEER_HEADS, sort_head, 0)
            t1 = [t1_ref[blk, r] for r in range(k)]
            t2 = [t2_ref[blk, r] for r in range(k)]
            rows = [[t1[r1] + t2[r2] for r2 in range(k // (r1 + 1))] for r1 in range(k)]
            best = rows[0]
            rest = [x for rw in rows[1:] for x in rw]
            while rest:
                grp, rest = rest[:k], rest[k:]
                grp = grp + [neg] * (k - len(grp))
                _bitonic_sort_desc(grp)
                best = _merge_top(best, grp)
            mx = best[0]
            z = jnp.exp(best[0] - mx)
            for r in range(1, k):
                z = z + jnp.exp(best[r] - mx)
            rz = 1.0 / z
            tau = best[k - 1]
            for r1 in range(k):
                cnt = jnp.where(rows[r1][0] >= tau, 1.0, 0.0)
                for r2 in range(1, k // (r1 + 1)):
                    cnt = cnt + jnp.where(rows[r1][r2] >= tau, 1.0, 0.0)
                cr_ref[blk, r1] = cnt
            rz_ref[blk] = rz

            def finish_head(h, carry):
                hs = pl.ds(h, 1)
                s1h = s1_ref[h, :, ls]
                s2h = s2_ref[h, :, ls]
                cn = jnp.zeros_like(s1h)
                rk = jnp.full_like(s2h, float(k))
                for r in reversed(range(k)):
                    cn = jnp.where(s1h == t1_ref[blk, r, hs, :], cr_ref[blk, r, hs, :], cn)
                    rk = jnp.where(s2h >= t2_ref[blk, r, hs, :], float(r), rk)
                cnt_ref[h, :, ls] = cn
                rk2_ref[h, :, ls] = rk.astype(BF16)
                f1_ref[h, :, ls] = jnp.exp(s1h - t1_ref[blk, 0, hs, :]) * rz_ref[blk, hs, :]
                e2_ref[h, :, ls] = jnp.exp(s2h - t2_ref[blk, 0, hs, :]).astype(BF16)
                return carry

            lax.fori_loop(0, PEER_HEADS, finish_head, 0)
        acc_ref[...] = jnp.zeros_like(acc_ref)

    ec = n_i1 * nk
    rb_rows = nk // 2
    n_grp = n_i1 // PEER_GROUP
    g_rows = ec // n_grp
    gpm = n_grp // PEER_MM_SLICES
    d_rows = d // PEER_MM_SLICES
    e_rows = ec // PEER_MM_SLICES

    def gate_group(grp, at_oth, w_oth, cb):
        g_off = pl.multiple_of(grp * g_rows, g_rows)
        i1s = [cb * n_i1 + grp * PEER_GROUP + jj for jj in range(PEER_GROUP)]
        cnrows = [[cnt_ref[h, pl.ds(i1, 1), :].astype(BF16) for i1 in i1s] for h in range(PEER_HEADS)]
        f1rows = [[f1_ref[h, pl.ds(i1, 1), :].astype(BF16) for i1 in i1s] for h in range(PEER_HEADS)]
        for tb in range(tm // LANES):
            ts = slice(tb * LANES, (tb + 1) * LANES)
            for rb in range(nk // rb_rows):
                rs = slice(rb * rb_rows, (rb + 1) * rb_rows)
                gs = [jnp.zeros((rb_rows, LANES), BF16) for _ in range(PEER_GROUP)]
                for h in range(PEER_HEADS):
                    rk2 = rk2_ref[h, rs, ts]
                    e2b = e2_ref[h, rs, ts]
                    for jj in range(PEER_GROUP):
                        cnr = cnrows[h][jj][:, ts]
                        f1r = f1rows[h][jj][:, ts]
                        gs[jj] = gs[jj] + jnp.where(rk2 < cnr, e2b * f1r, zero_b)
                for jj in range(PEER_GROUP):
                    r0 = pl.multiple_of(g_off + jj * nk + rb * rb_rows, rb_rows)
                    a = at_oth[pl.ds(r0, rb_rows), ts].astype(BF16)
                    act = (0.5 * a) * (1.0 + _erf(a * (2.0 ** -0.5)))
                    w_oth[pl.ds(r0, rb_rows), ts] = act * gs[jj]

    def substeps(do1, do2, do3):
        for sub in range(PEER_SUBSTEPS):
            at_cur, at_oth = at_refs[sub], at_refs[1 - sub]
            w_cur, w_oth = w_refs[sub], w_refs[1 - sub]
            cb = s * PEER_SUBSTEPS + sub - 1

            def mm_slice(ms, carry, sub=sub, at_cur=at_cur, at_oth=at_oth, w_cur=w_cur, w_oth=w_oth, cb=cb):
                if do1[sub]:
                    e_off = pl.multiple_of(ms * e_rows, e_rows)
                    u_off = pl.multiple_of((sub * ec + e_off) // 2, e_rows // 2)
                    u_rows = pltpu.bitcast(u_ref[pl.ds(u_off, e_rows // 2), :], BF16)
                    at_cur[pl.ds(e_off, e_rows), :] = _dot(u_rows, ht_ref[...])
                if do3[sub]:
                    d_off = pl.multiple_of(ms * d_rows, d_rows)
                    v_off = pl.multiple_of(d_off // 2, d_rows // 2)
                    v_rows = pltpu.bitcast(vt_ref[pl.ds(v_off, d_rows // 2), sub * ec:(sub + 1) * ec], BF16)
                    acc_ref[pl.ds(d_off, d_rows), :] += _dot(v_rows, w_cur[...])
                if do2[sub]:
                    for gi in range(gpm):
                        gate_group(ms * gpm + gi, at_oth, w_oth, cb)
                return carry

            lax.fori_loop(0, PEER_MM_SLICES, mm_slice, 0)

    first = s == 0
    last = s == n_steps - 1

    @pl.when(first)
    def _():
        substeps(do1=(True, True), do2=(False, True), do3=(False, False))

    @pl.when(jnp.logical_not(first | last))
    def _():
        substeps(do1=(True, True), do2=(True, True), do3=(True, True))

    @pl.when(last)
    def _():
        substeps(do1=(False, False), do2=(True, False), do3=(True, True))

    @pl.when(s == n_steps - 1)
    def _():
        for half in range(tm // TOK_TILE):
            g256 = t * (tm // TOK_TILE) + half
            row = jnp.where(g256 % tiles_per_batch < n_lat_tiles, g256 // tiles_per_batch, n_batch)
            g2 = ada_ref[pl.ds(row, 1), 5 * d:6 * d]
            rs = slice(half * TOK_TILE, (half + 1) * TOK_TILE)
            x2 = x_ref[rs, :] + g2 * acc_ref[:, rs].T
            if final:
                x2 = _rms(x2, fg_ref[...])
            o_ref[rs, :] = x2


def _peer(x1, h2, ada, final_g, wq, keys, u, vt, layer, tiles_per_batch, n_lat_tiles, n_batch, final):
    rows, d = x1.shape
    tm = PEER_TILE
    nk = keys.shape[2]
    n_i1 = PEER_CHUNK_I1
    ec = n_i1 * nk
    n_blocks = 2 * u.shape[1] // (PEER_SUBSTEPS * ec)
    n_sort = tm // PEER_SORT_W
    assert PEER_HEADS == SUBLANES
    full = lambda arr: pl.BlockSpec(arr.shape, lambda t, s: (0,) * arr.ndim, pipeline_mode=pl.Buffered(1))
    tok = pl.BlockSpec((tm, d), lambda t, s: (t, 0))
    return pl.pallas_call(
        functools.partial(_peer_kernel, tiles_per_batch, n_lat_tiles, n_batch, n_i1, final),
        grid=(rows // tm, n_blocks + 1),
        in_specs=[tok, tok, full(ada), full(final_g), full(wq), full(keys),
                  pl.BlockSpec((None, PEER_SUBSTEPS * ec // 2, d),
                               lambda t, s: (layer, jnp.minimum(s, n_blocks - 1), 0)),
                  pl.BlockSpec((None, d // 2, PEER_SUBSTEPS * ec), lambda t, s: (layer, 0, jnp.maximum(s - 1, 0)))],
        out_specs=tok,
        out_shape=jax.ShapeDtypeStruct((rows, d), F32),
        scratch_shapes=[pltpu.VMEM((PEER_HEADS, nk, tm), F32), pltpu.VMEM((PEER_HEADS, nk, tm), F32),
                        pltpu.VMEM((PEER_HEADS, nk, tm), F32), pltpu.VMEM((PEER_HEADS, nk, tm), F32),
                        pltpu.VMEM((PEER_HEADS, nk, tm), BF16), pltpu.VMEM((PEER_HEADS, nk, tm), BF16),
                        pltpu.VMEM((n_sort, PEER_TOPK, PEER_HEADS, PEER_SORT_W), F32),
                        pltpu.VMEM((n_sort, PEER_TOPK, PEER_HEADS, PEER_SORT_W), F32),
                        pltpu.VMEM((n_sort, PEER_TOPK, PEER_HEADS, PEER_SORT_W), F32),
                        pltpu.VMEM((n_sort, PEER_HEADS, PEER_SORT_W), F32),
                        pltpu.VMEM((ec, tm), F32), pltpu.VMEM((ec, tm), F32),
                        pltpu.VMEM((ec, tm), BF16), pltpu.VMEM((ec, tm), BF16),
                        pltpu.VMEM((d, tm), F32), pltpu.VMEM((d, tm), BF16)],
        compiler_params=_cparams(("parallel", "arbitrary")),
        name="peer",
    )(x1, h2, ada, final_g, wq, keys, u, vt)


def _rope_tables(seq, t):
    quarter = DA_QK // 4
    inv = ROPE_BASE ** (-np.arange(quarter, dtype=np.float32) / quarter)
    pos = np.arange(seq)
    ang_r = (pos // GRID_W).astype(np.float32)[:, None] * inv[None, :]
    ang_c = (pos % GRID_W).astype(np.float32)[:, None] * inv[None, :]
    half = np.concatenate([ang_r, ang_c, ang_r, ang_c], axis=1)
    cos = np.ones((t, LANES), np.float32)
    sin = np.zeros((t, LANES), np.float32)
    cos[:seq] = np.concatenate([np.cos(half), np.cos(half)], axis=1)
    sin[:seq] = np.concatenate([-np.sin(half), np.sin(half)], axis=1)
    return jnp.asarray(cos), jnp.asarray(sin)


def _pack_kernel(transpose, x_ref, o_ref):
    x = x_ref[0]
    if transpose:
        x = x.T
    o_ref[0] = pltpu.bitcast(x.astype(BF16), jnp.uint32)


def _pack_table(tab, transpose):
    depth, r, c = tab.shape
    rb = PACK_ROWS
    if transpose:
        in_spec = pl.BlockSpec((1, rb, c), lambda l, j: (l, j, 0))
        out_spec = pl.BlockSpec((1, c // 2, rb), lambda l, j: (l, 0, j))
        out_shape = (depth, c // 2, r)
    else:
        in_spec = pl.BlockSpec((1, rb, c), lambda l, j: (l, j, 0))
        out_spec = pl.BlockSpec((1, rb // 2, c), lambda l, j: (l, j, 0))
        out_shape = (depth, r // 2, c)
    return pl.pallas_call(
        functools.partial(_pack_kernel, transpose),
        grid=(depth, r // rb),
        in_specs=[in_spec],
        out_specs=out_spec,
        out_shape=jax.ShapeDtypeStruct(out_shape, jnp.uint32),
        compiler_params=_cparams(("parallel", "parallel")),
        name="pack_t" if transpose else "pack",
    )(tab)


def kernel(x, c, ctx, c_ctx, w_ada, b_ada, norm1_g, norm2_g, w_in, ml_gate_b, ml_conv_w, ml_norm_g, cv_conv_w,
           da_lam, da_subln_g, w_da, w_ml, w_cv, w_o, peer_wq, peer_keys, peer_u, peer_v, final_g):
    bsz, seq, d = x.shape
    n_ctx = ctx.shape[1]
    t = seq + n_ctx
    depth = w_ada.shape[0]
    tm = TOK_TILE
    assert seq % tm == 0 and n_ctx % tm == 0 and seq % ML_CHUNK == 0 and n_ctx % ML_CHUNK == 0
    assert (bsz * t) % PEER_TILE == 0 and (bsz * seq) % PEER_TILE == 0
    assert seq % ATTN_TILE == 0 and seq % n_ctx == 0
    n_lat_tiles = seq // tm
    n_tiles = t // tm

    rows = -(-(bsz + 1) // SUBLANES) * SUBLANES
    cc = jnp.zeros((rows, d), F32).at[:bsz].set(c).at[bsz].set(c_ctx)
    ada = _ada(cc, w_ada, b_ada)

    o = np.cumsum([0, DA_WIDTH, DA_WIDTH, DA_WIDTH, 2 * ML_WIDTH, ML_WIDTH, ML_WIDTH, 4 * ML_HEADS,
                   CV_WIDTH, CV_WIDTH, CV_WIDTH, 3 * d])
    quarter = DA_QK // 4
    wqk = w_in[:, :, o[0]:o[2]].reshape(depth, d, 2 * DA_HEADS, 2, 2, 2, quarter)
    wqk = wqk.transpose(0, 1, 2, 5, 3, 4, 6).reshape(depth, d, 2 * DA_WIDTH)
    wda_in = jnp.concatenate([wqk, w_in[:, :, o[2]:o[3]]], axis=-1).astype(BF16)
    wml_in = w_in[:, :, o[3]:o[6]].astype(BF16)
    wmg_in = jnp.pad(w_in[:, :, o[6]:o[7]], ((0, 0), (0, 0), (0, LANES - 4 * ML_HEADS))).astype(BF16)
    wcv_in = w_in[:, :, o[7]:o[10]].astype(BF16)
    wgt_in = w_in[:, :, o[10]:o[11]].astype(BF16)
    gate_b = jnp.pad(ml_gate_b.reshape(depth, 1, 4 * ML_HEADS), ((0, 0), (0, 0), (0, LANES - 4 * ML_HEADS)))
    lam_p = da_lam
    w_da_b, w_ml_b, w_cv_b, w_o_b = (a.astype(BF16) for a in (w_da, w_ml, w_cv, w_o))
    wq_b = peer_wq.astype(BF16)
    keys_b = peer_keys.astype(BF16)
    u_b = _pack_table(peer_u, transpose=False)
    vt_b = _pack_table(peer_v, transpose=True)
    cos, sin = _rope_tables(seq, t)

    xl, xc, xc_tile0 = x, ctx, 0
    for l in range(depth):
        need_ctx = l < depth - 1
        lam_init = 0.8 - 0.6 * math.exp(-0.3 * l)
        da, ml, mg, cv, gt = _inproj(xl, xc, xc_tile0, t, ada[l], norm1_g[l].reshape(1, d), cos, sin,
                                     wda_in[l], wml_in[l], wmg_in[l], wcv_in[l], wgt_in[l], n_lat_tiles)
        yda = _attention(da, lam_p[l], da_subln_g[l], lam_init, ATTN_TILE, 0, seq // ATTN_TILE, 0, t)
        yda_c = _attention(da, lam_p[l], da_subln_g[l], lam_init, n_ctx, seq, 1, seq, n_ctx) if need_ctx else yda
        yml = _mlstm(ml, mg, ml_conv_w[l], gate_b[l], ml_norm_g[l], seq // ML_CHUNK, n_ctx // ML_CHUNK)
        n_out_tiles = n_tiles if need_ctx else n_lat_tiles
        x1, h2 = _merge(xl, xc, xc_tile0, t, ada[l], norm2_g[l].reshape(1, d), yda, yda_c, yml, cv, gt, cv_conv_w[l],
                        w_da_b[l], w_ml_b[l], w_cv_b[l], w_o_b[l], n_lat_tiles, n_out_tiles)
        t_out = n_out_tiles * tm
        xs = _peer(x1.reshape(bsz * t_out, d), h2.reshape(bsz * t_out, d), ada[l], final_g.reshape(1, d),
                   wq_b[l], keys_b[l], u_b, vt_b, l, n_out_tiles, n_lat_tiles, bsz, final=not need_ctx)
        xs = xs.reshape(bsz, t_out, d)
        xl, xc, xc_tile0 = xs, xs, n_lat_tiles
    return xs
```

```python
import functools
import math

import numpy as np
import jax
import jax.numpy as jnp
from jax import lax
from jax.experimental import pallas as pl
from jax.experimental.pallas import tpu as pltpu

EPS = 1e-6
GRID_W = 64
DA_HEADS = 4
DA_QK = 64
DA_V = 2 * DA_QK
DA_WIDTH = DA_HEADS * DA_V
ROPE_BASE = 10000.0
ML_HEADS = 4
ML_DH = 128
ML_WIDTH = ML_HEADS * ML_DH
ML_CHUNK = 128
CV_WIDTH = 512
PEER_HEADS = 8
PEER_TOPK = 16
PEER_GROUP = 2

LANES = 128
SUBLANES = 8
ATTN_TILE = 1024
TOK_TILE = 256
HALO = 16
PEER_TILE = 512
PEER_CHUNK_I1 = 8
PEER_SUBSTEPS = 2
PACK_ROWS = 1024
PEER_MM_SLICES = 2
PEER_SORT_W = 2 * LANES
VMEM_LIMIT = 56 * 1024 * 1024

NEG_BIG = -1e30
BF16 = jnp.bfloat16
F32 = jnp.float32
HIGHEST = lax.Precision.HIGHEST


def _cparams(sem):
    return pltpu.CompilerParams(dimension_semantics=sem, vmem_limit_bytes=VMEM_LIMIT)


def _dot(a, b):
    return jnp.dot(a, b, preferred_element_type=F32)


def _dot_nt(a, b):
    return lax.dot_general(a, b, (((1,), (1,)), ((), ())), preferred_element_type=F32)


def _dot_tn(a, b):
    return lax.dot_general(a, b, (((0,), (0,)), ((), ())), preferred_element_type=F32)


def _rms(x, g):
    return x * lax.rsqrt(jnp.mean(x * x, axis=-1, keepdims=True) + EPS) * g


def _sigmoid(x):
    return 1.0 / (1.0 + jnp.exp(-x))


def _log_sigmoid(x):
    return jnp.minimum(x, 0.0) - jnp.log1p(jnp.exp(-jnp.abs(x)))


def _ada_kernel(c_ref, w_ref, b_ref, o_ref):
    c = c_ref[...]
    s = c * _sigmoid(c)
    o_ref[0] = jnp.dot(s, w_ref[0], preferred_element_type=F32, precision=HIGHEST) + b_ref[0]


def _ada(cc, w_ada, b_ada):
    depth, d, d6 = w_ada.shape
    rows = cc.shape[0]
    nb = d6 // d
    return pl.pallas_call(
        _ada_kernel,
        grid=(depth, nb),
        in_specs=[pl.BlockSpec((rows, d), lambda l, j: (0, 0)),
                  pl.BlockSpec((1, d, d), lambda l, j: (l, 0, j)),
                  pl.BlockSpec((1, 1, d), lambda l, j: (l, 0, j))],
        out_specs=pl.BlockSpec((1, rows, d), lambda l, j: (l, 0, j)),
        out_shape=jax.ShapeDtypeStruct((depth, rows, d6), F32),
        compiler_params=_cparams(("parallel", "parallel")),
        name="ada",
    )(cc, w_ada, b_ada.reshape(depth, 1, d6))


def _inproj_kernel(n_lat_tiles, n_batch, xl_ref, xc_ref, ada_ref, g_ref, cos_ref, sin_ref,
                   wda_ref, wml_ref, wmg_ref, wcv_ref, wgt_ref,
                   da_ref, ml_ref, mg_ref, cv_ref, gt_ref):
    b = pl.program_id(0)
    i = pl.program_id(1)
    d = xl_ref.shape[-1]
    is_lat = i < n_lat_tiles
    row = jnp.where(is_lat, b, n_batch)
    shift = ada_ref[pl.ds(row, 1), 0:d]
    scale = ada_ref[pl.ds(row, 1), d:2 * d]
    x = jnp.where(is_lat, xl_ref[0], xc_ref[0])
    h = _rms(x, g_ref[...]) * (1.0 + scale) + shift
    hb = h.astype(BF16)

    zda = _dot(hb, wda_ref[...])
    cos = cos_ref[...]
    sin = sin_ref[...]
    for j in range(2 * DA_HEADS):
        blk = zda[:, j * LANES:(j + 1) * LANES]
        rot = blk * cos + pltpu.roll(blk, LANES // 2, 1) * sin
        da_ref[0, :, j * LANES:(j + 1) * LANES] = rot.astype(BF16)
    da_ref[0, :, 2 * DA_WIDTH:] = zda[:, 2 * DA_WIDTH:].astype(BF16)

    ml_ref[0] = _dot(hb, wml_ref[...]).astype(BF16)
    mg_ref[0] = _dot(hb, wmg_ref[...])

    zcv = _dot(hb, wcv_ref[...])
    cv_ref[0, :, 0:CV_WIDTH] = zcv[:, 0:CV_WIDTH].astype(BF16)
    cv_ref[0, :, CV_WIDTH:] = (zcv[:, CV_WIDTH:2 * CV_WIDTH] * zcv[:, 2 * CV_WIDTH:]).astype(BF16)

    for j in range(3):
        zg = _dot(hb, wgt_ref[:, j * d:(j + 1) * d])
        gt_ref[0, :, j * d:(j + 1) * d] = _sigmoid(zg).astype(BF16)


def _stream_specs(tm, d, n_lat_tiles, n_tiles, xc_tile0):
    n_ctx_tiles = n_tiles - n_lat_tiles
    lat = pl.BlockSpec((1, tm, d), lambda b, i: (b, jnp.minimum(i, n_lat_tiles - 1), 0))
    ctx = pl.BlockSpec((1, tm, d), lambda b, i: (b, xc_tile0 + jnp.clip(i - n_lat_tiles, 0, n_ctx_tiles - 1), 0))
    return lat, ctx


def _inproj(xl, xc, xc_tile0, t, ada, g, cos, sin, wda, wml, wmg, wcv, wgt, n_lat_tiles):
    bsz, _, d = xl.shape
    tm = TOK_TILE
    full = lambda arr: pl.BlockSpec(arr.shape, lambda b, i: (0,) * arr.ndim, pipeline_mode=pl.Buffered(1))
    tok = lambda w: pl.BlockSpec((1, tm, w), lambda b, i: (b, i, 0))
    lat_spec, ctx_spec = _stream_specs(tm, d, n_lat_tiles, t // tm, xc_tile0)
    widths =(wda.shape[1], wml.shape[1], wmg.shape[1], 2 * CV_WIDTH, wgt.shape[1])
    dts = (BF16, BF16, F32, BF16, BF16)
    return pl.pallas_call(
        functools.partial(_inproj_kernel, n_lat_tiles, bsz),
        grid=(bsz, t // tm),
        in_specs=[lat_spec, ctx_spec, full(ada), full(g),
                  pl.BlockSpec((tm, LANES), lambda b, i: (i, 0)),
                  pl.BlockSpec((tm, LANES), lambda b, i: (i, 0)),
                  full(wda), full(wml), full(wmg), full(wcv), full(wgt)],
        out_specs=[tok(w) for w in widths],
        out_shape=[jax.ShapeDtypeStruct((bsz, t, w), dt) for w, dt in zip(widths, dts)],
        compiler_params=_cparams(("parallel", "parallel")),
        name="inproj",
    )(xl, xc, ada, g, cos, sin, wda, wml, wmg, wcv, wgt)


def _attn_kernel(lam_init, q_ref, k_ref, v_ref, lam_ref, g_ref, o_ref, va_ref):
    i = pl.program_id(2)
    lv = lam_ref[...]
    lam = (jnp.exp(jnp.sum(lv[0:1] * lv[1:2], axis=-1, keepdims=True))
           - jnp.exp(jnp.sum(lv[2:3] * lv[3:4], axis=-1, keepdims=True)) + lam_init)
    lane = lax.broadcasted_iota(jnp.int32, (1, LANES), 1)
    map0 = (lane % (LANES // 2)) < (DA_QK // 2)

    @pl.when(i == 0)
    def _():
        va_ref[:, 0:LANES] = v_ref[0]
        va_ref[:, LANES:] = jnp.ones((va_ref.shape[0], LANES), BF16)

    k = k_ref[0]
    va = va_ref[...]
    q = q_ref[0].astype(F32) * (DA_QK ** -0.5)
    q0 = jnp.where(map0, q, 0.0).astype(BF16)
    q1 = jnp.where(map0, 0.0, q).astype(BF16)
    s0 = _dot_nt(q0, k)
    s1 = _dot_nt(q1, k)
    e0 = jnp.exp((s0 - jnp.max(s0, axis=-1, keepdims=True)).astype(BF16))
    e1 = jnp.exp((s1 - jnp.max(s1, axis=-1, keepdims=True)).astype(BF16))
    o0 = _dot(e0, va)
    o1 = _dot(e1, va)
    o = o0[:, 0:LANES] / o0[:, LANES:] - lam * (o1[:, 0:LANES] / o1[:, LANES:])
    o_ref[0] = (_rms(o, g_ref[...]) * (1.0 - lam_init)).astype(o_ref.dtype)


def _attention(da, lam_p, subln_g, lam_init, tq, q_row0, n_q, kv_row0, kv_rows):
    bsz = da.shape[0]
    q0 = q_row0 // tq
    kv0 = kv_row0 // kv_rows
    return pl.pallas_call(
        functools.partial(_attn_kernel, lam_init),
        grid=(bsz, DA_HEADS, n_q),
        in_specs=[pl.BlockSpec((1, tq, LANES), lambda b, h, i: (b, q0 + i, h)),
                  pl.BlockSpec((1, kv_rows, LANES), lambda b, h, i: (b, kv0, DA_HEADS + h)),
                  pl.BlockSpec((1, kv_rows, LANES), lambda b, h, i: (b, kv0, 2 * DA_HEADS + h)),
                  pl.BlockSpec(lam_p.shape, lambda b, h, i: (0, 0)),
                  pl.BlockSpec((1, LANES), lambda b, h, i: (0, 0))],
        out_specs=pl.BlockSpec((1, tq, LANES), lambda b, h, i: (b, i, h)),
        out_shape=jax.ShapeDtypeStruct((bsz, n_q * tq, DA_WIDTH), BF16),
        scratch_shapes=[pltpu.VMEM((kv_rows, 2 * LANES), BF16)],
        compiler_params=_cparams(("parallel", "parallel", "arbitrary")),
        name="diffattn",
    )(da, da, da, lam_p, subln_g.reshape(1, LANES))


def _mlstm_kernel(ncl, ncc, ml_ref, mg_ref, cw_ref, gb_ref, ng_ref, o_ref,
                  qs_ref, ks_ref, hf_ref, hb_ref, c_ref, n_ref, m_ref):
    L = ML_CHUNK
    nc = ncl + ncc
    t_all = nc * L
    seq = ncl * L
    w = ML_WIDTH

    trow = lax.broadcasted_iota(jnp.int32, (t_all, 1), 0)
    first = (trow == 0) | (trow == seq)
    last = (trow == seq - 1) | (trow == t_all - 1)
    for j in range(2 * ML_HEADS):
        x = ml_ref[0, :, j * LANES:(j + 1) * LANES].astype(F32)
        xm = jnp.where(first, 0.0, pltpu.roll(x, 1, 0))
        xp = jnp.where(last, 0.0, pltpu.roll(x, t_all - 1, 0))
        cw = cw_ref[:, j * LANES:(j + 1) * LANES]
        y = xm * cw[0:1] + x * cw[1:2] + xp * cw[2:3]
        y = y * _sigmoid(y)
        if j < ML_HEADS:
            qs_ref[:, j * LANES:(j + 1) * LANES] = y.astype(BF16)
        else:
            jj = j - ML_HEADS
            ks_ref[:, jj * LANES:(jj + 1) * LANES] = (y * (ML_DH ** -0.5)).astype(BF16)

    c_ref[...] = jnp.zeros_like(c_ref)
    n_ref[...] = jnp.zeros_like(n_ref)
    m_ref[...] = jnp.zeros_like(m_ref)

    r_i = lax.broadcasted_iota(jnp.int32, (L, L), 0)
    c_i = lax.broadcasted_iota(jnp.int32, (L, L), 1)
    lower = r_i >= c_i
    upper = r_i <= c_i
    lower_f = lower.astype(F32)
    upper_f = upper.astype(F32)
    ones_blk = jnp.ones((L, LANES), BF16)

    def step(it, carry):
        dirs = []
        for dr in range(2):
            if dr == 0:
                ch = (it + ncl) % nc
                seen, tri_col, tri_row, end, h_ref = lower, lower_f, upper_f, L - 1, hf_ref
            else:
                ch = nc - 1 - it
                seen, tri_col, tri_row, end, h_ref = upper, upper_f, lower_f, 0, hb_ref
            r0 = pl.multiple_of(ch * L, L)
            g = mg_ref[0, pl.ds(r0, L), :] + gb_ref[...]
            gt = g.T
            bcol_all = jnp.dot(tri_col, _log_sigmoid(g), preferred_element_type=F32, precision=HIGHEST)
            brow_all = jnp.dot(_log_sigmoid(gt), tri_row, preferred_element_type=F32, precision=HIGHEST)
            dirs.append((seen, end, h_ref, r0, g, gt, bcol_all, brow_all))

        rec = []
        for dr in range(2):
            seen, end, h_ref, r0, g, gt, bcol_all, brow_all = dirs[dr]
            for h in range(ML_HEADS):
                icol = (2 * dr) * ML_HEADS + h
                fcol = (2 * dr + 1) * ML_HEADS + h
                idx = dr * ML_HEADS + h
                rec.append(dict(
                    seen=seen, end=end, h_ref=h_ref, r0=r0, idx=idx, cols=slice(h * LANES, (h + 1) * LANES),
                    q=qs_ref[pl.ds(r0, L), h * LANES:(h + 1) * LANES],
                    k=ks_ref[pl.ds(r0, L), h * LANES:(h + 1) * LANES],
                    v=ml_ref[0, pl.ds(r0, L), 2 * w + h * LANES:2 * w + (h + 1) * LANES],
                    b_col=bcol_all[:, fcol:fcol + 1], ig_col=g[:, icol:icol + 1],
                    b_row=brow_all[fcol:fcol + 1, :], ig_row=gt[icol:icol + 1, :],
                    m_prev=m_ref[idx][:, 0:1], cmat=c_ref[idx], nrow=n_ref[idx]))

        for c in rec:
            c["b_cb"] = jnp.broadcast_to(c["b_col"], (L, L))
            c["log_d"] = jnp.where(c["seen"], c["b_cb"] - c["b_row"] + c["ig_row"], NEG_BIG)
        for c in rec:
            c["m_path"] = c["b_col"] + c["m_prev"]
            c["m_t"] = jnp.maximum(c["m_path"], jnp.max(c["log_d"], axis=-1, keepdims=True))
        for c in rec:
            c["m_tb"] = jnp.broadcast_to(c["m_t"], (L, L))
            c["dmat"] = jnp.exp(c["log_d"] - c["m_tb"])
            c["carry_b"] = jnp.exp((c["b_cb"] + c["m_prev"]) - c["m_tb"])
        for c in rec:
            wmat = _dot_nt(c["q"], c["k"]) * c["dmat"]
            wv = _dot(wmat.astype(BF16), jnp.concatenate([c["v"], ones_blk], axis=1))
            qn = _dot_nt(c["q"], jnp.broadcast_to(c["nrow"], (L, LANES)).astype(BF16))
            num = wv[:, 0:LANES] + c["carry_b"] * _dot(c["q"], c["cmat"].astype(BF16))
            den = wv[:, LANES:] + c["carry_b"] * qn
            hval = num / jnp.maximum(jnp.abs(den), jnp.exp(-c["m_tb"]))
            c["h_ref"][pl.ds(c["r0"], L), c["cols"]] = hval
        for c in rec:
            b_end = c["b_col"][c["end"]:c["end"] + 1, :]
            log_w = b_end - c["b_col"] + c["ig_col"]
            m_new = jnp.maximum(b_end + c["m_prev"], jnp.max(log_w, axis=0, keepdims=True))
            c["wk"] = jnp.exp(log_w - m_new)
            c["decay"] = jnp.exp(b_end + c["m_prev"] - m_new)
            c["m_new"] = m_new
        for c in rec:
            kw = c["k"].astype(F32) * c["wk"]
            idx = c["idx"]
            c_ref[idx] = c["decay"] * c["cmat"] + _dot_tn(kw.astype(BF16), c["v"])
            n_ref[idx] = c["decay"] * c["nrow"] + jnp.sum(kw, axis=0, keepdims=True)
            m_ref[idx] = jnp.broadcast_to(c["m_new"], (1, LANES))
        return carry

    lax.fori_loop(0, nc, step, 0)

    for h in range(ML_HEADS):
        sl = slice(h * LANES, (h + 1) * LANES)
        hs = hf_ref[:, sl] + hb_ref[:, sl]
        y = _rms(hs, ng_ref[:, sl])
        og = ml_ref[0, :, 3 * w + h * LANES:3 * w + (h + 1) * LANES].astype(F32)
        o_ref[0, :, sl] = (y * _sigmoid(og)).astype(o_ref.dtype)


def _mlstm(ml, mg, conv_w, gate_b_row, norm_g, ncl, ncc):
    bsz, t, wtot = ml.shape
    nstate = 2 * ML_HEADS
    return pl.pallas_call(
        functools.partial(_mlstm_kernel, ncl, ncc),
        grid=(bsz,),
        in_specs=[pl.BlockSpec((1, t, wtot), lambda b: (b, 0, 0)),
                  pl.BlockSpec((1, t, LANES), lambda b: (b, 0, 0)),
                  pl.BlockSpec(conv_w.shape, lambda b: (0, 0)),
                  pl.BlockSpec((1, LANES), lambda b: (0, 0)),
                  pl.BlockSpec((1, ML_WIDTH), lambda b: (0, 0))],
        out_specs=pl.BlockSpec((1, t, ML_WIDTH), lambda b: (b, 0, 0)),
        out_shape=jax.ShapeDtypeStruct((bsz, t, ML_WIDTH), BF16),
        scratch_shapes=[pltpu.VMEM((t, ML_WIDTH), BF16), pltpu.VMEM((t, ML_WIDTH), BF16),
                        pltpu.VMEM((t, ML_WIDTH), F32), pltpu.VMEM((t, ML_WIDTH), F32),
                        pltpu.VMEM((nstate, ML_DH, ML_DH), F32),
                        pltpu.VMEM((nstate, 1, ML_DH), F32),
                        pltpu.VMEM((nstate, 1, LANES), F32)],
        compiler_params=_cparams(("parallel",)),
        name="mlstm",
    )(ml, mg, conv_w, gate_b_row, norm_g.reshape(1, ML_WIDTH))


def _merge_kernel(n_lat_tiles, n_batch, xl_ref, xc_ref, ada_ref, g2_ref, ydal_ref, ydac_ref, yml_ref, cv_ref, pprev_ref, pnext_ref,
                  gt_ref, cw_ref, wda_ref, wml_ref, wcv_ref, wo_ref, x1_ref, h2_ref):
    b = pl.program_id(0)
    i = pl.program_id(1)
    n_tiles = pl.num_programs(1)
    d = xl_ref.shape[-1]
    tm = xl_ref.shape[1]
    is_lat = i < n_lat_tiles
    row = jnp.where(is_lat, b, n_batch)

    bg = cv_ref[0, :, 0:CV_WIDTH].astype(F32)
    p = cv_ref[0, :, CV_WIDTH:].astype(F32)
    seq_start = (i == 0) | (i == n_lat_tiles)
    seq_end = (i == n_lat_tiles - 1) | (i == n_tiles - 1)
    prev_row = jnp.where(seq_start, 0.0, pprev_ref[0, HALO - 1:HALO, :].astype(F32))
    next_row = jnp.where(seq_end, 0.0, pnext_ref[0, 0:1, :].astype(F32))
    trow = lax.broadcasted_iota(jnp.int32, (tm, 1), 0)
    pm = jnp.where(trow == 0, prev_row, pltpu.roll(p, 1, 0))
    pp = jnp.where(trow == tm - 1, next_row, pltpu.roll(p, tm - 1, 0))
    cw = cw_ref[...]
    ycv = bg * (pm * cw[0:1] + p * cw[1:2] + pp * cw[2:3])

    yda = jnp.where(is_lat, ydal_ref[0], ydac_ref[0])
    m = (gt_ref[0, :, 0:d].astype(F32) * _dot(yda, wda_ref[...])
         + gt_ref[0, :, d:2 * d].astype(F32) * _dot(yml_ref[0], wml_ref[...])
         + gt_ref[0, :, 2 * d:].astype(F32) * _dot(ycv.astype(BF16), wcv_ref[...]))
    out = _dot(m.astype(BF16), wo_ref[...])
    g1 = ada_ref[pl.ds(row, 1), 2 * d:3 * d]
    x1 = jnp.where(is_lat, xl_ref[0], xc_ref[0]) + g1 * out
    x1_ref[0] = x1
    sh2 = ada_ref[pl.ds(row, 1), 3 * d:4 * d]
    sc2 = ada_ref[pl.ds(row, 1), 4 * d:5 * d]
    h2_ref[0] = (_rms(x1, g2_ref[...]) * (1.0 + sc2) + sh2).astype(BF16)


def _merge(xl, xc, xc_tile0, t, ada, g2, yda, yda_c, yml, cv, gt, cv_w, wda, wml, wcv, wo, n_lat_tiles, n_tiles):
    bsz, _, d = xl.shape
    tm = TOK_TILE
    lat_spec, ctx_spec = _stream_specs(tm, d, n_lat_tiles, t // tm, xc_tile0)
    yl_spec, yc_spec = _stream_specs(tm, DA_WIDTH, n_lat_tiles, t // tm, 0)
    r = tm // HALO
    nrb = t // HALO
    full = lambda arr: pl.BlockSpec(arr.shape, lambda b, i: (0,) * arr.ndim, pipeline_mode=pl.Buffered(1))
    tok = lambda w: pl.BlockSpec((1, tm, w), lambda b, i: (b, i, 0))
    return pl.pallas_call(
        functools.partial(_merge_kernel, n_lat_tiles, bsz),
        grid=(bsz, n_tiles),
        in_specs=[lat_spec, ctx_spec, full(ada), full(g2), yl_spec, yc_spec, tok(ML_WIDTH), tok(2 * CV_WIDTH),
                  pl.BlockSpec((1, HALO, CV_WIDTH), lambda b, i: (b, jnp.maximum(i * r - 1, 0), 1)),
                  pl.BlockSpec((1, HALO, CV_WIDTH), lambda b, i: (b, jnp.minimum((i + 1) * r, nrb - 1), 1)),
                  tok(3 * d), full(cv_w), full(wda), full(wml), full(wcv), full(wo)],
        out_specs=[tok(d), tok(d)],
        out_shape=[jax.ShapeDtypeStruct((bsz, n_tiles * tm, d), F32),
                   jax.ShapeDtypeStruct((bsz, n_tiles * tm, d), BF16)],
        compiler_params=_cparams(("parallel", "parallel")),
        name="merge",
    )(xl, xc, ada, g2, yda, yda_c, yml, cv, cv, cv, gt, cv_w, wda, wml, wcv, wo)


def _cmpx(v, i, j):
    a, b = v[i], v[j]
    v[i] = jnp.maximum(a, b)
    v[j] = jnp.minimum(a, b)


def _bitonic_merge_desc(v):
    n = len(v)
    j = n // 2
    while j >= 1:
        for i in range(n):
            l = i ^ j
            if l > i:
                _cmpx(v, i, l)
        j //= 2


def _bitonic_sort_desc(v):
    n = len(v)
    k = 2
    while k <= n:
        j = k // 2
        while j >= 1:
            for i in range(n):
                l = i ^ j
                if l > i:
                    if (i & k) == 0:
                        _cmpx(v, i, l)
                    else:
                        _cmpx(v, l, i)
            j //= 2
        k *= 2


def _merge_top(a, b):
    n = len(a)
    c = [jnp.maximum(a[r], b[n - 1 - r]) for r in range(n)]
    _bitonic_merge_desc(c)
    return c


def _top16_sorted(s):
    nk, tm = s.shape
    k = PEER_TOPK
    groups = nk // k
    assert groups == SUBLANES
    a = s.reshape(k, SUBLANES, tm)
    v = [a[r] for r in range(k)]
    _bitonic_sort_desc(v)
    sh = SUBLANES // 2
    while sh >= 1:
        v = _merge_top(v, [pltpu.roll(x, sh, 0) for x in v])
        sh //= 2
    return v


def _erf(x):
    return lax.erf(x)


def _peer_kernel(tiles_per_batch, n_lat_tiles, n_batch, n_i1, final, x_ref, h_ref, ada_ref, fg_ref, wq_ref, keys_ref,
                 u_ref, vt_ref, o_ref, s1_ref, s2_ref, f1_ref, thr_ref, e2_ref, t1_ref, t2_ref, cr_ref, rz_ref,
                 at0_ref, at1_ref, w0_ref, w1_ref, acc_ref, ht_ref):
    t = pl.program_id(0)
    s = pl.program_id(1)
    n_steps = pl.num_programs(1)
    n_chunks = (n_steps - 1) * PEER_SUBSTEPS
    d = x_ref.shape[-1]
    tm = x_ref.shape[0]
    nk = keys_ref.shape[2]
    dsub = keys_ref.shape[3]
    k = PEER_TOPK
    sort_w = PEER_SORT_W
    at_refs = (at0_ref, at1_ref)
    w_refs = (w0_ref, w1_ref)

    @pl.when(s == 0)
    def _():
        hb = h_ref[...]
        for half in range(tm // TOK_TILE):
            cs = slice(half * TOK_TILE, (half + 1) * TOK_TILE)
            ht_ref[:, cs] = hb[cs, :].astype(F32).T.astype(BF16)
        q = _dot(hb, wq_ref[...]).astype(BF16)
        for h in range(PEER_HEADS):
            for p in range(2):
                col = (h * 2 + p) * dsub
                (s1_ref if p == 0 else s2_ref)[h] = _dot_nt(keys_ref[h, p], q[:, col:col + dsub])
        neg = jnp.full((SUBLANES, sort_w), -jnp.inf, F32)
        for blk in range(tm // sort_w):
            ls = slice(blk * sort_w, (blk + 1) * sort_w)

            def sort_head(h, carry):
                for sref, tref in ((s1_ref, t1_ref), (s2_ref, t2_ref)):
                    v = _top16_sorted(sref[h, :, ls])
                    for r in range(k):
                        tref[blk, r, pl.ds(h, 1), :] = v[r][0:1, :]
                return carry

            lax.fori_loop(0, PEER_HEADS, sort_head, 0)
            t1 = [t1_ref[blk, r] for r in range(k)]
            t2 = [t2_ref[blk, r] for r in range(k)]
            rows = [[t1[r1] + t2[r2] for r2 in range(k // (r1 + 1))] for r1 in range(k)]
            best = rows[0]
            rest = [x for rw in rows[1:] for x in rw]
            while rest:
                grp, rest = rest[:k], rest[k:]
                grp = grp + [neg] * (k - len(grp))
                _bitonic_sort_desc(grp)
                best = _merge_top(best, grp)
            mx = best[0]
            z = jnp.exp(best[0] - mx)
            for r in range(1, k):
                z = z + jnp.exp(best[r] - mx)
            rz = 1.0 / z
            tau = best[k - 1]
            for r1 in range(k):
                thr = jnp.full((SUBLANES, sort_w), jnp.inf, F32)
                for r2 in range(k // (r1 + 1)):
                    thr = jnp.minimum(thr, jnp.where(rows[r1][r2] >= tau, t2[r2], jnp.inf))
                cr_ref[blk, r1] = thr
            rz_ref[blk] = rz

            def finish_head(h, carry):
                hs = pl.ds(h, 1)
                s1h = s1_ref[h, :, ls]
                th = jnp.full_like(s1h, jnp.inf)
                for r in reversed(range(k)):
                    th = jnp.where(s1h == t1_ref[blk, r, hs, :], cr_ref[blk, r, hs, :], th)
                thr_ref[h, :, ls] = th
                f1_ref[h, :, ls] = jnp.exp(s1h - t1_ref[blk, 0, hs, :]) * rz_ref[blk, hs, :]
                e2_ref[h, :, ls] = jnp.exp(s2_ref[h, :, ls] - t2_ref[blk, 0, hs, :])
                return carry

            lax.fori_loop(0, PEER_HEADS, finish_head, 0)
        acc_ref[...] = jnp.zeros_like(acc_ref)

    ec = n_i1 * nk
    rb_rows = nk // 2
    n_grp = n_i1 // PEER_GROUP
    g_rows = ec // n_grp
    gpm = n_grp // PEER_MM_SLICES
    d_rows = d // PEER_MM_SLICES
    e_rows = ec // PEER_MM_SLICES

    def gate_group(grp, at_oth, w_oth, cb):
        g_off = pl.multiple_of(grp * g_rows, g_rows)
        i1s = [cb * n_i1 + grp * PEER_GROUP + jj for jj in range(PEER_GROUP)]
        throws = [[thr_ref[h, pl.ds(i1, 1), :] for i1 in i1s] for h in range(PEER_HEADS)]
        f1rows = [[f1_ref[h, pl.ds(i1, 1), :] for i1 in i1s] for h in range(PEER_HEADS)]
        for tb in range(tm // LANES):
            ts = slice(tb * LANES, (tb + 1) * LANES)
            for rb in range(nk // rb_rows):
                rs = slice(rb * rb_rows, (rb + 1) * rb_rows)
                gs = [jnp.zeros((rb_rows, LANES), F32) for _ in range(PEER_GROUP)]
                for h in range(PEER_HEADS):
                    s2b = s2_ref[h, rs, ts]
                    e2b = e2_ref[h, rs, ts]
                    for jj in range(PEER_GROUP):
                        thr = throws[h][jj][:, ts]
                        f1r = f1rows[h][jj][:, ts]
                        gs[jj] = gs[jj] + jnp.where(s2b >= thr, e2b * f1r, 0.0)
                for jj in range(PEER_GROUP):
                    r0 = pl.multiple_of(g_off + jj * nk + rb * rb_rows, rb_rows)
                    a = at_oth[pl.ds(r0, rb_rows), ts].astype(BF16)
                    act = (0.5 * a) * (1.0 + _erf(a * (2.0 ** -0.5)))
                    w_oth[pl.ds(r0, rb_rows), ts] = act * gs[jj].astype(BF16)

    def substeps(do1, do2, do3):
        for sub in range(PEER_SUBSTEPS):
            at_cur, at_oth = at_refs[sub], at_refs[1 - sub]
            w_cur, w_oth = w_refs[sub], w_refs[1 - sub]
            cb = s * PEER_SUBSTEPS + sub - 1

            def mm_slice(ms, carry, sub=sub, at_cur=at_cur, at_oth=at_oth, w_cur=w_cur, w_oth=w_oth, cb=cb):
                if do1[sub]:
                    e_off = pl.multiple_of(ms * e_rows, e_rows)
                    u_off = pl.multiple_of((sub * ec + e_off) // 2, e_rows // 2)
                    u_rows = pltpu.bitcast(u_ref[pl.ds(u_off, e_rows // 2), :], BF16)
                    at_cur[pl.ds(e_off, e_rows), :] = _dot(u_rows, ht_ref[...])
                if do3[sub]:
                    d_off = pl.multiple_of(ms * d_rows, d_rows)
                    v_off = pl.multiple_of(d_off // 2, d_rows // 2)
                    v_rows = pltpu.bitcast(vt_ref[pl.ds(v_off, d_rows // 2), sub * ec:(sub + 1) * ec], BF16)
                    acc_ref[pl.ds(d_off, d_rows), :] += _dot(v_rows, w_cur[...])
                if do2[sub]:
                    for gi in range(gpm):
                        gate_group(ms * gpm + gi, at_oth, w_oth, cb)
                return carry

            lax.fori_loop(0, PEER_MM_SLICES, mm_slice, 0)

    first = s == 0
    last = s == n_steps - 1

    @pl.when(first)
    def _():
        substeps(do1=(True, True), do2=(False, True), do3=(False, False))

    @pl.when(jnp.logical_not(first | last))
    def _():
        substeps(do1=(True, True), do2=(True, True), do3=(True, True))

    @pl.when(last)
    def _():
        substeps(do1=(False, False), do2=(True, False), do3=(True, True))

    @pl.when(s == n_steps - 1)
    def _():
        for half in range(tm // TOK_TILE):
            g256 = t * (tm // TOK_TILE) + half
            row = jnp.where(g256 % tiles_per_batch < n_lat_tiles, g256 // tiles_per_batch, n_batch)
            g2 = ada_ref[pl.ds(row, 1), 5 * d:6 * d]
            rs = slice(half * TOK_TILE, (half + 1) * TOK_TILE)
            x2 = x_ref[rs, :] + g2 * acc_ref[:, rs].T
            if final:
                x2 = _rms(x2, fg_ref[...])
            o_ref[rs, :] = x2


def _peer(x1, h2, ada, final_g, wq, keys, u, vt, layer, tiles_per_batch, n_lat_tiles, n_batch, final):
    rows, d = x1.shape
    tm = PEER_TILE
    nk = keys.shape[2]
    n_i1 = PEER_CHUNK_I1
    ec = n_i1 * nk
    n_blocks = 2 * u.shape[1] // (PEER_SUBSTEPS * ec)
    n_sort = tm // PEER_SORT_W
    assert PEER_HEADS == SUBLANES
    full = lambda arr: pl.BlockSpec(arr.shape, lambda t, s: (0,) * arr.ndim, pipeline_mode=pl.Buffered(1))
    tok = pl.BlockSpec((tm, d), lambda t, s: (t, 0))
    return pl.pallas_call(
        functools.partial(_peer_kernel, tiles_per_batch, n_lat_tiles, n_batch, n_i1, final),
        grid=(rows // tm, n_blocks + 1),
        in_specs=[tok, tok, full(ada), full(final_g), full(wq), full(keys),
                  pl.BlockSpec((None, PEER_SUBSTEPS * ec // 2, d),
                               lambda t, s: (layer, jnp.minimum(s, n_blocks - 1), 0)),
                  pl.BlockSpec((None, d // 2, PEER_SUBSTEPS * ec), lambda t, s: (layer, 0, jnp.maximum(s - 1, 0)))],
        out_specs=tok,
        out_shape=jax.ShapeDtypeStruct((rows, d), F32),
        scratch_shapes=[pltpu.VMEM((PEER_HEADS, nk, tm), F32), pltpu.VMEM((PEER_HEADS, nk, tm), F32),
                        pltpu.VMEM((PEER_HEADS, nk, tm), F32), pltpu.VMEM((PEER_HEADS, nk, tm), F32),
                        pltpu.VMEM((PEER_HEADS, nk, tm), F32),
                        pltpu.VMEM((n_sort, PEER_TOPK, PEER_HEADS, PEER_SORT_W), F32),
                        pltpu.VMEM((n_sort, PEER_TOPK, PEER_HEADS, PEER_SORT_W), F32),
                        pltpu.VMEM((n_sort, PEER_TOPK, PEER_HEADS, PEER_SORT_W), F32),
                        pltpu.VMEM((n_sort, PEER_HEADS, PEER_SORT_W), F32),
                        pltpu.VMEM((ec, tm), F32), pltpu.VMEM((ec, tm), F32),
                        pltpu.VMEM((ec, tm), BF16), pltpu.VMEM((ec, tm), BF16),
                        pltpu.VMEM((d, tm), F32), pltpu.VMEM((d, tm), BF16)],
        compiler_params=_cparams(("parallel", "arbitrary")),
        name="peer",
    )(x1, h2, ada, final_g, wq, keys, u, vt)


def _rope_tables(seq, t):
    quarter = DA_QK // 4
    inv = ROPE_BASE ** (-np.arange(quarter, dtype=np.float32) / quarter)
    pos = np.arange(seq)
    ang_r = (pos // GRID_W).astype(np.float32)[:, None] * inv[None, :]
    ang_c = (pos % GRID_W).astype(np.float32)[:, None] * inv[None, :]
    half = np.concatenate([ang_r, ang_c, ang_r, ang_c], axis=1)
    cos = np.ones((t, LANES), np.float32)
    sin = np.zeros((t, LANES), np.float32)
    cos[:seq] = np.concatenate([np.cos(half), np.cos(half)], axis=1)
    sin[:seq] = np.concatenate([-np.sin(half), np.sin(half)], axis=1)
    return jnp.asarray(cos), jnp.asarray(sin)


def _pack_kernel(transpose, x_ref, o_ref):
    x = x_ref[0]
    if transpose:
        x = x.T
    o_ref[0] = pltpu.bitcast(x.astype(BF16), jnp.uint32)


def _pack_table(tab, transpose):
    depth, r, c = tab.shape
    rb = PACK_ROWS
    if transpose:
        in_spec = pl.BlockSpec((1, rb, c), lambda l, j: (l, j, 0))
        out_spec = pl.BlockSpec((1, c // 2, rb), lambda l, j: (l, 0, j))
        out_shape = (depth, c // 2, r)
    else:
        in_spec = pl.BlockSpec((1, rb, c), lambda l, j: (l, j, 0))
        out_spec = pl.BlockSpec((1, rb // 2, c), lambda l, j: (l, j, 0))
        out_shape = (depth, r // 2, c)
    return pl.pallas_call(
        functools.partial(_pack_kernel, transpose),
        grid=(depth, r // rb),
        in_specs=[in_spec],
        out_specs=out_spec,
        out_shape=jax.ShapeDtypeStruct(out_shape, jnp.uint32),
        compiler_params=_cparams(("parallel", "parallel")),
        name="pack_t" if transpose else "pack",
    )(tab)


def kernel(x, c, ctx, c_ctx, w_ada, b_ada, norm1_g, norm2_g, w_in, ml_gate_b, ml_conv_w, ml_norm_g, cv_conv_w,
           da_lam, da_subln_g, w_da, w_ml, w_cv, w_o, peer_wq, peer_keys, peer_u, peer_v, final_g):
    bsz, seq, d = x.shape
    n_ctx = ctx.shape[1]
    t = seq + n_ctx
    depth = w_ada.shape[0]
    tm = TOK_TILE
    assert seq % tm == 0 and n_ctx % tm == 0 and seq % ML_CHUNK == 0 and n_ctx % ML_CHUNK == 0
    assert (bsz * t) % PEER_TILE == 0 and (bsz * seq) % PEER_TILE == 0
    assert seq % ATTN_TILE == 0 and seq % n_ctx == 0
    n_lat_tiles = seq // tm
    n_tiles = t // tm

    rows = -(-(bsz + 1) // SUBLANES) * SUBLANES
    cc = jnp.zeros((rows, d), F32).at[:bsz].set(c).at[bsz].set(c_ctx)
    ada = _ada(cc, w_ada, b_ada)

    o = np.cumsum([0, DA_WIDTH, DA_WIDTH, DA_WIDTH, 2 * ML_WIDTH, ML_WIDTH, ML_WIDTH, 4 * ML_HEADS,
                   CV_WIDTH, CV_WIDTH, CV_WIDTH, 3 * d])
    quarter = DA_QK // 4
    wqk = w_in[:, :, o[0]:o[2]].reshape(depth, d, 2 * DA_HEADS, 2, 2, 2, quarter)
    wqk = wqk.transpose(0, 1, 2, 5, 3, 4, 6).reshape(depth, d, 2 * DA_WIDTH)
    wda_in = jnp.concatenate([wqk, w_in[:, :, o[2]:o[3]]], axis=-1).astype(BF16)
    wml_in = w_in[:, :, o[3]:o[6]].astype(BF16)
    wmg_in = jnp.pad(w_in[:, :, o[6]:o[7]], ((0, 0), (0, 0), (0, LANES - 4 * ML_HEADS))).astype(BF16)
    wcv_in = w_in[:, :, o[7]:o[10]].astype(BF16)
    wgt_in = w_in[:, :, o[10]:o[11]].astype(BF16)
    gate_b = jnp.pad(ml_gate_b.reshape(depth, 1, 4 * ML_HEADS), ((0, 0), (0, 0), (0, LANES - 4 * ML_HEADS)))
    lam_p = da_lam
    w_da_b, w_ml_b, w_cv_b, w_o_b = (a.astype(BF16) for a in (w_da, w_ml, w_cv, w_o))
    wq_b = peer_wq.astype(BF16)
    keys_b = peer_keys.astype(BF16)
    u_b = _pack_table(peer_u, transpose=False)
    vt_b = _pack_table(peer_v, transpose=True)
    cos, sin = _rope_tables(seq, t)

    xl, xc, xc_tile0 = x, ctx, 0
    for l in range(depth):
        need_ctx = l < depth - 1
        lam_init = 0.8 - 0.6 * math.exp(-0.3 * l)
        da, ml, mg, cv, gt = _inproj(xl, xc, xc_tile0, t, ada[l], norm1_g[l].reshape(1, d), cos, sin,
                                     wda_in[l], wml_in[l], wmg_in[l], wcv_in[l], wgt_in[l], n_lat_tiles)
        yda = _attention(da, lam_p[l], da_subln_g[l], lam_init, ATTN_TILE, 0, seq // ATTN_TILE, 0, t)
        yda_c = _attention(da, lam_p[l], da_subln_g[l], lam_init, n_ctx, seq, 1, seq, n_ctx) if need_ctx else yda
        yml = _mlstm(ml, mg, ml_conv_w[l], gate_b[l], ml_norm_g[l], seq // ML_CHUNK, n_ctx // ML_CHUNK)
        n_out_tiles = n_tiles if need_ctx else n_lat_tiles
        x1, h2 = _merge(xl, xc, xc_tile0, t, ada[l], norm2_g[l].reshape(1, d), yda, yda_c, yml, cv, gt, cv_conv_w[l],
                        w_da_b[l], w_ml_b[l], w_cv_b[l], w_o_b[l], n_lat_tiles, n_out_tiles)
        t_out = n_out_tiles * tm
        xs = _peer(x1.reshape(bsz * t_out, d), h2.reshape(bsz * t_out, d), ada[l], final_g.reshape(1, d),
                   wq_b[l], keys_b[l], u_b, vt_b, l, n_out_tiles, n_lat_tiles, bsz, final=not need_ctx)
        xs = xs.reshape(bsz, t_out, d)
        xl, xc, xc_tile0 = xs, xs, n_lat_tiles
    return xs
```

```python
import functools
import math

import numpy as np
import jax
import jax.numpy as jnp
from jax import lax
from jax.experimental import pallas as pl
from jax.experimental.pallas import tpu as pltpu

EPS = 1e-6
GRID_W = 64
DA_HEADS = 4
DA_QK = 64
DA_V = 2 * DA_QK
DA_WIDTH = DA_HEADS * DA_V
ROPE_BASE = 10000.0
ML_HEADS = 4
ML_DH = 128
ML_WIDTH = ML_HEADS * ML_DH
ML_CHUNK = 128
CV_WIDTH = 512
PEER_HEADS = 8
PEER_TOPK = 16
PEER_GROUP = 2

LANES = 128
SUBLANES = 8
ATTN_TILE = 1024
TOK_TILE = 256
HALO = 16
PEER_TILE = 512
PEER_CHUNK_I1 = 8
PEER_SUBSTEPS = 2
PACK_ROWS = 1024
PEER_MM_SLICES = 2
PEER_SORT_W = 2 * LANES
VMEM_LIMIT = 56 * 1024 * 1024

NEG_BIG = -1e30
BF16 = jnp.bfloat16
F32 = jnp.float32
HIGHEST = lax.Precision.HIGHEST


def _cparams(sem):
    return pltpu.CompilerParams(dimension_semantics=sem, vmem_limit_bytes=VMEM_LIMIT)


def _dot(a, b):
    return jnp.dot(a, b, preferred_element_type=F32)


def _dot_nt(a, b):
    return lax.dot_general(a, b, (((1,), (1,)), ((), ())), preferred_element_type=F32)


def _dot_tn(a, b):
    return lax.dot_general(a, b, (((0,), (0,)), ((), ())), preferred_element_type=F32)


def _rms(x, g):
    return x * lax.rsqrt(jnp.mean(x * x, axis=-1, keepdims=True) + EPS) * g


def _sigmoid(x):
    return 1.0 / (1.0 + jnp.exp(-x))


def _log_sigmoid(x):
    return jnp.minimum(x, 0.0) - jnp.log1p(jnp.exp(-jnp.abs(x)))


def _ada_kernel(c_ref, w_ref, b_ref, o_ref):
    c = c_ref[...]
    s = c * _sigmoid(c)
    o_ref[0] = jnp.dot(s, w_ref[0], preferred_element_type=F32, precision=HIGHEST) + b_ref[0]


def _ada(cc, w_ada, b_ada):
    depth, d, d6 = w_ada.shape
    rows = cc.shape[0]
    nb = d6 // d
    return pl.pallas_call(
        _ada_kernel,
        grid=(depth, nb),
        in_specs=[pl.BlockSpec((rows, d), lambda l, j: (0, 0)),
                  pl.BlockSpec((1, d, d), lambda l, j: (l, 0, j)),
                  pl.BlockSpec((1, 1, d), lambda l, j: (l, 0, j))],
        out_specs=pl.BlockSpec((1, rows, d), lambda l, j: (l, 0, j)),
        out_shape=jax.ShapeDtypeStruct((depth, rows, d6), F32),
        compiler_params=_cparams(("parallel", "parallel")),
        name="ada",
    )(cc, w_ada, b_ada.reshape(depth, 1, d6))


def _inproj_kernel(n_lat_tiles, n_batch, xl_ref, xc_ref, ada_ref, g_ref, cos_ref, sin_ref,
                   wda_ref, wml_ref, wmg_ref, wcv_ref, wgt_ref,
                   da_ref, ml_ref, mg_ref, cv_ref, gt_ref):
    b = pl.program_id(0)
    i = pl.program_id(1)
    d = xl_ref.shape[-1]
    is_lat = i < n_lat_tiles
    row = jnp.where(is_lat, b, n_batch)
    shift = ada_ref[pl.ds(row, 1), 0:d]
    scale = ada_ref[pl.ds(row, 1), d:2 * d]
    x = jnp.where(is_lat, xl_ref[0], xc_ref[0])
    h = _rms(x, g_ref[...]) * (1.0 + scale) + shift
    hb = h.astype(BF16)

    zda = _dot(hb, wda_ref[...])
    cos = cos_ref[...]
    sin = sin_ref[...]
    for j in range(2 * DA_HEADS):
        blk = zda[:, j * LANES:(j + 1) * LANES]
        rot = blk * cos + pltpu.roll(blk, LANES // 2, 1) * sin
        da_ref[0, :, j * LANES:(j + 1) * LANES] = rot.astype(BF16)
    da_ref[0, :, 2 * DA_WIDTH:] = zda[:, 2 * DA_WIDTH:].astype(BF16)

    ml_ref[0] = _dot(hb, wml_ref[...]).astype(BF16)
    mg_ref[0] = _dot(hb, wmg_ref[...])

    zcv = _dot(hb, wcv_ref[...])
    cv_ref[0, :, 0:CV_WIDTH] = zcv[:, 0:CV_WIDTH].astype(BF16)
    cv_ref[0, :, CV_WIDTH:] = (zcv[:, CV_WIDTH:2 * CV_WIDTH] * zcv[:, 2 * CV_WIDTH:]).astype(BF16)

    for j in range(3):
        zg = _dot(hb, wgt_ref[:, j * d:(j + 1) * d])
        gt_ref[0, :, j * d:(j + 1) * d] = _sigmoid(zg).astype(BF16)


def _stream_specs(tm, d, n_lat_tiles, n_tiles, xc_tile0):
    n_ctx_tiles = n_tiles - n_lat_tiles
    lat = pl.BlockSpec((1, tm, d), lambda b, i: (b, jnp.minimum(i, n_lat_tiles - 1), 0))
    ctx = pl.BlockSpec((1, tm, d), lambda b, i: (b, xc_tile0 + jnp.clip(i - n_lat_tiles, 0, n_ctx_tiles - 1), 0))
    return lat, ctx


def _inproj(xl, xc, xc_tile0, t, ada, g, cos, sin, wda, wml, wmg, wcv, wgt, n_lat_tiles):
    bsz, _, d = xl.shape
    tm = TOK_TILE
    full = lambda arr: pl.BlockSpec(arr.shape, lambda b, i: (0,) * arr.ndim, pipeline_mode=pl.Buffered(1))
    tok = lambda w: pl.BlockSpec((1, tm, w), lambda b, i: (b, i, 0))
    lat_spec, ctx_spec = _stream_specs(tm, d, n_lat_tiles, t // tm, xc_tile0)
    widths =(wda.shape[1], wml.shape[1], wmg.shape[1], 2 * CV_WIDTH, wgt.shape[1])
    dts = (BF16, BF16, F32, BF16, BF16)
    return pl.pallas_call(
        functools.partial(_inproj_kernel, n_lat_tiles, bsz),
        grid=(bsz, t // tm),
        in_specs=[lat_spec, ctx_spec, full(ada), full(g),
                  pl.BlockSpec((tm, LANES), lambda b, i: (i, 0)),
                  pl.BlockSpec((tm, LANES), lambda b, i: (i, 0)),
                  full(wda), full(wml), full(wmg), full(wcv), full(wgt)],
        out_specs=[tok(w) for w in widths],
        out_shape=[jax.ShapeDtypeStruct((bsz, t, w), dt) for w, dt in zip(widths, dts)],
        compiler_params=_cparams(("parallel", "parallel")),
        name="inproj",
    )(xl, xc, ada, g, cos, sin, wda, wml, wmg, wcv, wgt)


def _attn_kernel(lam_init, q_ref, k_ref, v_ref, lam_ref, g_ref, o_ref, va_ref):
    i = pl.program_id(2)
    lv = lam_ref[...]
    lam = (jnp.exp(jnp.sum(lv[0:1] * lv[1:2], axis=-1, keepdims=True))
           - jnp.exp(jnp.sum(lv[2:3] * lv[3:4], axis=-1, keepdims=True)) + lam_init)
    lane = lax.broadcasted_iota(jnp.int32, (1, LANES), 1)
    map0 = (lane % (LANES // 2)) < (DA_QK // 2)

    @pl.when(i == 0)
    def _():
        va_ref[:, 0:LANES] = v_ref[0]
        va_ref[:, LANES:] = jnp.ones((va_ref.shape[0], LANES), BF16)

    k = k_ref[0]
    va = va_ref[...]
    q = q_ref[0].astype(F32) * (DA_QK ** -0.5)
    q0 = jnp.where(map0, q, 0.0).astype(BF16)
    q1 = jnp.where(map0, 0.0, q).astype(BF16)
    s0 = _dot_nt(q0, k)
    s1 = _dot_nt(q1, k)
    e0 = jnp.exp((s0 - jnp.max(s0, axis=-1, keepdims=True)).astype(BF16))
    e1 = jnp.exp((s1 - jnp.max(s1, axis=-1, keepdims=True)).astype(BF16))
    o0 = _dot(e0, va)
    o1 = _dot(e1, va)
    o = o0[:, 0:LANES] / o0[:, LANES:] - lam * (o1[:, 0:LANES] / o1[:, LANES:])
    o_ref[0] = (_rms(o, g_ref[...]) * (1.0 - lam_init)).astype(o_ref.dtype)


def _attention(da, lam_p, subln_g, lam_init, tq, q_row0, n_q, kv_row0, kv_rows):
    bsz = da.shape[0]
    q0 = q_row0 // tq
    kv0 = kv_row0 // kv_rows
    return pl.pallas_call(
        functools.partial(_attn_kernel, lam_init),
        grid=(bsz, DA_HEADS, n_q),
        in_specs=[pl.BlockSpec((1, tq, LANES), lambda b, h, i: (b, q0 + i, h)),
                  pl.BlockSpec((1, kv_rows, LANES), lambda b, h, i: (b, kv0, DA_HEADS + h)),
                  pl.BlockSpec((1, kv_rows, LANES), lambda b, h, i: (b, kv0, 2 * DA_HEADS + h)),
                  pl.BlockSpec(lam_p.shape, lambda b, h, i: (0, 0)),
                  pl.BlockSpec((1, LANES), lambda b, h, i: (0, 0))],
        out_specs=pl.BlockSpec((1, tq, LANES), lambda b, h, i: (b, i, h)),
        out_shape=jax.ShapeDtypeStruct((bsz, n_q * tq, DA_WIDTH), BF16),
        scratch_shapes=[pltpu.VMEM((kv_rows, 2 * LANES), BF16)],
        compiler_params=_cparams(("parallel", "parallel", "arbitrary")),
        name="diffattn",
    )(da, da, da, lam_p, subln_g.reshape(1, LANES))


def _mlstm_kernel(ncl, ncc, ml_ref, mg_ref, cw_ref, gb_ref, ng_ref, o_ref,
                  qs_ref, ks_ref, hf_ref, hb_ref, c_ref, n_ref, m_ref):
    L = ML_CHUNK
    nc = ncl + ncc
    t_all = nc * L
    seq = ncl * L
    w = ML_WIDTH

    trow = lax.broadcasted_iota(jnp.int32, (t_all, 1), 0)
    first = (trow == 0) | (trow == seq)
    last = (trow == seq - 1) | (trow == t_all - 1)
    for j in range(2 * ML_HEADS):
        x = ml_ref[0, :, j * LANES:(j + 1) * LANES].astype(F32)
        xm = jnp.where(first, 0.0, pltpu.roll(x, 1, 0))
        xp = jnp.where(last, 0.0, pltpu.roll(x, t_all - 1, 0))
        cw = cw_ref[:, j * LANES:(j + 1) * LANES]
        y = xm * cw[0:1] + x * cw[1:2] + xp * cw[2:3]
        y = y * _sigmoid(y)
        if j < ML_HEADS:
            qs_ref[:, j * LANES:(j + 1) * LANES] = y.astype(BF16)
        else:
            jj = j - ML_HEADS
            ks_ref[:, jj * LANES:(jj + 1) * LANES] = (y * (ML_DH ** -0.5)).astype(BF16)

    c_ref[...] = jnp.zeros_like(c_ref)
    n_ref[...] = jnp.zeros_like(n_ref)
    m_ref[...] = jnp.zeros_like(m_ref)

    r_i = lax.broadcasted_iota(jnp.int32, (L, L), 0)
    c_i = lax.broadcasted_iota(jnp.int32, (L, L), 1)
    lower = r_i >= c_i
    upper = r_i <= c_i
    lower_f = lower.astype(F32)
    upper_f = upper.astype(F32)
    ones_blk = jnp.ones((L, LANES), BF16)

    def step(it, carry):
        dirs = []
        for dr in range(2):
            if dr == 0:
                ch = (it + ncl) % nc
                seen, tri_col, tri_row, end, h_ref = lower, lower_f, upper_f, L - 1, hf_ref
            else:
                ch = nc - 1 - it
                seen, tri_col, tri_row, end, h_ref = upper, upper_f, lower_f, 0, hb_ref
            r0 = pl.multiple_of(ch * L, L)
            g = mg_ref[0, pl.ds(r0, L), :] + gb_ref[...]
            gt = g.T
            bcol_all = jnp.dot(tri_col, _log_sigmoid(g), preferred_element_type=F32, precision=HIGHEST)
            brow_all = jnp.dot(_log_sigmoid(gt), tri_row, preferred_element_type=F32, precision=HIGHEST)
            dirs.append((seen, end, h_ref, r0, g, gt, bcol_all, brow_all))

        rec = []
        for dr in range(2):
            seen, end, h_ref, r0, g, gt, bcol_all, brow_all = dirs[dr]
            for h in range(ML_HEADS):
                icol = (2 * dr) * ML_HEADS + h
                fcol = (2 * dr + 1) * ML_HEADS + h
                idx = dr * ML_HEADS + h
                rec.append(dict(
                    seen=seen, end=end, h_ref=h_ref, r0=r0, idx=idx, cols=slice(h * LANES, (h + 1) * LANES),
                    q=qs_ref[pl.ds(r0, L), h * LANES:(h + 1) * LANES],
                    k=ks_ref[pl.ds(r0, L), h * LANES:(h + 1) * LANES],
                    v=ml_ref[0, pl.ds(r0, L), 2 * w + h * LANES:2 * w + (h + 1) * LANES],
                    b_col=bcol_all[:, fcol:fcol + 1], ig_col=g[:, icol:icol + 1],
                    b_row=brow_all[fcol:fcol + 1, :], ig_row=gt[icol:icol + 1, :],
                    m_prev=m_ref[idx][:, 0:1], cmat=c_ref[idx], nrow=n_ref[idx]))

        for c in rec:
            c["b_cb"] = jnp.broadcast_to(c["b_col"], (L, L))
            c["log_d"] = jnp.where(c["seen"], c["b_cb"] - c["b_row"] + c["ig_row"], NEG_BIG)
        for c in rec:
            c["m_path"] = c["b_col"] + c["m_prev"]
            c["m_t"] = jnp.maximum(c["m_path"], jnp.max(c["log_d"], axis=-1, keepdims=True))
        for c in rec:
            c["m_tb"] = jnp.broadcast_to(c["m_t"], (L, L))
            c["dmat"] = jnp.exp(c["log_d"] - c["m_tb"])
            c["carry_b"] = jnp.exp((c["b_cb"] + c["m_prev"]) - c["m_tb"])
        for c in rec:
            wmat = _dot_nt(c["q"], c["k"]) * c["dmat"]
            wv = _dot(wmat.astype(BF16), jnp.concatenate([c["v"], ones_blk], axis=1))
            qn = _dot_nt(c["q"], jnp.broadcast_to(c["nrow"], (L, LANES)).astype(BF16))
            num = wv[:, 0:LANES] + c["carry_b"] * _dot(c["q"], c["cmat"].astype(BF16))
            den = wv[:, LANES:] + c["carry_b"] * qn
            hval = num / jnp.maximum(jnp.abs(den), jnp.exp(-c["m_tb"]))
            c["h_ref"][pl.ds(c["r0"], L), c["cols"]] = hval
        for c in rec:
            b_end = c["b_col"][c["end"]:c["end"] + 1, :]
            log_w = b_end - c["b_col"] + c["ig_col"]
            m_new = jnp.maximum(b_end + c["m_prev"], jnp.max(log_w, axis=0, keepdims=True))
            c["wk"] = jnp.exp(log_w - m_new)
            c["decay"] = jnp.exp(b_end + c["m_prev"] - m_new)
            c["m_new"] = m_new
        for c in rec:
            kw = c["k"].astype(F32) * c["wk"]
            idx = c["idx"]
            c_ref[idx] = c["decay"] * c["cmat"] + _dot_tn(kw.astype(BF16), c["v"])
            n_ref[idx] = c["decay"] * c["nrow"] + jnp.sum(kw, axis=0, keepdims=True)
            m_ref[idx] = jnp.broadcast_to(c["m_new"], (1, LANES))
        return carry

    lax.fori_loop(0, nc, step, 0)

    for h in range(ML_HEADS):
        sl = slice(h * LANES, (h + 1) * LANES)
        hs = hf_ref[:, sl] + hb_ref[:, sl]
        y = _rms(hs, ng_ref[:, sl])
        og = ml_ref[0, :, 3 * w + h * LANES:3 * w + (h + 1) * LANES].astype(F32)
        o_ref[0, :, sl] = (y * _sigmoid(og)).astype(o_ref.dtype)


def _mlstm(ml, mg, conv_w, gate_b_row, norm_g, ncl, ncc):
    bsz, t, wtot = ml.shape
    nstate = 2 * ML_HEADS
    return pl.pallas_call(
        functools.partial(_mlstm_kernel, ncl, ncc),
        grid=(bsz,),
        in_specs=[pl.BlockSpec((1, t, wtot), lambda b: (b, 0, 0)),
                  pl.BlockSpec((1, t, LANES), lambda b: (b, 0, 0)),
                  pl.BlockSpec(conv_w.shape, lambda b: (0, 0)),
                  pl.BlockSpec((1, LANES), lambda b: (0, 0)),
                  pl.BlockSpec((1, ML_WIDTH), lambda b: (0, 0))],
        out_specs=pl.BlockSpec((1, t, ML_WIDTH), lambda b: (b, 0, 0)),
        out_shape=jax.ShapeDtypeStruct((bsz, t, ML_WIDTH), BF16),
        scratch_shapes=[pltpu.VMEM((t, ML_WIDTH), BF16), pltpu.VMEM((t, ML_WIDTH), BF16),
                        pltpu.VMEM((t, ML_WIDTH), F32), pltpu.VMEM((t, ML_WIDTH), F32),
                        pltpu.VMEM((nstate, ML_DH, ML_DH), F32),
                        pltpu.VMEM((nstate, 1, ML_DH), F32),
                        pltpu.VMEM((nstate, 1, LANES), F32)],
        compiler_params=_cparams(("parallel",)),
        name="mlstm",
    )(ml, mg, conv_w, gate_b_row, norm_g.reshape(1, ML_WIDTH))


def _merge_kernel(n_lat_tiles, n_batch, xl_ref, xc_ref, ada_ref, g2_ref, ydal_ref, ydac_ref, yml_ref, cv_ref, pprev_ref, pnext_ref,
                  gt_ref, cw_ref, wda_ref, wml_ref, wcv_ref, wo_ref, x1_ref, h2_ref):
    b = pl.program_id(0)
    i = pl.program_id(1)
    n_tiles = pl.num_programs(1)
    d = xl_ref.shape[-1]
    tm = xl_ref.shape[1]
    is_lat = i < n_lat_tiles
    row = jnp.where(is_lat, b, n_batch)

    bg = cv_ref[0, :, 0:CV_WIDTH].astype(F32)
    p = cv_ref[0, :, CV_WIDTH:].astype(F32)
    seq_start = (i == 0) | (i == n_lat_tiles)
    seq_end = (i == n_lat_tiles - 1) | (i == n_tiles - 1)
    prev_row = jnp.where(seq_start, 0.0, pprev_ref[0, HALO - 1:HALO, :].astype(F32))
    next_row = jnp.where(seq_end, 0.0, pnext_ref[0, 0:1, :].astype(F32))
    trow = lax.broadcasted_iota(jnp.int32, (tm, 1), 0)
    pm = jnp.where(trow == 0, prev_row, pltpu.roll(p, 1, 0))
    pp = jnp.where(trow == tm - 1, next_row, pltpu.roll(p, tm - 1, 0))
    cw = cw_ref[...]
    ycv = bg * (pm * cw[0:1] + p * cw[1:2] + pp * cw[2:3])

    yda = jnp.where(is_lat, ydal_ref[0], ydac_ref[0])
    m = (gt_ref[0, :, 0:d].astype(F32) * _dot(yda, wda_ref[...])
         + gt_ref[0, :, d:2 * d].astype(F32) * _dot(yml_ref[0], wml_ref[...])
         + gt_ref[0, :, 2 * d:].astype(F32) * _dot(ycv.astype(BF16), wcv_ref[...]))
    out = _dot(m.astype(BF16), wo_ref[...])
    g1 = ada_ref[pl.ds(row, 1), 2 * d:3 * d]
    x1 = jnp.where(is_lat, xl_ref[0], xc_ref[0]) + g1 * out
    x1_ref[0] = x1
    sh2 = ada_ref[pl.ds(row, 1), 3 * d:4 * d]
    sc2 = ada_ref[pl.ds(row, 1), 4 * d:5 * d]
    h2_ref[0] = (_rms(x1, g2_ref[...]) * (1.0 + sc2) + sh2).astype(BF16)


def _merge(xl, xc, xc_tile0, t, ada, g2, yda, yda_c, yml, cv, gt, cv_w, wda, wml, wcv, wo, n_lat_tiles, n_tiles):
    bsz, _, d = xl.shape
    tm = TOK_TILE
    lat_spec, ctx_spec = _stream_specs(tm, d, n_lat_tiles, t // tm, xc_tile0)
    yl_spec, yc_spec = _stream_specs(tm, DA_WIDTH, n_lat_tiles, t // tm, 0)
    r = tm // HALO
    nrb = t // HALO
    full = lambda arr: pl.BlockSpec(arr.shape, lambda b, i: (0,) * arr.ndim, pipeline_mode=pl.Buffered(1))
    tok = lambda w: pl.BlockSpec((1, tm, w), lambda b, i: (b, i, 0))
    return pl.pallas_call(
        functools.partial(_merge_kernel, n_lat_tiles, bsz),
        grid=(bsz, n_tiles),
        in_specs=[lat_spec, ctx_spec, full(ada), full(g2), yl_spec, yc_spec, tok(ML_WIDTH), tok(2 * CV_WIDTH),
                  pl.BlockSpec((1, HALO, CV_WIDTH), lambda b, i: (b, jnp.maximum(i * r - 1, 0), 1)),
                  pl.BlockSpec((1, HALO, CV_WIDTH), lambda b, i: (b, jnp.minimum((i + 1) * r, nrb - 1), 1)),
                  tok(3 * d), full(cv_w), full(wda), full(wml), full(wcv), full(wo)],
        out_specs=[tok(d), tok(d)],
        out_shape=[jax.ShapeDtypeStruct((bsz, n_tiles * tm, d), F32),
                   jax.ShapeDtypeStruct((bsz, n_tiles * tm, d), BF16)],
        compiler_params=_cparams(("parallel", "parallel")),
        name="merge",
    )(xl, xc, ada, g2, yda, yda_c, yml, cv, cv, cv, gt, cv_w, wda, wml, wcv, wo)


def _cmpx(v, i, j):
    a, b = v[i], v[j]
    v[i] = jnp.maximum(a, b)
    v[j] = jnp.minimum(a, b)


def _bitonic_merge_desc(v):
    n = len(v)
    j = n // 2
    while j >= 1:
        for i in range(n):
            l = i ^ j
            if l > i:
                _cmpx(v, i, l)
        j //= 2


def _bitonic_sort_desc(v):
    n = len(v)
    k = 2
    while k <= n:
        j = k // 2
        while j >= 1:
            for i in range(n):
                l = i ^ j
                if l > i:
                    if (i & k) == 0:
                        _cmpx(v, i, l)
                    else:
                        _cmpx(v, l, i)
            j //= 2
        k *= 2


def _merge_top(a, b):
    n = len(a)
    c = [jnp.maximum(a[r], b[n - 1 - r]) for r in range(n)]
    _bitonic_merge_desc(c)
    return c


def _top16_sorted(s):
    nk, tm = s.shape
    k = PEER_TOPK
    groups = nk // k
    assert groups == SUBLANES
    a = s.reshape(k, SUBLANES, tm)
    v = [a[r] for r in range(k)]
    _bitonic_sort_desc(v)
    sh = SUBLANES // 2
    while sh >= 1:
        v = _merge_top(v, [pltpu.roll(x, sh, 0) for x in v])
        sh //= 2
    return v


def _erf(x):
    return lax.erf(x)


def _peer_kernel(tiles_per_batch, n_lat_tiles, n_batch, n_i1, final, x_ref, h_ref, ada_ref, fg_ref, wq_ref, keys_ref,
                 u_ref, vt_ref, o_ref, s1_ref, s2_ref, f1_ref, thr_ref, e2_ref, t1_ref, t2_ref, cr_ref, rz_ref,
                 at0_ref, at1_ref, w0_ref, w1_ref, acc_ref, ht_ref):
    t = pl.program_id(0)
    s = pl.program_id(1)
    n_steps = pl.num_programs(1)
    n_chunks = (n_steps - 1) * PEER_SUBSTEPS
    d = x_ref.shape[-1]
    tm = x_ref.shape[0]
    nk = keys_ref.shape[2]
    dsub = keys_ref.shape[3]
    k = PEER_TOPK
    sort_w = PEER_SORT_W
    at_refs = (at0_ref, at1_ref)
    w_refs = (w0_ref, w1_ref)

    @pl.when(s == 0)
    def _():
        hb = h_ref[...]
        for half in range(tm // TOK_TILE):
            cs = slice(half * TOK_TILE, (half + 1) * TOK_TILE)
            ht_ref[:, cs] = hb[cs, :].astype(F32).T.astype(BF16)
        q = _dot(hb, wq_ref[...]).astype(BF16)
        for h in range(PEER_HEADS):
            for p in range(2):
                col = (h * 2 + p) * dsub
                (s1_ref if p == 0 else s2_ref)[h] = _dot_nt(keys_ref[h, p], q[:, col:col + dsub])
        neg = jnp.full((SUBLANES, sort_w), -jnp.inf, F32)
        for blk in range(tm // sort_w):
            ls = slice(blk * sort_w, (blk + 1) * sort_w)

            def sort_head(h, carry):
                for sref, tref in ((s1_ref, t1_ref), (s2_ref, t2_ref)):
                    v = _top16_sorted(sref[h, :, ls])
                    for r in range(k):
                        tref[blk, r, pl.ds(h, 1), :] = v[r][0:1, :]
                return carry

            lax.fori_loop(0, PEER_HEADS, sort_head, 0)
            t1 = [t1_ref[blk, r] for r in range(k)]
            t2 = [t2_ref[blk, r] for r in range(k)]
            rows = [[t1[r1] + t2[r2] for r2 in range(k // (r1 + 1))] for r1 in range(k)]
            best = rows[0]
            rest = [x for rw in rows[1:] for x in rw]
            while rest:
                grp, rest = rest[:k], rest[k:]
                grp = grp + [neg] * (k - len(grp))
                _bitonic_sort_desc(grp)
                best = _merge_top(best, grp)
            mx = best[0]
            z = jnp.exp(best[0] - mx)
            for r in range(1, k):
                z = z + jnp.exp(best[r] - mx)
            rz = 1.0 / z
            tau = best[k - 1]
            for r1 in range(k):
                thr = jnp.full((SUBLANES, sort_w), jnp.inf, F32)
                for r2 in range(k // (r1 + 1)):
                    thr = jnp.minimum(thr, jnp.where(rows[r1][r2] >= tau, t2[r2], jnp.inf))
                cr_ref[blk, r1] = thr
            rz_ref[blk] = rz

            def finish_head(h, carry):
                hs = pl.ds(h, 1)
                s1h = s1_ref[h, :, ls]
                th = jnp.full_like(s1h, jnp.inf)
                for r in reversed(range(k)):
                    th = jnp.where(s1h == t1_ref[blk, r, hs, :], cr_ref[blk, r, hs, :], th)
                thr_ref[h, :, ls] = th
                f1_ref[h, :, ls] = jnp.exp(s1h - t1_ref[blk, 0, hs, :]) * rz_ref[blk, hs, :]
                e2_ref[h, :, ls] = jnp.exp(s2_ref[h, :, ls] - t2_ref[blk, 0, hs, :])
                return carry

            lax.fori_loop(0, PEER_HEADS, finish_head, 0)
        acc_ref[...] = jnp.zeros_like(acc_ref)

    ec = n_i1 * nk
    rb_rows = nk // 4
    n_grp = n_i1 // PEER_GROUP
    g_rows = ec // n_grp
    gpm = n_grp // PEER_MM_SLICES
    d_rows = d // PEER_MM_SLICES
    e_rows = ec // PEER_MM_SLICES

    def gate_group(grp, at_oth, w_oth, cb):
        g_off = pl.multiple_of(grp * g_rows, g_rows)
        i1s = [cb * n_i1 + grp * PEER_GROUP + jj for jj in range(PEER_GROUP)]
        throws = [[thr_ref[h, pl.ds(i1, 1), :] for i1 in i1s] for h in range(PEER_HEADS)]
        f1rows = [[f1_ref[h, pl.ds(i1, 1), :] for i1 in i1s] for h in range(PEER_HEADS)]
        for tb in range(tm // LANES):
            ts = slice(tb * LANES, (tb + 1) * LANES)
            for rb in range(nk // rb_rows):
                rs = slice(rb * rb_rows, (rb + 1) * rb_rows)
                gs = [jnp.zeros((rb_rows, LANES), F32) for _ in range(PEER_GROUP)]
                for h in range(PEER_HEADS):
                    s2b = s2_ref[h, rs, ts]
                    e2b = e2_ref[h, rs, ts]
                    for jj in range(PEER_GROUP):
                        thr = throws[h][jj][:, ts]
                        f1r = f1rows[h][jj][:, ts]
                        gs[jj] = gs[jj] + jnp.where(s2b >= thr, e2b * f1r, 0.0)
                for jj in range(PEER_GROUP):
                    r0 = pl.multiple_of(g_off + jj * nk + rb * rb_rows, rb_rows)
                    a = at_oth[pl.ds(r0, rb_rows), ts].astype(BF16)
                    act = (0.5 * a) * (1.0 + _erf(a * (2.0 ** -0.5)))
                    w_oth[pl.ds(r0, rb_rows), ts] = act * gs[jj].astype(BF16)

    def substeps(do1, do2, do3):
        for sub in range(PEER_SUBSTEPS):
            at_cur, at_oth = at_refs[sub], at_refs[1 - sub]
            w_cur, w_oth = w_refs[sub], w_refs[1 - sub]
            cb = s * PEER_SUBSTEPS + sub - 1

            def mm_slice(ms, carry, sub=sub, at_cur=at_cur, at_oth=at_oth, w_cur=w_cur, w_oth=w_oth, cb=cb):
                if do1[sub]:
                    e_off = pl.multiple_of(ms * e_rows, e_rows)
                    u_off = pl.multiple_of((sub * ec + e_off) // 2, e_rows // 2)
                    u_rows = pltpu.bitcast(u_ref[pl.ds(u_off, e_rows // 2), :], BF16)
                    at_cur[pl.ds(e_off, e_rows), :] = _dot(u_rows, ht_ref[...])
                if do3[sub]:
                    d_off = pl.multiple_of(ms * d_rows, d_rows)
                    v_off = pl.multiple_of(d_off // 2, d_rows // 2)
                    v_rows = pltpu.bitcast(vt_ref[pl.ds(v_off, d_rows // 2), sub * ec:(sub + 1) * ec], BF16)
                    acc_ref[pl.ds(d_off, d_rows), :] += _dot(v_rows, w_cur[...])
                if do2[sub]:
                    for gi in range(gpm):
                        gate_group(ms * gpm + gi, at_oth, w_oth, cb)
                return carry

            lax.fori_loop(0, PEER_MM_SLICES, mm_slice, 0)

    first = s == 0
    last = s == n_steps - 1

    @pl.when(first)
    def _():
        substeps(do1=(True, True), do2=(False, True), do3=(False, False))

    @pl.when(jnp.logical_not(first | last))
    def _():
        substeps(do1=(True, True), do2=(True, True), do3=(True, True))

    @pl.when(last)
    def _():
        substeps(do1=(False, False), do2=(True, False), do3=(True, True))

    @pl.when(s == n_steps - 1)
    def _():
        for half in range(tm // TOK_TILE):
            g256 = t * (tm // TOK_TILE) + half
            row = jnp.where(g256 % tiles_per_batch < n_lat_tiles, g256 // tiles_per_batch, n_batch)
            g2 = ada_ref[pl.ds(row, 1), 5 * d:6 * d]
            rs = slice(half * TOK_TILE, (half + 1) * TOK_TILE)
            x2 = x_ref[rs, :] + g2 * acc_ref[:, rs].T
            if final:
                x2 = _rms(x2, fg_ref[...])
            o_ref[rs, :] = x2


def _peer(x1, h2, ada, final_g, wq, keys, u, vt, layer, tiles_per_batch, n_lat_tiles, n_batch, final):
    rows, d = x1.shape
    tm = PEER_TILE
    nk = keys.shape[2]
    n_i1 = PEER_CHUNK_I1
    ec = n_i1 * nk
    n_blocks = 2 * u.shape[1] // (PEER_SUBSTEPS * ec)
    n_sort = tm // PEER_SORT_W
    assert PEER_HEADS == SUBLANES
    full = lambda arr: pl.BlockSpec(arr.shape, lambda t, s: (0,) * arr.ndim, pipeline_mode=pl.Buffered(1))
    tok = pl.BlockSpec((tm, d), lambda t, s: (t, 0))
    return pl.pallas_call(
        functools.partial(_peer_kernel, tiles_per_batch, n_lat_tiles, n_batch, n_i1, final),
        grid=(rows // tm, n_blocks + 1),
        in_specs=[tok, tok, full(ada), full(final_g), full(wq), full(keys),
                  pl.BlockSpec((None, PEER_SUBSTEPS * ec // 2, d),
                               lambda t, s: (layer, jnp.minimum(s, n_blocks - 1), 0)),
                  pl.BlockSpec((None, d // 2, PEER_SUBSTEPS * ec), lambda t, s: (layer, 0, jnp.maximum(s - 1, 0)))],
        out_specs=tok,
        out_shape=jax.ShapeDtypeStruct((rows, d), F32),
        scratch_shapes=[pltpu.VMEM((PEER_HEADS, nk, tm), F32), pltpu.VMEM((PEER_HEADS, nk, tm), F32),
                        pltpu.VMEM((PEER_HEADS, nk, tm), F32), pltpu.VMEM((PEER_HEADS, nk, tm), F32),
                        pltpu.VMEM((PEER_HEADS, nk, tm), F32),
                        pltpu.VMEM((n_sort, PEER_TOPK, PEER_HEADS, PEER_SORT_W), F32),
                        pltpu.VMEM((n_sort, PEER_TOPK, PEER_HEADS, PEER_SORT_W), F32),
                        pltpu.VMEM((n_sort, PEER_TOPK, PEER_HEADS, PEER_SORT_W), F32),
                        pltpu.VMEM((n_sort, PEER_HEADS, PEER_SORT_W), F32),
                        pltpu.VMEM((ec, tm), F32), pltpu.VMEM((ec, tm), F32),
                        pltpu.VMEM((ec, tm), BF16), pltpu.VMEM((ec, tm), BF16),
                        pltpu.VMEM((d, tm), F32), pltpu.VMEM((d, tm), BF16)],
        compiler_params=_cparams(("parallel", "arbitrary")),
        name="peer",
    )(x1, h2, ada, final_g, wq, keys, u, vt)


def _rope_tables(seq, t):
    quarter = DA_QK // 4
    inv = ROPE_BASE ** (-np.arange(quarter, dtype=np.float32) / quarter)
    pos = np.arange(seq)
    ang_r = (pos // GRID_W).astype(np.float32)[:, None] * inv[None, :]
    ang_c = (pos % GRID_W).astype(np.float32)[:, None] * inv[None, :]
    half = np.concatenate([ang_r, ang_c, ang_r, ang_c], axis=1)
    cos = np.ones((t, LANES), np.float32)
    sin = np.zeros((t, LANES), np.float32)
    cos[:seq] = np.concatenate([np.cos(half), np.cos(half)], axis=1)
    sin[:seq] = np.concatenate([-np.sin(half), np.sin(half)], axis=1)
    return jnp.asarray(cos), jnp.asarray(sin)


def _pack_kernel(transpose, x_ref, o_ref):
    x = x_ref[0]
    if transpose:
        x = x.T
    o_ref[0] = pltpu.bitcast(x.astype(BF16), jnp.uint32)


def _pack_table(tab, transpose):
    depth, r, c = tab.shape
    rb = PACK_ROWS
    if transpose:
        in_spec = pl.BlockSpec((1, rb, c), lambda l, j: (l, j, 0))
        out_spec = pl.BlockSpec((1, c // 2, rb), lambda l, j: (l, 0, j))
        out_shape = (depth, c // 2, r)
    else:
        in_spec = pl.BlockSpec((1, rb, c), lambda l, j: (l, j, 0))
        out_spec = pl.BlockSpec((1, rb // 2, c), lambda l, j: (l, j, 0))
        out_shape = (depth, r // 2, c)
    return pl.pallas_call(
        functools.partial(_pack_kernel, transpose),
        grid=(depth, r // rb),
        in_specs=[in_spec],
        out_specs=out_spec,
        out_shape=jax.ShapeDtypeStruct(out_shape, jnp.uint32),
        compiler_params=_cparams(("parallel", "parallel")),
        name="pack_t" if transpose else "pack",
    )(tab)


def kernel(x, c, ctx, c_ctx, w_ada, b_ada, norm1_g, norm2_g, w_in, ml_gate_b, ml_conv_w, ml_norm_g, cv_conv_w,
           da_lam, da_subln_g, w_da, w_ml, w_cv, w_o, peer_wq, peer_keys, peer_u, peer_v, final_g):
    bsz, seq, d = x.shape
    n_ctx = ctx.shape[1]
    t = seq + n_ctx
    depth = w_ada.shape[0]
    tm = TOK_TILE
    assert seq % tm == 0 and n_ctx % tm == 0 and seq % ML_CHUNK == 0 and n_ctx % ML_CHUNK == 0
    assert (bsz * t) % PEER_TILE == 0 and (bsz * seq) % PEER_TILE == 0
    assert seq % ATTN_TILE == 0 and seq % n_ctx == 0
    n_lat_tiles = seq // tm
    n_tiles = t // tm

    rows = -(-(bsz + 1) // SUBLANES) * SUBLANES
    cc = jnp.zeros((rows, d), F32).at[:bsz].set(c).at[bsz].set(c_ctx)
    ada = _ada(cc, w_ada, b_ada)

    o = np.cumsum([0, DA_WIDTH, DA_WIDTH, DA_WIDTH, 2 * ML_WIDTH, ML_WIDTH, ML_WIDTH, 4 * ML_HEADS,
                   CV_WIDTH, CV_WIDTH, CV_WIDTH, 3 * d])
    quarter = DA_QK // 4
    wqk = w_in[:, :, o[0]:o[2]].reshape(depth, d, 2 * DA_HEADS, 2, 2, 2, quarter)
    wqk = wqk.transpose(0, 1, 2, 5, 3, 4, 6).reshape(depth, d, 2 * DA_WIDTH)
    wda_in = jnp.concatenate([wqk, w_in[:, :, o[2]:o[3]]], axis=-1).astype(BF16)
    wml_in = w_in[:, :, o[3]:o[6]].astype(BF16)
    wmg_in = jnp.pad(w_in[:, :, o[6]:o[7]], ((0, 0), (0, 0), (0, LANES - 4 * ML_HEADS))).astype(BF16)
    wcv_in = w_in[:, :, o[7]:o[10]].astype(BF16)
    wgt_in = w_in[:, :, o[10]:o[11]].astype(BF16)
    gate_b = jnp.pad(ml_gate_b.reshape(depth, 1, 4 * ML_HEADS), ((0, 0), (0, 0), (0, LANES - 4 * ML_HEADS)))
    lam_p = da_lam
    w_da_b, w_ml_b, w_cv_b, w_o_b = (a.astype(BF16) for a in (w_da, w_ml, w_cv, w_o))
    wq_b = peer_wq.astype(BF16)
    keys_b = peer_keys.astype(BF16)
    u_b = _pack_table(peer_u, transpose=False)
    vt_b = _pack_table(peer_v, transpose=True)
    cos, sin = _rope_tables(seq, t)

    xl, xc, xc_tile0 = x, ctx, 0
    for l in range(depth):
        need_ctx = l < depth - 1
        lam_init = 0.8 - 0.6 * math.exp(-0.3 * l)
        da, ml, mg, cv, gt = _inproj(xl, xc, xc_tile0, t, ada[l], norm1_g[l].reshape(1, d), cos, sin,
                                     wda_in[l], wml_in[l], wmg_in[l], wcv_in[l], wgt_in[l], n_lat_tiles)
        yda = _attention(da, lam_p[l], da_subln_g[l], lam_init, ATTN_TILE, 0, seq // ATTN_TILE, 0, t)
        yda_c = _attention(da, lam_p[l], da_subln_g[l], lam_init, n_ctx, seq, 1, seq, n_ctx) if need_ctx else yda
        yml = _mlstm(ml, mg, ml_conv_w[l], gate_b[l], ml_norm_g[l], seq // ML_CHUNK, n_ctx // ML_CHUNK)
        n_out_tiles = n_tiles if need_ctx else n_lat_tiles
        x1, h2 = _merge(xl, xc, xc_tile0, t, ada[l], norm2_g[l].reshape(1, d), yda, yda_c, yml, cv, gt, cv_conv_w[l],
                        w_da_b[l], w_ml_b[l], w_cv_b[l], w_o_b[l], n_lat_tiles, n_out_tiles)
        t_out = n_out_tiles * tm
        xs = _peer(x1.reshape(bsz * t_out, d), h2.reshape(bsz * t_out, d), ada[l], final_g.reshape(1, d),
                   wq_b[l], keys_b[l], u_b, vt_b, l, n_out_tiles, n_lat_tiles, bsz, final=not need_ctx)
        xs = xs.reshape(bsz, t_out, d)
        xl, xc, xc_tile0 = xs, xs, n_lat_tiles
    return xs
```

```python
import functools
import math

import numpy as np
import jax
import jax.numpy as jnp
from jax import lax
from jax.experimental import pallas as pl
from jax.experimental.pallas import tpu as pltpu

EPS = 1e-6
GRID_W = 64
DA_HEADS = 4
DA_QK = 64
DA_V = 2 * DA_QK
DA_WIDTH = DA_HEADS * DA_V
ROPE_BASE = 10000.0
ML_HEADS = 4
ML_DH = 128
ML_WIDTH = ML_HEADS * ML_DH
ML_CHUNK = 128
CV_WIDTH = 512
PEER_HEADS = 8
PEER_TOPK = 16
PEER_GROUP = 4

LANES = 128
SUBLANES = 8
ATTN_TILE = 1024
TOK_TILE = 256
HALO = 16
PEER_TILE = 512
PEER_CHUNK_I1 = 8
PEER_SUBSTEPS = 2
PACK_ROWS = 1024
PEER_MM_SLICES = 2
PEER_SORT_W = 2 * LANES
VMEM_LIMIT = 56 * 1024 * 1024

NEG_BIG = -1e30
BF16 = jnp.bfloat16
F32 = jnp.float32
HIGHEST = lax.Precision.HIGHEST


def _cparams(sem):
    return pltpu.CompilerParams(dimension_semantics=sem, vmem_limit_bytes=VMEM_LIMIT)


def _dot(a, b):
    return jnp.dot(a, b, preferred_element_type=F32)


def _dot_nt(a, b):
    return lax.dot_general(a, b, (((1,), (1,)), ((), ())), preferred_element_type=F32)


def _dot_tn(a, b):
    return lax.dot_general(a, b, (((0,), (0,)), ((), ())), preferred_element_type=F32)


def _rms(x, g):
    return x * lax.rsqrt(jnp.mean(x * x, axis=-1, keepdims=True) + EPS) * g


def _sigmoid(x):
    return 1.0 / (1.0 + jnp.exp(-x))


def _log_sigmoid(x):
    return jnp.minimum(x, 0.0) - jnp.log1p(jnp.exp(-jnp.abs(x)))


def _ada_kernel(c_ref, w_ref, b_ref, o_ref):
    c = c_ref[...]
    s = c * _sigmoid(c)
    o_ref[0] = jnp.dot(s, w_ref[0], preferred_element_type=F32, precision=HIGHEST) + b_ref[0]


def _ada(cc, w_ada, b_ada):
    depth, d, d6 = w_ada.shape
    rows = cc.shape[0]
    nb = d6 // d
    return pl.pallas_call(
        _ada_kernel,
        grid=(depth, nb),
        in_specs=[pl.BlockSpec((rows, d), lambda l, j: (0, 0)),
                  pl.BlockSpec((1, d, d), lambda l, j: (l, 0, j)),
                  pl.BlockSpec((1, 1, d), lambda l, j: (l, 0, j))],
        out_specs=pl.BlockSpec((1, rows, d), lambda l, j: (l, 0, j)),
        out_shape=jax.ShapeDtypeStruct((depth, rows, d6), F32),
        compiler_params=_cparams(("parallel", "parallel")),
        name="ada",
    )(cc, w_ada, b_ada.reshape(depth, 1, d6))


def _inproj_kernel(n_lat_tiles, n_batch, xl_ref, xc_ref, ada_ref, g_ref, cos_ref, sin_ref,
                   wda_ref, wml_ref, wmg_ref, wcv_ref, wgt_ref,
                   da_ref, ml_ref, mg_ref, cv_ref, gt_ref):
    b = pl.program_id(0)
    i = pl.program_id(1)
    d = xl_ref.shape[-1]
    is_lat = i < n_lat_tiles
    row = jnp.where(is_lat, b, n_batch)
    shift = ada_ref[pl.ds(row, 1), 0:d]
    scale = ada_ref[pl.ds(row, 1), d:2 * d]
    x = jnp.where(is_lat, xl_ref[0], xc_ref[0])
    h = _rms(x, g_ref[...]) * (1.0 + scale) + shift
    hb = h.astype(BF16)

    zda = _dot(hb, wda_ref[...])
    cos = cos_ref[...]
    sin = sin_ref[...]
    for j in range(2 * DA_HEADS):
        blk = zda[:, j * LANES:(j + 1) * LANES]
        rot = blk * cos + pltpu.roll(blk, LANES // 2, 1) * sin
        da_ref[0, :, j * LANES:(j + 1) * LANES] = rot.astype(BF16)
    da_ref[0, :, 2 * DA_WIDTH:] = zda[:, 2 * DA_WIDTH:].astype(BF16)

    ml_ref[0] = _dot(hb, wml_ref[...]).astype(BF16)
    mg_ref[0] = _dot(hb, wmg_ref[...])

    zcv = _dot(hb, wcv_ref[...])
    cv_ref[0, :, 0:CV_WIDTH] = zcv[:, 0:CV_WIDTH].astype(BF16)
    cv_ref[0, :, CV_WIDTH:] = (zcv[:, CV_WIDTH:2 * CV_WIDTH] * zcv[:, 2 * CV_WIDTH:]).astype(BF16)

    for j in range(3):
        zg = _dot(hb, wgt_ref[:, j * d:(j + 1) * d])
        gt_ref[0, :, j * d:(j + 1) * d] = _sigmoid(zg).astype(BF16)


def _stream_specs(tm, d, n_lat_tiles, n_tiles, xc_tile0):
    n_ctx_tiles = n_tiles - n_lat_tiles
    lat = pl.BlockSpec((1, tm, d), lambda b, i: (b, jnp.minimum(i, n_lat_tiles - 1), 0))
    ctx = pl.BlockSpec((1, tm, d), lambda b, i: (b, xc_tile0 + jnp.clip(i - n_lat_tiles, 0, n_ctx_tiles - 1), 0))
    return lat, ctx


def _inproj(xl, xc, xc_tile0, t, ada, g, cos, sin, wda, wml, wmg, wcv, wgt, n_lat_tiles):
    bsz, _, d = xl.shape
    tm = TOK_TILE
    full = lambda arr: pl.BlockSpec(arr.shape, lambda b, i: (0,) * arr.ndim, pipeline_mode=pl.Buffered(1))
    tok = lambda w: pl.BlockSpec((1, tm, w), lambda b, i: (b, i, 0))
    lat_spec, ctx_spec = _stream_specs(tm, d, n_lat_tiles, t // tm, xc_tile0)
    widths =(wda.shape[1], wml.shape[1], wmg.shape[1], 2 * CV_WIDTH, wgt.shape[1])
    dts = (BF16, BF16, F32, BF16, BF16)
    return pl.pallas_call(
        functools.partial(_inproj_kernel, n_lat_tiles, bsz),
        grid=(bsz, t // tm),
        in_specs=[lat_spec, ctx_spec, full(ada), full(g),
                  pl.BlockSpec((tm, LANES), lambda b, i: (i, 0)),
                  pl.BlockSpec((tm, LANES), lambda b, i: (i, 0)),
                  full(wda), full(wml), full(wmg), full(wcv), full(wgt)],
        out_specs=[tok(w) for w in widths],
        out_shape=[jax.ShapeDtypeStruct((bsz, t, w), dt) for w, dt in zip(widths, dts)],
        compiler_params=_cparams(("parallel", "parallel")),
        name="inproj",
    )(xl, xc, ada, g, cos, sin, wda, wml, wmg, wcv, wgt)


def _attn_kernel(lam_init, q_ref, k_ref, v_ref, lam_ref, g_ref, o_ref, va_ref):
    i = pl.program_id(2)
    lv = lam_ref[...]
    lam = (jnp.exp(jnp.sum(lv[0:1] * lv[1:2], axis=-1, keepdims=True))
           - jnp.exp(jnp.sum(lv[2:3] * lv[3:4], axis=-1, keepdims=True)) + lam_init)
    lane = lax.broadcasted_iota(jnp.int32, (1, LANES), 1)
    map0 = (lane % (LANES // 2)) < (DA_QK // 2)

    @pl.when(i == 0)
    def _():
        va_ref[:, 0:LANES] = v_ref[0]
        va_ref[:, LANES:] = jnp.ones((va_ref.shape[0], LANES), BF16)

    k = k_ref[0]
    va = va_ref[...]
    q = q_ref[0].astype(F32) * (DA_QK ** -0.5)
    q0 = jnp.where(map0, q, 0.0).astype(BF16)
    q1 = jnp.where(map0, 0.0, q).astype(BF16)
    s0 = _dot_nt(q0, k)
    s1 = _dot_nt(q1, k)
    e0 = jnp.exp((s0 - jnp.max(s0, axis=-1, keepdims=True)).astype(BF16))
    e1 = jnp.exp((s1 - jnp.max(s1, axis=-1, keepdims=True)).astype(BF16))
    o0 = _dot(e0, va)
    o1 = _dot(e1, va)
    o = o0[:, 0:LANES] / o0[:, LANES:] - lam * (o1[:, 0:LANES] / o1[:, LANES:])
    o_ref[0] = (_rms(o, g_ref[...]) * (1.0 - lam_init)).astype(o_ref.dtype)


def _attention(da, lam_p, subln_g, lam_init, tq, q_row0, n_q, kv_row0, kv_rows):
    bsz = da.shape[0]
    q0 = q_row0 // tq
    kv0 = kv_row0 // kv_rows
    return pl.pallas_call(
        functools.partial(_attn_kernel, lam_init),
        grid=(bsz, DA_HEADS, n_q),
        in_specs=[pl.BlockSpec((1, tq, LANES), lambda b, h, i: (b, q0 + i, h)),
                  pl.BlockSpec((1, kv_rows, LANES), lambda b, h, i: (b, kv0, DA_HEADS + h)),
                  pl.BlockSpec((1, kv_rows, LANES), lambda b, h, i: (b, kv0, 2 * DA_HEADS + h)),
                  pl.BlockSpec(lam_p.shape, lambda b, h, i: (0, 0)),
                  pl.BlockSpec((1, LANES), lambda b, h, i: (0, 0))],
        out_specs=pl.BlockSpec((1, tq, LANES), lambda b, h, i: (b, i, h)),
        out_shape=jax.ShapeDtypeStruct((bsz, n_q * tq, DA_WIDTH), BF16),
        scratch_shapes=[pltpu.VMEM((kv_rows, 2 * LANES), BF16)],
        compiler_params=_cparams(("parallel", "parallel", "arbitrary")),
        name="diffattn",
    )(da, da, da, lam_p, subln_g.reshape(1, LANES))


def _mlstm_kernel(ncl, ncc, ml_ref, mg_ref, cw_ref, gb_ref, ng_ref, o_ref,
                  qs_ref, ks_ref, hf_ref, hb_ref, c_ref, n_ref, m_ref):
    L = ML_CHUNK
    nc = ncl + ncc
    t_all = nc * L
    seq = ncl * L
    w = ML_WIDTH

    trow = lax.broadcasted_iota(jnp.int32, (t_all, 1), 0)
    first = (trow == 0) | (trow == seq)
    last = (trow == seq - 1) | (trow == t_all - 1)
    for j in range(2 * ML_HEADS):
        x = ml_ref[0, :, j * LANES:(j + 1) * LANES].astype(F32)
        xm = jnp.where(first, 0.0, pltpu.roll(x, 1, 0))
        xp = jnp.where(last, 0.0, pltpu.roll(x, t_all - 1, 0))
        cw = cw_ref[:, j * LANES:(j + 1) * LANES]
        y = xm * cw[0:1] + x * cw[1:2] + xp * cw[2:3]
        y = y * _sigmoid(y)
        if j < ML_HEADS:
            qs_ref[:, j * LANES:(j + 1) * LANES] = y.astype(BF16)
        else:
            jj = j - ML_HEADS
            ks_ref[:, jj * LANES:(jj + 1) * LANES] = (y * (ML_DH ** -0.5)).astype(BF16)

    c_ref[...] = jnp.zeros_like(c_ref)
    n_ref[...] = jnp.zeros_like(n_ref)
    m_ref[...] = jnp.zeros_like(m_ref)

    r_i = lax.broadcasted_iota(jnp.int32, (L, L), 0)
    c_i = lax.broadcasted_iota(jnp.int32, (L, L), 1)
    lower = r_i >= c_i
    upper = r_i <= c_i
    lower_f = lower.astype(F32)
    upper_f = upper.astype(F32)
    ones_blk = jnp.ones((L, LANES), BF16)

    def step(it, carry):
        dirs = []
        for dr in range(2):
            if dr == 0:
                ch = (it + ncl) % nc
                seen, tri_col, tri_row, end, h_ref = lower, lower_f, upper_f, L - 1, hf_ref
            else:
                ch = nc - 1 - it
                seen, tri_col, tri_row, end, h_ref = upper, upper_f, lower_f, 0, hb_ref
            r0 = pl.multiple_of(ch * L, L)
            g = mg_ref[0, pl.ds(r0, L), :] + gb_ref[...]
            gt = g.T
            bcol_all = jnp.dot(tri_col, _log_sigmoid(g), preferred_element_type=F32, precision=HIGHEST)
            brow_all = jnp.dot(_log_sigmoid(gt), tri_row, preferred_element_type=F32, precision=HIGHEST)
            dirs.append((seen, end, h_ref, r0, g, gt, bcol_all, brow_all))

        rec = []
        for dr in range(2):
            seen, end, h_ref, r0, g, gt, bcol_all, brow_all = dirs[dr]
            for h in range(ML_HEADS):
                icol = (2 * dr) * ML_HEADS + h
                fcol = (2 * dr + 1) * ML_HEADS + h
                idx = dr * ML_HEADS + h
                rec.append(dict(
                    seen=seen, end=end, h_ref=h_ref, r0=r0, idx=idx, cols=slice(h * LANES, (h + 1) * LANES),
                    q=qs_ref[pl.ds(r0, L), h * LANES:(h + 1) * LANES],
                    k=ks_ref[pl.ds(r0, L), h * LANES:(h + 1) * LANES],
                    v=ml_ref[0, pl.ds(r0, L), 2 * w + h * LANES:2 * w + (h + 1) * LANES],
                    b_col=bcol_all[:, fcol:fcol + 1], ig_col=g[:, icol:icol + 1],
                    b_row=brow_all[fcol:fcol + 1, :], ig_row=gt[icol:icol + 1, :],
                    m_prev=m_ref[idx][:, 0:1], cmat=c_ref[idx], nrow=n_ref[idx]))

        for c in rec:
            c["b_cb"] = jnp.broadcast_to(c["b_col"], (L, L))
            c["log_d"] = jnp.where(c["seen"], c["b_cb"] - c["b_row"] + c["ig_row"], NEG_BIG)
        for c in rec:
            c["m_path"] = c["b_col"] + c["m_prev"]
            c["m_t"] = jnp.maximum(c["m_path"], jnp.max(c["log_d"], axis=-1, keepdims=True))
        for c in rec:
            c["m_tb"] = jnp.broadcast_to(c["m_t"], (L, L))
            c["dmat"] = jnp.exp(c["log_d"] - c["m_tb"])
            c["carry_b"] = jnp.exp((c["b_cb"] + c["m_prev"]) - c["m_tb"])
        for c in rec:
            wmat = _dot_nt(c["q"], c["k"]) * c["dmat"]
            wv = _dot(wmat.astype(BF16), jnp.concatenate([c["v"], ones_blk], axis=1))
            qn = _dot_nt(c["q"], jnp.broadcast_to(c["nrow"], (L, LANES)).astype(BF16))
            num = wv[:, 0:LANES] + c["carry_b"] * _dot(c["q"], c["cmat"].astype(BF16))
            den = wv[:, LANES:] + c["carry_b"] * qn
            hval = num / jnp.maximum(jnp.abs(den), jnp.exp(-c["m_tb"]))
            c["h_ref"][pl.ds(c["r0"], L), c["cols"]] = hval
        for c in rec:
            b_end = c["b_col"][c["end"]:c["end"] + 1, :]
            log_w = b_end - c["b_col"] + c["ig_col"]
            m_new = jnp.maximum(b_end + c["m_prev"], jnp.max(log_w, axis=0, keepdims=True))
            c["wk"] = jnp.exp(log_w - m_new)
            c["decay"] = jnp.exp(b_end + c["m_prev"] - m_new)
            c["m_new"] = m_new
        for c in rec:
            kw = c["k"].astype(F32) * c["wk"]
            idx = c["idx"]
            c_ref[idx] = c["decay"] * c["cmat"] + _dot_tn(kw.astype(BF16), c["v"])
            n_ref[idx] = c["decay"] * c["nrow"] + jnp.sum(kw, axis=0, keepdims=True)
            m_ref[idx] = jnp.broadcast_to(c["m_new"], (1, LANES))
        return carry

    lax.fori_loop(0, nc, step, 0)

    for h in range(ML_HEADS):
        sl = slice(h * LANES, (h + 1) * LANES)
        hs = hf_ref[:, sl] + hb_ref[:, sl]
        y = _rms(hs, ng_ref[:, sl])
        og = ml_ref[0, :, 3 * w + h * LANES:3 * w + (h + 1) * LANES].astype(F32)
        o_ref[0, :, sl] = (y * _sigmoid(og)).astype(o_ref.dtype)


def _mlstm(ml, mg, conv_w, gate_b_row, norm_g, ncl, ncc):
    bsz, t, wtot = ml.shape
    nstate = 2 * ML_HEADS
    return pl.pallas_call(
        functools.partial(_mlstm_kernel, ncl, ncc),
        grid=(bsz,),
        in_specs=[pl.BlockSpec((1, t, wtot), lambda b: (b, 0, 0)),
                  pl.BlockSpec((1, t, LANES), lambda b: (b, 0, 0)),
                  pl.BlockSpec(conv_w.shape, lambda b: (0, 0)),
                  pl.BlockSpec((1, LANES), lambda b: (0, 0)),
                  pl.BlockSpec((1, ML_WIDTH), lambda b: (0, 0))],
        out_specs=pl.BlockSpec((1, t, ML_WIDTH), lambda b: (b, 0, 0)),
        out_shape=jax.ShapeDtypeStruct((bsz, t, ML_WIDTH), BF16),
        scratch_shapes=[pltpu.VMEM((t, ML_WIDTH), BF16), pltpu.VMEM((t, ML_WIDTH), BF16),
                        pltpu.VMEM((t, ML_WIDTH), F32), pltpu.VMEM((t, ML_WIDTH), F32),
                        pltpu.VMEM((nstate, ML_DH, ML_DH), F32),
                        pltpu.VMEM((nstate, 1, ML_DH), F32),
                        pltpu.VMEM((nstate, 1, LANES), F32)],
        compiler_params=_cparams(("parallel",)),
        name="mlstm",
    )(ml, mg, conv_w, gate_b_row, norm_g.reshape(1, ML_WIDTH))


def _merge_kernel(n_lat_tiles, n_batch, xl_ref, xc_ref, ada_ref, g2_ref, ydal_ref, ydac_ref, yml_ref, cv_ref, pprev_ref, pnext_ref,
                  gt_ref, cw_ref, wda_ref, wml_ref, wcv_ref, wo_ref, x1_ref, h2_ref):
    b = pl.program_id(0)
    i = pl.program_id(1)
    n_tiles = pl.num_programs(1)
    d = xl_ref.shape[-1]
    tm = xl_ref.shape[1]
    is_lat = i < n_lat_tiles
    row = jnp.where(is_lat, b, n_batch)

    bg = cv_ref[0, :, 0:CV_WIDTH].astype(F32)
    p = cv_ref[0, :, CV_WIDTH:].astype(F32)
    seq_start = (i == 0) | (i == n_lat_tiles)
    seq_end = (i == n_lat_tiles - 1) | (i == n_tiles - 1)
    prev_row = jnp.where(seq_start, 0.0, pprev_ref[0, HALO - 1:HALO, :].astype(F32))
    next_row = jnp.where(seq_end, 0.0, pnext_ref[0, 0:1, :].astype(F32))
    trow = lax.broadcasted_iota(jnp.int32, (tm, 1), 0)
    pm = jnp.where(trow == 0, prev_row, pltpu.roll(p, 1, 0))
    pp = jnp.where(trow == tm - 1, next_row, pltpu.roll(p, tm - 1, 0))
    cw = cw_ref[...]
    ycv = bg * (pm * cw[0:1] + p * cw[1:2] + pp * cw[2:3])

    yda = jnp.where(is_lat, ydal_ref[0], ydac_ref[0])
    m = (gt_ref[0, :, 0:d].astype(F32) * _dot(yda, wda_ref[...])
         + gt_ref[0, :, d:2 * d].astype(F32) * _dot(yml_ref[0], wml_ref[...])
         + gt_ref[0, :, 2 * d:].astype(F32) * _dot(ycv.astype(BF16), wcv_ref[...]))
    out = _dot(m.astype(BF16), wo_ref[...])
    g1 = ada_ref[pl.ds(row, 1), 2 * d:3 * d]
    x1 = jnp.where(is_lat, xl_ref[0], xc_ref[0]) + g1 * out
    x1_ref[0] = x1
    sh2 = ada_ref[pl.ds(row, 1), 3 * d:4 * d]
    sc2 = ada_ref[pl.ds(row, 1), 4 * d:5 * d]
    h2_ref[0] = (_rms(x1, g2_ref[...]) * (1.0 + sc2) + sh2).astype(BF16)


def _merge(xl, xc, xc_tile0, t, ada, g2, yda, yda_c, yml, cv, gt, cv_w, wda, wml, wcv, wo, n_lat_tiles, n_tiles):
    bsz, _, d = xl.shape
    tm = TOK_TILE
    lat_spec, ctx_spec = _stream_specs(tm, d, n_lat_tiles, t // tm, xc_tile0)
    yl_spec, yc_spec = _stream_specs(tm, DA_WIDTH, n_lat_tiles, t // tm, 0)
    r = tm // HALO
    nrb = t // HALO
    full = lambda arr: pl.BlockSpec(arr.shape, lambda b, i: (0,) * arr.ndim, pipeline_mode=pl.Buffered(1))
    tok = lambda w: pl.BlockSpec((1, tm, w), lambda b, i: (b, i, 0))
    return pl.pallas_call(
        functools.partial(_merge_kernel, n_lat_tiles, bsz),
        grid=(bsz, n_tiles),
        in_specs=[lat_spec, ctx_spec, full(ada), full(g2), yl_spec, yc_spec, tok(ML_WIDTH), tok(2 * CV_WIDTH),
                  pl.BlockSpec((1, HALO, CV_WIDTH), lambda b, i: (b, jnp.maximum(i * r - 1, 0), 1)),
                  pl.BlockSpec((1, HALO, CV_WIDTH), lambda b, i: (b, jnp.minimum((i + 1) * r, nrb - 1), 1)),
                  tok(3 * d), full(cv_w), full(wda), full(wml), full(wcv), full(wo)],
        out_specs=[tok(d), tok(d)],
        out_shape=[jax.ShapeDtypeStruct((bsz, n_tiles * tm, d), F32),
                   jax.ShapeDtypeStruct((bsz, n_tiles * tm, d), BF16)],
        compiler_params=_cparams(("parallel", "parallel")),
        name="merge",
    )(xl, xc, ada, g2, yda, yda_c, yml, cv, cv, cv, gt, cv_w, wda, wml, wcv, wo)


def _cmpx(v, i, j):
    a, b = v[i], v[j]
    v[i] = jnp.maximum(a, b)
    v[j] = jnp.minimum(a, b)


def _bitonic_merge_desc(v):
    n = len(v)
    j = n // 2
    while j >= 1:
        for i in range(n):
            l = i ^ j
            if l > i:
                _cmpx(v, i, l)
        j //= 2


def _bitonic_sort_desc(v):
    n = len(v)
    k = 2
    while k <= n:
        j = k // 2
        while j >= 1:
            for i in range(n):
                l = i ^ j
                if l > i:
                    if (i & k) == 0:
                        _cmpx(v, i, l)
                    else:
                        _cmpx(v, l, i)
            j //= 2
        k *= 2


def _merge_top(a, b):
    n = len(a)
    c = [jnp.maximum(a[r], b[n - 1 - r]) for r in range(n)]
    _bitonic_merge_desc(c)
    return c


def _top16_sorted(s):
    nk, tm = s.shape
    k = PEER_TOPK
    groups = nk // k
    assert groups == SUBLANES
    a = s.reshape(k, SUBLANES, tm)
    v = [a[r] for r in range(k)]
    _bitonic_sort_desc(v)
    sh = SUBLANES // 2
    while sh >= 1:
        v = _merge_top(v, [pltpu.roll(x, sh, 0) for x in v])
        sh //= 2
    return v


def _erf(x):
    return lax.erf(x)


def _peer_kernel(tiles_per_batch, n_lat_tiles, n_batch, n_i1, final, x_ref, h_ref, ada_ref, fg_ref, wq_ref, keys_ref,
                 u_ref, vt_ref, o_ref, s1_ref, s2_ref, f1_ref, thr_ref, e2_ref, t1_ref, t2_ref, cr_ref, rz_ref,
                 at0_ref, at1_ref, w0_ref, w1_ref, acc_ref, ht_ref):
    t = pl.program_id(0)
    s = pl.program_id(1)
    n_steps = pl.num_programs(1)
    n_chunks = (n_steps - 1) * PEER_SUBSTEPS
    d = x_ref.shape[-1]
    tm = x_ref.shape[0]
    nk = keys_ref.shape[2]
    dsub = keys_ref.shape[3]
    k = PEER_TOPK
    sort_w = PEER_SORT_W
    at_refs = (at0_ref, at1_ref)
    w_refs = (w0_ref, w1_ref)

    @pl.when(s == 0)
    def _():
        hb = h_ref[...]
        for half in range(tm // TOK_TILE):
            cs = slice(half * TOK_TILE, (half + 1) * TOK_TILE)
            ht_ref[:, cs] = hb[cs, :].astype(F32).T.astype(BF16)
        q = _dot(hb, wq_ref[...]).astype(BF16)
        for h in range(PEER_HEADS):
            for p in range(2):
                col = (h * 2 + p) * dsub
                (s1_ref if p == 0 else s2_ref)[h] = _dot_nt(keys_ref[h, p], q[:, col:col + dsub])
        neg = jnp.full((SUBLANES, sort_w), -jnp.inf, F32)
        for blk in range(tm // sort_w):
            ls = slice(blk * sort_w, (blk + 1) * sort_w)

            def sort_head(h, carry):
                for sref, tref in ((s1_ref, t1_ref), (s2_ref, t2_ref)):
                    v = _top16_sorted(sref[h, :, ls])
                    for r in range(k):
                        tref[blk, r, pl.ds(h, 1), :] = v[r][0:1, :]
                return carry

            lax.fori_loop(0, PEER_HEADS, sort_head, 0)
            t1 = [t1_ref[blk, r] for r in range(k)]
            t2 = [t2_ref[blk, r] for r in range(k)]
            rows = [[t1[r1] + t2[r2] for r2 in range(k // (r1 + 1))] for r1 in range(k)]
            best = rows[0]
            rest = [x for rw in rows[1:] for x in rw]
            while rest:
                grp, rest = rest[:k], rest[k:]
                grp = grp + [neg] * (k - len(grp))
                _bitonic_sort_desc(grp)
                best = _merge_top(best, grp)
            mx = best[0]
            z = jnp.exp(best[0] - mx)
            for r in range(1, k):
                z = z + jnp.exp(best[r] - mx)
            rz = 1.0 / z
            tau = best[k - 1]
            for r1 in range(k):
                thr = jnp.full((SUBLANES, sort_w), jnp.inf, F32)
                for r2 in range(k // (r1 + 1)):
                    thr = jnp.minimum(thr, jnp.where(rows[r1][r2] >= tau, t2[r2], jnp.inf))
                cr_ref[blk, r1] = thr
            rz_ref[blk] = rz

            def finish_head(h, carry):
                hs = pl.ds(h, 1)
                s1h = s1_ref[h, :, ls]
                th = jnp.full_like(s1h, jnp.inf)
                for r in reversed(range(k)):
                    th = jnp.where(s1h == t1_ref[blk, r, hs, :], cr_ref[blk, r, hs, :], th)
                thr_ref[h, :, ls] = th
                f1_ref[h, :, ls] = jnp.exp(s1h - t1_ref[blk, 0, hs, :]) * rz_ref[blk, hs, :]
                e2_ref[h, :, ls] = jnp.exp(s2_ref[h, :, ls] - t2_ref[blk, 0, hs, :])
                return carry

            lax.fori_loop(0, PEER_HEADS, finish_head, 0)
        acc_ref[...] = jnp.zeros_like(acc_ref)

    ec = n_i1 * nk
    rb_rows = nk // 8
    n_grp = n_i1 // PEER_GROUP
    g_rows = ec // n_grp
    gpm = n_grp // PEER_MM_SLICES
    d_rows = d // PEER_MM_SLICES
    e_rows = ec // PEER_MM_SLICES

    def gate_group(grp, at_oth, w_oth, cb):
        g_off = pl.multiple_of(grp * g_rows, g_rows)
        i1s = [cb * n_i1 + grp * PEER_GROUP + jj for jj in range(PEER_GROUP)]
        throws = [[thr_ref[h, pl.ds(i1, 1), :] for i1 in i1s] for h in range(PEER_HEADS)]
        f1rows = [[f1_ref[h, pl.ds(i1, 1), :] for i1 in i1s] for h in range(PEER_HEADS)]
        for tb in range(tm // LANES):
            ts = slice(tb * LANES, (tb + 1) * LANES)
            for rb in range(nk // rb_rows):
                rs = slice(rb * rb_rows, (rb + 1) * rb_rows)
                gs = [jnp.zeros((rb_rows, LANES), F32) for _ in range(PEER_GROUP)]
                for h in range(PEER_HEADS):
                    s2b = s2_ref[h, rs, ts]
                    e2b = e2_ref[h, rs, ts]
                    for jj in range(PEER_GROUP):
                        thr = throws[h][jj][:, ts]
                        f1r = f1rows[h][jj][:, ts]
                        gs[jj] = gs[jj] + jnp.where(s2b >= thr, e2b * f1r, 0.0)
                for jj in range(PEER_GROUP):
                    r0 = pl.multiple_of(g_off + jj * nk + rb * rb_rows, rb_rows)
                    a = at_oth[pl.ds(r0, rb_rows), ts].astype(BF16)
                    act = (0.5 * a) * (1.0 + _erf(a * (2.0 ** -0.5)))
                    w_oth[pl.ds(r0, rb_rows), ts] = act * gs[jj].astype(BF16)

    def substeps(do1, do2, do3):
        for sub in range(PEER_SUBSTEPS):
            at_cur, at_oth = at_refs[sub], at_refs[1 - sub]
            w_cur, w_oth = w_refs[sub], w_refs[1 - sub]
            cb = s * PEER_SUBSTEPS + sub - 1

            def mm_slice(ms, carry, sub=sub, at_cur=at_cur, at_oth=at_oth, w_cur=w_cur, w_oth=w_oth, cb=cb):
                if do1[sub]:
                    e_off = pl.multiple_of(ms * e_rows, e_rows)
                    u_off = pl.multiple_of((sub * ec + e_off) // 2, e_rows // 2)
                    u_rows = pltpu.bitcast(u_ref[pl.ds(u_off, e_rows // 2), :], BF16)
                    at_cur[pl.ds(e_off, e_rows), :] = _dot(u_rows, ht_ref[...])
                if do3[sub]:
                    d_off = pl.multiple_of(ms * d_rows, d_rows)
                    v_off = pl.multiple_of(d_off // 2, d_rows // 2)
                    v_rows = pltpu.bitcast(vt_ref[pl.ds(v_off, d_rows // 2), sub * ec:(sub + 1) * ec], BF16)
                    acc_ref[pl.ds(d_off, d_rows), :] += _dot(v_rows, w_cur[...])
                if do2[sub]:
                    for gi in range(gpm):
                        gate_group(ms * gpm + gi, at_oth, w_oth, cb)
                return carry

            lax.fori_loop(0, PEER_MM_SLICES, mm_slice, 0)

    first = s == 0
    last = s == n_steps - 1

    @pl.when(first)
    def _():
        substeps(do1=(True, True), do2=(False, True), do3=(False, False))

    @pl.when(jnp.logical_not(first | last))
    def _():
        substeps(do1=(True, True), do2=(True, True), do3=(True, True))

    @pl.when(last)
    def _():
        substeps(do1=(False, False), do2=(True, False), do3=(True, True))

    @pl.when(s == n_steps - 1)
    def _():
        for half in range(tm // TOK_TILE):
            g256 = t * (tm // TOK_TILE) + half
            row = jnp.where(g256 % tiles_per_batch < n_lat_tiles, g256 // tiles_per_batch, n_batch)
            g2 = ada_ref[pl.ds(row, 1), 5 * d:6 * d]
            rs = slice(half * TOK_TILE, (half + 1) * TOK_TILE)
            x2 = x_ref[rs, :] + g2 * acc_ref[:, rs].T
            if final:
                x2 = _rms(x2, fg_ref[...])
            o_ref[rs, :] = x2


def _peer(x1, h2, ada, final_g, wq, keys, u, vt, layer, tiles_per_batch, n_lat_tiles, n_batch, final):
    rows, d = x1.shape
    tm = PEER_TILE
    nk = keys.shape[2]
    n_i1 = PEER_CHUNK_I1
    ec = n_i1 * nk
    n_blocks = 2 * u.shape[1] // (PEER_SUBSTEPS * ec)
    n_sort = tm // PEER_SORT_W
    assert PEER_HEADS == SUBLANES
    full = lambda arr: pl.BlockSpec(arr.shape, lambda t, s: (0,) * arr.ndim, pipeline_mode=pl.Buffered(1))
    tok = pl.BlockSpec((tm, d), lambda t, s: (t, 0))
    return pl.pallas_call(
        functools.partial(_peer_kernel, tiles_per_batch, n_lat_tiles, n_batch, n_i1, final),
        grid=(rows // tm, n_blocks + 1),
        in_specs=[tok, tok, full(ada), full(final_g), full(wq), full(keys),
                  pl.BlockSpec((None, PEER_SUBSTEPS * ec // 2, d),
                               lambda t, s: (layer, jnp.minimum(s, n_blocks - 1), 0)),
                  pl.BlockSpec((None, d // 2, PEER_SUBSTEPS * ec), lambda t, s: (layer, 0, jnp.maximum(s - 1, 0)))],
        out_specs=tok,
        out_shape=jax.ShapeDtypeStruct((rows, d), F32),
        scratch_shapes=[pltpu.VMEM((PEER_HEADS, nk, tm), F32), pltpu.VMEM((PEER_HEADS, nk, tm), F32),
                        pltpu.VMEM((PEER_HEADS, nk, tm), F32), pltpu.VMEM((PEER_HEADS, nk, tm), F32),
                        pltpu.VMEM((PEER_HEADS, nk, tm), F32),
                        pltpu.VMEM((n_sort, PEER_TOPK, PEER_HEADS, PEER_SORT_W), F32),
                        pltpu.VMEM((n_sort, PEER_TOPK, PEER_HEADS, PEER_SORT_W), F32),
                        pltpu.VMEM((n_sort, PEER_TOPK, PEER_HEADS, PEER_SORT_W), F32),
                        pltpu.VMEM((n_sort, PEER_HEADS, PEER_SORT_W), F32),
                        pltpu.VMEM((ec, tm), F32), pltpu.VMEM((ec, tm), F32),
                        pltpu.VMEM((ec, tm), BF16), pltpu.VMEM((ec, tm), BF16),
                        pltpu.VMEM((d, tm), F32), pltpu.VMEM((d, tm), BF16)],
        compiler_params=_cparams(("parallel", "arbitrary")),
        name="peer",
    )(x1, h2, ada, final_g, wq, keys, u, vt)


def _rope_tables(seq, t):
    quarter = DA_QK // 4
    inv = ROPE_BASE ** (-np.arange(quarter, dtype=np.float32) / quarter)
    pos = np.arange(seq)
    ang_r = (pos // GRID_W).astype(np.float32)[:, None] * inv[None, :]
    ang_c = (pos % GRID_W).astype(np.float32)[:, None] * inv[None, :]
    half = np.concatenate([ang_r, ang_c, ang_r, ang_c], axis=1)
    cos = np.ones((t, LANES), np.float32)
    sin = np.zeros((t, LANES), np.float32)
    cos[:seq] = np.concatenate([np.cos(half), np.cos(half)], axis=1)
    sin[:seq] = np.concatenate([-np.sin(half), np.sin(half)], axis=1)
    return jnp.asarray(cos), jnp.asarray(sin)


def _pack_kernel(transpose, x_ref, o_ref):
    x = x_ref[0]
    if transpose:
        x = x.T
    o_ref[0] = pltpu.bitcast(x.astype(BF16), jnp.uint32)


def _pack_table(tab, transpose):
    depth, r, c = tab.shape
    rb = PACK_ROWS
    if transpose:
        in_spec = pl.BlockSpec((1, rb, c), lambda l, j: (l, j, 0))
        out_spec = pl.BlockSpec((1, c // 2, rb), lambda l, j: (l, 0, j))
        out_shape = (depth, c // 2, r)
    else:
        in_spec = pl.BlockSpec((1, rb, c), lambda l, j: (l, j, 0))
        out_spec = pl.BlockSpec((1, rb // 2, c), lambda l, j: (l, j, 0))
        out_shape = (depth, r // 2, c)
    return pl.pallas_call(
        functools.partial(_pack_kernel, transpose),
        grid=(depth, r // rb),
        in_specs=[in_spec],
        out_specs=out_spec,
        out_shape=jax.ShapeDtypeStruct(out_shape, jnp.uint32),
        compiler_params=_cparams(("parallel", "parallel")),
        name="pack_t" if transpose else "pack",
    )(tab)


def kernel(x, c, ctx, c_ctx, w_ada, b_ada, norm1_g, norm2_g, w_in, ml_gate_b, ml_conv_w, ml_norm_g, cv_conv_w,
           da_lam, da_subln_g, w_da, w_ml, w_cv, w_o, peer_wq, peer_keys, peer_u, peer_v, final_g):
    bsz, seq, d = x.shape
    n_ctx = ctx.shape[1]
    t = seq + n_ctx
    depth = w_ada.shape[0]
    tm = TOK_TILE
    assert seq % tm == 0 and n_ctx % tm == 0 and seq % ML_CHUNK == 0 and n_ctx % ML_CHUNK == 0
    assert (bsz * t) % PEER_TILE == 0 and (bsz * seq) % PEER_TILE == 0
    assert seq % ATTN_TILE == 0 and seq % n_ctx == 0
    n_lat_tiles = seq // tm
    n_tiles = t // tm

    rows = -(-(bsz + 1) // SUBLANES) * SUBLANES
    cc = jnp.zeros((rows, d), F32).at[:bsz].set(c).at[bsz].set(c_ctx)
    ada = _ada(cc, w_ada, b_ada)

    o = np.cumsum([0, DA_WIDTH, DA_WIDTH, DA_WIDTH, 2 * ML_WIDTH, ML_WIDTH, ML_WIDTH, 4 * ML_HEADS,
                   CV_WIDTH, CV_WIDTH, CV_WIDTH, 3 * d])
    quarter = DA_QK // 4
    wqk = w_in[:, :, o[0]:o[2]].reshape(depth, d, 2 * DA_HEADS, 2, 2, 2, quarter)
    wqk = wqk.transpose(0, 1, 2, 5, 3, 4, 6).reshape(depth, d, 2 * DA_WIDTH)
    wda_in = jnp.concatenate([wqk, w_in[:, :, o[2]:o[3]]], axis=-1).astype(BF16)
    wml_in = w_in[:, :, o[3]:o[6]].astype(BF16)
    wmg_in = jnp.pad(w_in[:, :, o[6]:o[7]], ((0, 0), (0, 0), (0, LANES - 4 * ML_HEADS))).astype(BF16)
    wcv_in = w_in[:, :, o[7]:o[10]].astype(BF16)
    wgt_in = w_in[:, :, o[10]:o[11]].astype(BF16)
    gate_b = jnp.pad(ml_gate_b.reshape(depth, 1, 4 * ML_HEADS), ((0, 0), (0, 0), (0, LANES - 4 * ML_HEADS)))
    lam_p = da_lam
    w_da_b, w_ml_b, w_cv_b, w_o_b = (a.astype(BF16) for a in (w_da, w_ml, w_cv, w_o))
    wq_b = peer_wq.astype(BF16)
    keys_b = peer_keys.astype(BF16)
    u_b = _pack_table(peer_u, transpose=False)
    vt_b = _pack_table(peer_v, transpose=True)
    cos, sin = _rope_tables(seq, t)

    xl, xc, xc_tile0 = x, ctx, 0
    for l in range(depth):
        need_ctx = l < depth - 1
        lam_init = 0.8 - 0.6 * math.exp(-0.3 * l)
        da, ml, mg, cv, gt = _inproj(xl, xc, xc_tile0, t, ada[l], norm1_g[l].reshape(1, d), cos, sin,
                                     wda_in[l], wml_in[l], wmg_in[l], wcv_in[l], wgt_in[l], n_lat_tiles)
        yda = _attention(da, lam_p[l], da_subln_g[l], lam_init, ATTN_TILE, 0, seq // ATTN_TILE, 0, t)
        yda_c = _attention(da, lam_p[l], da_subln_g[l], lam_init, n_ctx, seq, 1, seq, n_ctx) if need_ctx else yda
        yml = _mlstm(ml, mg, ml_conv_w[l], gate_b[l], ml_norm_g[l], seq // ML_CHUNK, n_ctx // ML_CHUNK)
        n_out_tiles = n_tiles if need_ctx else n_lat_tiles
        x1, h2 = _merge(xl, xc, xc_tile0, t, ada[l], norm2_g[l].reshape(1, d), yda, yda_c, yml, cv, gt, cv_conv_w[l],
                        w_da_b[l], w_ml_b[l], w_cv_b[l], w_o_b[l], n_lat_tiles, n_out_tiles)
        t_out = n_out_tiles * tm
        xs = _peer(x1.reshape(bsz * t_out, d), h2.reshape(bsz * t_out, d), ada[l], final_g.reshape(1, d),
                   wq_b[l], keys_b[l], u_b, vt_b, l, n_out_tiles, n_lat_tiles, bsz, final=not need_ctx)
        xs = xs.reshape(bsz, t_out, d)
        xl, xc, xc_tile0 = xs, xs, n_lat_tiles
    return xs
```
